```python
import math
import jax, jax.numpy as jnp
from jax import lax
import numpy as np

D_MODEL = 4096
BATCH = 2
SEQ = 4096
DEPTH = 4

CHUNK = 64
N_MIXERS = 2
POOL_WINDOWS = (2, 4, 8, 16)
POOL_GROUPS = len(POOL_WINDOWS)
POOL_GROUP_DIM = D_MODEL // POOL_GROUPS
FOX_HEAD_DIM = 128
FOX_HEADS = D_MODEL // FOX_HEAD_DIM
FOX_Q_BLOCK = 128
PEER_HEADS = 8
PEER_N_KEYS = 128
PEER_N_EXPERTS = PEER_N_KEYS * PEER_N_KEYS
PEER_D_KEY = 256
PEER_D_HALF = PEER_D_KEY // 2
PEER_TOPK = 16
PEER_TOKEN_BLOCK = 64
DN_ALPHA = (2 * DEPTH) ** 0.25
DN_BETA = (8 * DEPTH) ** -0.25
LN_EPS = 1e-5
N_MOD = 6
N_POOL_LAYERS = (DEPTH + 1) // 2
N_FOX_LAYERS = DEPTH // 2

kernel_name = "hybrid_pool_fox_peer_deepnorm_adaln"


def layer_norm(x, g, b):
    x32 = x.astype(jnp.float32)
    mu = jnp.mean(x32, axis=-1, keepdims=True)
    xc = x32 - mu
    var = jnp.mean(xc * xc, axis=-1, keepdims=True)
    y = xc * lax.rsqrt(var + LN_EPS) * g.astype(jnp.float32) + b.astype(jnp.float32)
    return y.astype(x.dtype)


def pool_mixer(h, w, scale):
    B, S, D = h.shape
    h32 = h.astype(jnp.float32)
    cs = jnp.concatenate([jnp.zeros((B, 1, D), jnp.float32), jnp.cumsum(h32, axis=1)], axis=1)
    t = jnp.arange(S)
    outs = []
    for gi, win in enumerate(POOL_WINDOWS):
        sl = slice(gi * POOL_GROUP_DIM, (gi + 1) * POOL_GROUP_DIM)
        lo = jnp.maximum(t + 1 - win, 0)
        cnt = (t + 1 - lo).astype(jnp.float32)
        cs_g = cs[:, :, sl]
        pooled = (cs_g[:, 1:] - jnp.take(cs_g, lo, axis=1)) / cnt[None, :, None]
        outs.append(pooled - h32[:, :, sl])
    z = jnp.stack(outs, axis=2).astype(h.dtype)
    y = jnp.einsum('bsgc,gcd->bsgd', z, w).reshape(B, S, D)
    return y * scale


def fox_mixer(h, w_in, b_f, w_o):
    B, S, D = h.shape
    H, dh = FOX_HEADS, FOX_HEAD_DIM
    proj = h @ w_in
    q = proj[..., :D].reshape(B, S, H, dh)
    k = proj[..., D:2 * D].reshape(B, S, H, dh)
    v = proj[..., 2 * D:3 * D].reshape(B, S, H, dh)
    f_logit = proj[..., 3 * D:] + b_f
    log_f = jax.nn.log_sigmoid(f_logit.astype(jnp.float32))
    cum = jnp.cumsum(log_f, axis=1).transpose(0, 2, 1)
    nb = S // FOX_Q_BLOCK
    q_blocks = q.reshape(B, nb, FOX_Q_BLOCK, H, dh).transpose(1, 0, 2, 3, 4)
    cq_blocks = cum.reshape(B, H, nb, FOX_Q_BLOCK).transpose(2, 0, 1, 3)
    pos_blocks = jnp.arange(S).reshape(nb, FOX_Q_BLOCK)
    kpos = jnp.arange(S)
    sm_scale = dh ** -0.5

    def attend(args):
        qb, cqb, qpos = args
        logits = jnp.einsum('bqhd,bkhd->bhqk', qb, k).astype(jnp.float32) * sm_scale
        logits = logits + (cqb[..., :, None] - cum[:, :, None, :])
        mask = kpos[None, :] <= qpos[:, None]
        logits = jnp.where(mask, logits, -jnp.inf)
        p = jax.nn.softmax(logits, axis=-1).astype(v.dtype)
        return jnp.einsum('bhqk,bkhd->bqhd', p, v)

    o = lax.map(attend, (q_blocks, cq_blocks, pos_blocks))
    o = o.transpose(1, 0, 2, 3, 4).reshape(B, S, D)
    return o @ w_o


def peer_ffn(h, w_q, keys, u, v):
    B, S, D = h.shape
    T = B * S
    K, NK = PEER_TOPK, PEER_N_KEYS
    ht = h.reshape(T, D)
    q = (ht @ w_q).reshape(T, PEER_HEADS, 2, PEER_D_HALF)
    s = jnp.einsum('thpc,pnc->thpn', q, keys).astype(jnp.float32)
    top_v, top_i = lax.top_k(s, K)
    cand_v = (top_v[:, :, 0, :, None] + top_v[:, :, 1, None, :]).reshape(T, PEER_HEADS, K * K)
    cand_i = (top_i[:, :, 0, :, None] * NK + top_i[:, :, 1, None, :]).reshape(T, PEER_HEADS, K * K)
    fin_v, fin_pos = lax.top_k(cand_v, K)
    idx = jnp.take_along_axis(cand_i, fin_pos, axis=-1)
    g = jax.nn.softmax(fin_v, axis=-1)
    nblk = T // PEER_TOKEN_BLOCK

    def experts(args):
        hb, ib, gb = args
        a = jnp.einsum('td,thkd->thk', hb, u[ib]).astype(jnp.float32)
        wgt = (gb * jax.nn.gelu(a, approximate=False)).astype(hb.dtype)
        return jnp.einsum('thk,thkd->td', wgt, v[ib])

    y = lax.map(experts, (ht.reshape(nblk, PEER_TOKEN_BLOCK, D),
                          idx.reshape(nblk, PEER_TOKEN_BLOCK, PEER_HEADS, K),
                          g.reshape(nblk, PEER_TOKEN_BLOCK, PEER_HEADS, K)))
    return y.reshape(B, S, D)


def setup_inputs(seed: int = 0) -> dict:
    key = jax.random.key(seed)
    ks = jax.random.split(key, 20)
    D = D_MODEL
    f32 = jnp.float32
    x = jax.random.normal(ks[0], (BATCH, SEQ, D), f32)
    c = jax.random.normal(ks[1], (BATCH, D), f32)
    w_c = jax.random.normal(ks[2], (D, N_MOD * D), f32) * (0.1 * D ** -0.5)
    b_c = jax.random.normal(ks[3], (N_MOD * D,), f32) * 0.01
    ada_table = jax.random.normal(ks[4], (DEPTH, N_MOD * D), f32) * 0.05
    ln_tok_g = 1.0 + 0.02 * jax.random.normal(ks[5], (DEPTH, D), f32)
    ln_tok_b = 0.02 * jax.random.normal(ks[6], (DEPTH, D), f32)
    ln_ch_g = 1.0 + 0.02 * jax.random.normal(ks[7], (DEPTH, D), f32)
    ln_ch_b = 0.02 * jax.random.normal(ks[8], (DEPTH, D), f32)
    pool_w = jax.random.normal(ks[9], (N_POOL_LAYERS, POOL_GROUPS, POOL_GROUP_DIM, POOL_GROUP_DIM), f32) * (DN_BETA * POOL_GROUP_DIM ** -0.5)
    pool_scale = 1.0 + 0.1 * jax.random.normal(ks[10], (N_POOL_LAYERS, D), f32)
    col_scale = jnp.concatenate([jnp.ones((2 * D,), f32), jnp.full((D,), DN_BETA, f32), jnp.ones((FOX_HEADS,), f32)])
    fox_w_in = jax.random.normal(ks[11], (N_FOX_LAYERS, D, 3 * D + FOX_HEADS), f32) * (D ** -0.5) * col_scale
    fox_b_f = jax.random.uniform(ks[12], (N_FOX_LAYERS, FOX_HEADS), f32, minval=1.0, maxval=6.0)
    fox_w_o = jax.random.normal(ks[13], (N_FOX_LAYERS, D, D), f32) * (DN_BETA * D ** -0.5)
    peer_w_q = jax.random.normal(ks[14], (DEPTH, D, PEER_HEADS * PEER_D_KEY), f32) * (D ** -0.5)
    peer_keys = jax.random.normal(ks[15], (DEPTH, 2, PEER_N_KEYS, PEER_D_HALF), f32) * (PEER_D_HALF ** -0.5)
    peer_u = jax.random.normal(ks[16], (DEPTH, PEER_N_EXPERTS, D), f32) * (DN_BETA * D ** -0.5)
    peer_v = jax.random.normal(ks[17], (DEPTH, PEER_N_EXPERTS, D), f32) * (DN_BETA * PEER_HEADS ** -0.5)
    return {"x": x, "c": c, "w_c": w_c, "b_c": b_c, "ada_table": ada_table,
            "ln_tok_g": ln_tok_g, "ln_tok_b": ln_tok_b, "ln_ch_g": ln_ch_g, "ln_ch_b": ln_ch_b,
            "pool_w": pool_w, "pool_scale": pool_scale,
            "fox_w_in": fox_w_in, "fox_b_f": fox_b_f, "fox_w_o": fox_w_o,
            "peer_w_q": peer_w_q, "peer_keys": peer_keys, "peer_u": peer_u, "peer_v": peer_v}


def reference(x, c, w_c, b_c, ada_table, ln_tok_g, ln_tok_b, ln_ch_g, ln_ch_b,
              pool_w, pool_scale, fox_w_in, fox_b_f, fox_w_o,
              peer_w_q, peer_keys, peer_u, peer_v):
    t0 = jax.nn.silu(c) @ w_c + b_c
    for i in range(DEPTH):
        mod = (t0 + ada_table[i])[:, None, :]
        sh_t, sc_t, g_t, sh_c, sc_c, g_c = jnp.split(mod, N_MOD, axis=-1)
        j = i // N_MIXERS
        h = x * (1.0 + sc_t) + sh_t
        if i % N_MIXERS == 0:
            y = pool_mixer(h, pool_w[j], pool_scale[j])
        else:
            y = fox_mixer(h, fox_w_in[j], fox_b_f[j], fox_w_o[j])
        x = layer_norm(DN_ALPHA * x + (1.0 + g_t) * y, ln_tok_g[i], ln_tok_b[i])
        h = x * (1.0 + sc_c) + sh_c
        y = peer_ffn(h, peer_w_q[i], peer_keys[i], peer_u[i], peer_v[i])
        x = layer_norm(DN_ALPHA * x + (1.0 + g_c) * y, ln_ch_g[i], ln_ch_b[i])
    return x
```

```python
import functools

import jax
import jax.numpy as jnp
from jax import lax
from jax.experimental import pallas as pl
from jax.experimental.pallas import tpu as pltpu

F32 = jnp.float32
BF16 = jnp.bfloat16

LN_EPS = 1e-5
N_MOD = 6
POOL_WINDOWS = (2, 4, 8, 16)
POOL_HALO = 16
PEER_TOPK = 16
SQRT_HALF = 0.7071067811865476
VMEM_LIMIT_BYTES = 56 * 1024 * 1024


def _params(*semantics):
    return pltpu.CompilerParams(dimension_semantics=semantics, vmem_limit_bytes=VMEM_LIMIT_BYTES)


def _tile(dim, target, align):
    if dim <= target:
        return dim
    t = (target // align) * align
    while t > align and dim % t:
        t -= align
    assert dim % t == 0, (dim, target, align)
    return t


def _layer_norm(r, g, b):
    mu = jnp.mean(r, axis=-1, keepdims=True)
    xc = r - mu
    var = jnp.mean(xc * xc, axis=-1, keepdims=True)
    return xc * lax.rsqrt(var + LN_EPS) * g + b


def _adaln_kernel(ct_ref, w_ref, b_ref, ada_ref, o_ref, acc_ref, *, nb, nk):
    k = pl.program_id(1)

    @pl.when(k == 0)
    def _():
        acc_ref[...] = jnp.zeros_like(acc_ref)

    s = jax.nn.silu(ct_ref[...])
    w = w_ref[...]
    tk, tn = w.shape
    for b in range(nb):
        prod = w * s[:, b:b + 1]
        acc_ref[b] += jnp.sum(prod.reshape(tk // 8, 8, tn), axis=0)

    @pl.when(k == nk - 1)
    def _():
        for b in range(nb):
            t0 = jnp.sum(acc_ref[b], axis=0, keepdims=True) + b_ref[...]
            o_ref[b] = t0 + ada_ref[...]


def _adaln(c, w_c, b_c, ada_table):
    nb, d = c.shape
    n = w_c.shape[1]
    depth = ada_table.shape[0]
    tk = _tile(d, 512, 8)
    tn = _tile(n, 2048, 128)
    nk = d // tk
    return pl.pallas_call(
        functools.partial(_adaln_kernel, nb=nb, nk=nk),
        grid=(n // tn, nk),
        in_specs=[
            pl.BlockSpec((tk, nb), lambda j, k: (k, 0)),
            pl.BlockSpec((tk, tn), lambda j, k: (k, j)),
            pl.BlockSpec((1, tn), lambda j, k: (0, j)),
            pl.BlockSpec((depth, tn), lambda j, k: (0, j)),
        ],
        out_specs=pl.BlockSpec((nb, depth, tn), lambda j, k: (0, 0, j)),
        out_shape=jax.ShapeDtypeStruct((nb, depth, n), F32),
        scratch_shapes=[pltpu.VMEM((nb, 8, tn), F32)],
        compiler_params=_params("parallel", "arbitrary"),
        name="adaln",
    )(c.T, w_c, b_c.reshape(1, n), ada_table)


def _pool_kernel(x_ref, xp_ref, mod_ref, w_ref, scale_ref, g_ref, b_ref, o_ref, hh_ref, *, alpha, ts, dg):
    s_idx = pl.program_id(1)
    m = mod_ref[0]
    sh, sc, gate = m[0:1], m[1:2], m[2:3]
    x = x_ref[0]
    h = x * (1.0 + sc) + sh
    hp = xp_ref[0] * (1.0 + sc) + sh
    hh_ref[0:POOL_HALO] = jnp.where(s_idx > 0, hp, 0.0)
    hh_ref[POOL_HALO:] = h
    pos = lax.broadcasted_iota(jnp.int32, (ts, 1), 0) + s_idx * ts
    ys = []
    for gi, win in enumerate(POOL_WINDOWS):
        cols = slice(gi * dg, (gi + 1) * dg)
        acc = hh_ref[pl.ds(POOL_HALO, ts), cols]
        for k in range(1, win):
            acc = acc + hh_ref[pl.ds(POOL_HALO - k, ts), cols]
        cnt = jnp.minimum(pos + 1, win).astype(F32)
        z = acc / cnt - h[:, cols]
        ys.append(jnp.dot(z.astype(BF16), w_ref[gi], preferred_element_type=F32))
    y = jnp.concatenate(ys, axis=1) * scale_ref[...]
    r = alpha * x + (1.0 + gate) * y
    o_ref[0] = _layer_norm(r, g_ref[...], b_ref[...])


def _pool_layer(x, mod, w, scale, ln_g, ln_b, alpha):
    nb, s, d = x.shape
    g, dg, _ = w.shape
    ts = _tile(s, 256, POOL_HALO)
    hb = ts // POOL_HALO
    return pl.pallas_call(
        functools.partial(_pool_kernel, alpha=alpha, ts=ts, dg=dg),
        grid=(nb, s // ts),
        in_specs=[
            pl.BlockSpec((1, ts, d), lambda b, i: (b, i, 0)),
            pl.BlockSpec((1, POOL_HALO, d), lambda b, i: (b, jnp.maximum(i * hb - 1, 0), 0)),
            pl.BlockSpec((1, N_MOD, d), lambda b, i: (b, 0, 0)),
            pl.BlockSpec((g, dg, dg), lambda b, i: (0, 0, 0)),
            pl.BlockSpec((1, d), lambda b, i: (0, 0)),
            pl.BlockSpec((1, d), lambda b, i: (0, 0)),
            pl.BlockSpec((1, d), lambda b, i: (0, 0)),
        ],
        out_specs=pl.BlockSpec((1, ts, d), lambda b, i: (b, i, 0)),
        out_shape=jax.ShapeDtypeStruct((nb, s, d), F32),
        scratch_shapes=[pltpu.VMEM((ts + POOL_HALO, d), F32)],
        compiler_params=_params("parallel", "parallel"),
        name="pool_layer",
    )(x, x, mod, w, scale.reshape(1, d), ln_g.reshape(1, d), ln_b.reshape(1, d))


def _fox_proj_kernel(x_ref, mod_ref, w_ref, wfh_ref, wfl_ref, qkv_ref, f_ref, h_ref, *, nq_tiles, sm_scale):
    j = pl.program_id(1)

    @pl.when(j == 0)
    def _():
        m = mod_ref[0]
        h = x_ref[...] * (1.0 + m[1:2]) + m[0:1]
        h_hi = h.astype(BF16)
        h_ref[...] = h_hi
        h_lo = (h - h_hi.astype(F32)).astype(BF16)
        f_ref[...] = (jnp.dot(h_hi, wfh_ref[...], preferred_element_type=F32)
                      + jnp.dot(h_lo, wfh_ref[...], preferred_element_type=F32)
                      + jnp.dot(h_hi, wfl_ref[...], preferred_element_type=F32))

    y = jnp.dot(h_ref[...], w_ref[...], preferred_element_type=F32)
    y = y * jnp.where(j < nq_tiles, sm_scale, 1.0)
    qkv_ref[...] = y.astype(BF16)


def _fox_proj(x2, mod, w_qkv, wf_hi, wf_lo, seq, sm_scale):
    t, d = x2.shape
    n = w_qkv.shape[1]
    hp = wf_hi.shape[1]
    tm = _tile(seq, 512, 16)
    tn = _tile(d, 1024, 128)
    bps = seq // tm
    return pl.pallas_call(
        functools.partial(_fox_proj_kernel, nq_tiles=d // tn, sm_scale=sm_scale),
        grid=(t // tm, n // tn),
        in_specs=[
            pl.BlockSpec((tm, d), lambda i, j: (i, 0)),
            pl.BlockSpec((1, N_MOD, d), lambda i, j: (i // bps, 0, 0)),
            pl.BlockSpec((d, tn), lambda i, j: (0, j)),
            pl.BlockSpec((d, hp), lambda i, j: (0, 0)),
            pl.BlockSpec((d, hp), lambda i, j: (0, 0)),
        ],
        out_specs=[
            pl.BlockSpec((tm, tn), lambda i, j: (i, j)),
            pl.BlockSpec((tm, hp), lambda i, j: (i, 0)),
        ],
        out_shape=[jax.ShapeDtypeStruct((t, n), BF16), jax.ShapeDtypeStruct((t, hp), F32)],
        scratch_shapes=[pltpu.VMEM((tm, d), BF16)],
        compiler_params=_params("parallel", "arbitrary"),
        name="fox_proj",
    )(x2, mod, w_qkv, wf_hi, wf_lo)


def _split3(v):
    p1 = v.astype(BF16)
    r1 = v - p1.astype(F32)
    p2 = r1.astype(BF16)
    p3 = (r1 - p2.astype(F32)).astype(BF16)
    return p1, p2, p3


def _decay_kernel(f_ref, bf_ref, o_ref, *, chunk):
    s = f_ref.shape[1]
    row = lax.broadcasted_iota(jnp.int32, (chunk, chunk), 0)
    col = lax.broadcasted_iota(jnp.int32, (chunk, chunk), 1)
    tri = (col <= row).astype(BF16)
    carry = jnp.zeros((1, f_ref.shape[2]), F32)
    for c in range(s // chunk):
        rows = slice(c * chunk, (c + 1) * chunk)
        lf = jax.nn.log_sigmoid(f_ref[0, rows, :] + bf_ref[...])
        p1, p2, p3 = _split3(lf)
        cs = (jnp.dot(tri, p1, preferred_element_type=F32)
              + jnp.dot(tri, p2, preferred_element_type=F32)
              + jnp.dot(tri, p3, preferred_element_type=F32)) + carry
        o_ref[0, rows, :] = cs
        carry = cs[chunk - 1:chunk, :]


def _decay_cumsum(f, b_f):
    nb, s, hp = f.shape
    chunk = _tile(s, 256, 16)
    return pl.pallas_call(
        functools.partial(_decay_kernel, chunk=chunk),
        grid=(nb,),
        in_specs=[pl.BlockSpec((1, s, hp), lambda b: (b, 0, 0)), pl.BlockSpec((1, hp), lambda b: (0, 0))],
        out_specs=pl.BlockSpec((1, s, hp), lambda b: (b, 0, 0)),
        out_shape=jax.ShapeDtypeStruct((nb, s, hp), F32),
        compiler_params=_params("parallel"),
        name="fox_decay",
    )(f, b_f)


def _attn_kernel(q_ref, k_ref, v_ref, cq_ref, ck_ref, o_ref, m_ref, l_ref, acc_ref, *, tq):
    qi = pl.program_id(2)
    q = q_ref[0]
    cq = cq_ref[0, 0]
    m_ref[...] = jnp.full_like(m_ref, -jnp.inf)
    l_ref[...] = jnp.zeros_like(l_ref)
    acc_ref[...] = jnp.zeros_like(acc_ref)

    def step(off, mask):
        kb = k_ref[0, pl.ds(off, tq), :]
        vb = v_ref[0, pl.ds(off, tq), :]
        ck = ck_ref[0, 0, :, pl.ds(off, tq)]
        u = lax.dot_general(q, kb, (((1,), (1,)), ((), ())), preferred_element_type=F32) - ck
        if mask is not None:
            u = jnp.where(mask, u, -jnp.inf)
        m_old = m_ref[...]
        m_new = jnp.maximum(m_old, jnp.max(u, axis=1, keepdims=True) + cq)
        a = jnp.exp(m_old - m_new)
        p = jnp.exp(u + (cq - m_new))
        l_ref[...] = a * l_ref[...] + jnp.sum(p, axis=1, keepdims=True)
        acc_ref[...] = a * acc_ref[...] + jnp.dot(p.astype(BF16), vb, preferred_element_type=F32)
        m_ref[...] = m_new

    def body(ki, carry):
        step(pl.multiple_of(ki * tq, tq), None)
        return carry

    lax.fori_loop(0, qi, body, 0)
    row = lax.broadcasted_iota(jnp.int32, (tq, tq), 0)
    col = lax.broadcasted_iota(jnp.int32, (tq, tq), 1)
    step(pl.multiple_of(qi * tq, tq), col <= row)
    o_ref[0] = (acc_ref[...] / l_ref[...]).astype(o_ref.dtype)


def _fox_attention(qkv, cq, ck, nheads, dh):
    nb, s, _ = qkv.shape
    d = nheads * dh
    tq = _tile(s, 512, 128)
    return pl.pallas_call(
        functools.partial(_attn_kernel, tq=tq),
        grid=(nb, nheads, s // tq),
        in_specs=[
            pl.BlockSpec((1, tq, dh), lambda b, h, i: (b, i, h)),
            pl.BlockSpec((1, s, dh), lambda b, h, i: (b, 0, nheads + h)),
            pl.BlockSpec((1, s, dh), lambda b, h, i: (b, 0, 2 * nheads + h)),
            pl.BlockSpec((1, 1, tq, 1), lambda b, h, i: (b, h, i, 0)),
            pl.BlockSpec((1, 1, 1, s), lambda b, h, i: (b, h, 0, 0)),
        ],
        out_specs=pl.BlockSpec((1, tq, dh), lambda b, h, i: (b, i, h)),
        out_shape=jax.ShapeDtypeStruct((nb, s, d), BF16),
        scratch_shapes=[pltpu.VMEM((tq, 1), F32), pltpu.VMEM((tq, 1), F32), pltpu.VMEM((tq, dh), F32)],
        compiler_params=_params("parallel", "parallel", "parallel"),
        name="fox_attention",
    )(qkv, qkv, qkv, cq, ck)


def _out_ln_kernel(o_ref, w_ref, x_ref, mod_ref, g_ref, b_ref, out_ref, *, alpha, tn, nj, gate_row):
    j = pl.program_id(1)
    y = jnp.dot(o_ref[...], w_ref[...], preferred_element_type=F32)
    out_ref[:, pl.ds(pl.multiple_of(j * tn, tn), tn)] = y

    @pl.when(j == nj - 1)
    def _():
        gate = mod_ref[0][gate_row:gate_row + 1]
        r = alpha * x_ref[...] + (1.0 + gate) * out_ref[...]
        out_ref[...] = _layer_norm(r, g_ref[...], b_ref[...])


def _fox_out_layer(o2, w_o, x2, mod, ln_g, ln_b, seq, alpha):
    t, d = x2.shape
    tm = _tile(seq, 256, 16)
    tn = _tile(d, 512, 128)
    bps = seq // tm
    nj = d // tn
    return pl.pallas_call(
        functools.partial(_out_ln_kernel, alpha=alpha, tn=tn, nj=nj, gate_row=2),
        grid=(t // tm, nj),
        in_specs=[
            pl.BlockSpec((tm, d), lambda i, j: (i, 0)),
            pl.BlockSpec((d, tn), lambda i, j: (0, j)),
            pl.BlockSpec((tm, d), lambda i, j: (i, 0)),
            pl.BlockSpec((1, N_MOD, d), lambda i, j: (i // bps, 0, 0)),
            pl.BlockSpec((1, d), lambda i, j: (0, 0)),
            pl.BlockSpec((1, d), lambda i, j: (0, 0)),
        ],
        out_specs=pl.BlockSpec((tm, d), lambda i, j: (i, 0)),
        out_shape=jax.ShapeDtypeStruct((t, d), F32),
        compiler_params=_params("parallel", "arbitrary"),
        name="fox_out_layer",
    )(o2, w_o, x2, mod, ln_g.reshape(1, d), ln_b.reshape(1, d))


def _top_ranked(s, k):
    rows = s.shape[0]
    row = lax.broadcasted_iota(jnp.int32, s.shape, 0)
    rank = jnp.full(s.shape, float(k), F32)
    vals = []
    for r in range(k):
        m = jnp.max(s, axis=0, keepdims=True)
        first = jnp.min(jnp.where(s == m, row, rows), axis=0, keepdims=True)
        sel = row == first
        rank = jnp.where(sel, float(r), rank)
        vals.append(m)
        s = jnp.where(sel, -jnp.inf, s)
    return jnp.concatenate(vals, axis=0), rank


def _router_kernel(x_ref, mod_ref, wq_ref, keys_ref, hbf_ref, n1_ref, e1_ref, r2_ref, e2_ref, *, nkeys, topk):
    hd = pl.program_id(1)

    @pl.when(hd == 0)
    def _():
        m = mod_ref[0]
        hbf_ref[...] = (x_ref[...] * (1.0 + m[4:5]) + m[3:4]).astype(BF16)

    q = jnp.dot(hbf_ref[...], wq_ref[...], preferred_element_type=F32).astype(BF16)
    dhalf = q.shape[1] // 2
    scores, tops, ranks = [], [], []
    for p in range(2):
        qp = q[:, p * dhalf:(p + 1) * dhalf]
        s = lax.dot_general(keys_ref[p], qp, (((1,), (1,)), ((), ())), preferred_element_type=F32)
        tv, rk = _top_ranked(s, topk)
        scores.append(s)
        tops.append(tv)
        ranks.append(rk)
    tv1, tv2 = tops

    half = topk // 2
    sub = lax.broadcasted_iota(jnp.int32, (8, 1), 0)
    pieces = [tv1[0:1] + tv2]
    for a in range(1, half):
        pieces.append(jnp.where(sub < topk // (a + 1), tv1[a:a + 1] + tv2[0:8], -jnp.inf))
    pieces.append(tv1[half:topk] + tv2[0:1])
    cand = jnp.concatenate(pieces, axis=0)
    cvals, crank = _top_ranked(cand, topk)
    taken = crank < float(topk)
    z = jnp.sum(jnp.where(taken, jnp.exp(cand - cvals[0:1]), 0.0), axis=0, keepdims=True)
    takenf = taken.astype(F32)

    rank1, rank2 = ranks
    n1 = jnp.zeros_like(rank1)
    start = 0
    for a in range(half):
        size = topk if a == 0 else 8
        n_a = jnp.sum(takenf[start:start + size], axis=0, keepdims=True)
        n1 = jnp.where(rank1 == float(a), n_a, n1)
        start += size
    for a in range(half, topk):
        n1 = jnp.where(rank1 == float(a), takenf[start + a - half:start + a - half + 1], n1)

    n1_ref[0] = n1
    e1_ref[0] = jnp.exp(scores[0] - tv1[0:1]) / z
    r2_ref[0] = rank2
    e2_ref[0] = jnp.exp(scores[1] - tv2[0:1])


def _peer_router(x2, mod, w_q, keys, seq):
    t, d = x2.shape
    _, nkeys, dhalf = keys.shape
    ph = w_q.shape[1] // (2 * dhalf)
    tr = _tile(seq, 256, 128)
    bps = seq // tr
    route = jax.ShapeDtypeStruct((ph, nkeys, t), F32)
    route_spec = pl.BlockSpec((1, nkeys, tr), lambda i, h: (h, 0, i))
    return pl.pallas_call(
        functools.partial(_router_kernel, nkeys=nkeys, topk=PEER_TOPK),
        grid=(t // tr, ph),
        in_specs=[
            pl.BlockSpec((tr, d), lambda i, h: (i, 0)),
            pl.BlockSpec((1, N_MOD, d), lambda i, h: (i // bps, 0, 0)),
            pl.BlockSpec((d, 2 * dhalf), lambda i, h: (0, h)),
            pl.BlockSpec((2, nkeys, dhalf), lambda i, h: (0, 0, 0)),
        ],
        out_specs=[pl.BlockSpec((tr, d), lambda i, h: (i, 0)), route_spec, route_spec, route_spec, route_spec],
        out_shape=[jax.ShapeDtypeStruct((t, d), BF16), route, route, route, route],
        compiler_params=_params("parallel", "arbitrary"),
        name="peer_router",
    )(x2, mod, w_q, keys)


def _expert_kernel(h_ref, u_ref, v_ref, n1_ref, e1_ref, r2_ref, e2_ref, x_ref, mod_ref, g_ref, b_ref, o_ref,
                   *, alpha, nkeys, ph, nj):
    j = pl.program_id(1)

    @pl.when(j == 0)
    def _():
        o_ref[...] = jnp.zeros_like(o_ref)

    a = lax.dot_general(u_ref[...], h_ref[...], (((1,), (1,)), ((), ())), preferred_element_type=F32)
    act = 0.5 * a * (1.0 + lax.erf(a * SQRT_HALF))
    te = a.shape[0]
    groups = te // nkeys
    pieces = []
    for il in range(groups):
        i_key = j * groups + il
        w = jnp.zeros((nkeys, a.shape[1]), F32)
        for hd in range(ph):
            n1 = n1_ref[hd, pl.ds(i_key, 1), :]
            e1 = e1_ref[hd, pl.ds(i_key, 1), :]
            w = w + jnp.where(r2_ref[hd] < n1, e2_ref[hd], 0.0) * e1
        pieces.append(w)
    gate = pieces[0] if groups == 1 else jnp.concatenate(pieces, axis=0)
    wgt = (gate * act).astype(BF16)
    o_ref[...] += lax.dot_general(wgt, v_ref[...], (((0,), (0,)), ((), ())), preferred_element_type=F32)

    @pl.when(j == nj - 1)
    def _():
        g_c = mod_ref[0][5:6]
        r = alpha * x_ref[...] + (1.0 + g_c) * o_ref[...]
        o_ref[...] = _layer_norm(r, g_ref[...], b_ref[...])


def _peer_experts(hbf, u, v, routes, x2, mod, ln_g, ln_b, seq, alpha):
    t, d = x2.shape
    ne = u.shape[0]
    ph, nkeys, _ = routes[0].shape
    tm = _tile(seq, 256, 128)
    te = _tile(ne, 512, nkeys)
    bps = seq // tm
    nj = ne // te
    route_spec = pl.BlockSpec((ph, nkeys, tm), lambda i, j: (0, 0, i))
    return pl.pallas_call(
        functools.partial(_expert_kernel, alpha=alpha, nkeys=nkeys, ph=ph, nj=nj),
        grid=(t // tm, nj),
        in_specs=[
            pl.BlockSpec((tm, d), lambda i, j: (i, 0)),
            pl.BlockSpec((te, d), lambda i, j: (j, 0)),
            pl.BlockSpec((te, d), lambda i, j: (j, 0)),
            route_spec, route_spec, route_spec, route_spec,
            pl.BlockSpec((tm, d), lambda i, j: (i, 0)),
            pl.BlockSpec((1, N_MOD, d), lambda i, j: (i // bps, 0, 0)),
            pl.BlockSpec((1, d), lambda i, j: (0, 0)),
            pl.BlockSpec((1, d), lambda i, j: (0, 0)),
        ],
        out_specs=pl.BlockSpec((tm, d), lambda i, j: (i, 0)),
        out_shape=jax.ShapeDtypeStruct((t, d), F32),
        compiler_params=_params("parallel", "arbitrary"),
        name="peer_experts",
    )(hbf, u, v, *routes, x2, mod, ln_g.reshape(1, d), ln_b.reshape(1, d))


def kernel(x, c, w_c, b_c, ada_table, ln_tok_g, ln_tok_b, ln_ch_g, ln_ch_b, pool_w, pool_scale,
           fox_w_in, fox_b_f, fox_w_o, peer_w_q, peer_keys, peer_u, peer_v):
    nb, seq, d = x.shape
    depth = ada_table.shape[0]
    alpha = float((2 * depth) ** 0.25)
    nheads = fox_b_f.shape[1]
    dh = d // nheads
    hpad = -(-nheads // 128) * 128

    mod_all = _adaln(c, w_c, b_c, ada_table)

    pool_w16 = pool_w.astype(BF16)
    w_qkv16 = fox_w_in[:, :, :3 * d].astype(BF16)
    w_f = jnp.pad(fox_w_in[:, :, 3 * d:], ((0, 0), (0, 0), (0, hpad - nheads)))
    wf_hi = w_f.astype(BF16)
    wf_lo = (w_f - wf_hi.astype(F32)).astype(BF16)
    b_f = jnp.pad(fox_b_f, ((0, 0), (0, hpad - nheads)))
    w_o16 = fox_w_o.astype(BF16)
    w_q16 = peer_w_q.astype(BF16)
    keys16 = peer_keys.astype(BF16)
    u16 = peer_u.astype(BF16)
    v16 = peer_v.astype(BF16)

    n_mixers = 2
    for i in range(depth):
        mod = mod_all[:, i].reshape(nb, N_MOD, d)
        jl = i // n_mixers
        if i % n_mixers == 0:
            x = _pool_layer(x, mod, pool_w16[jl], pool_scale[jl], ln_tok_g[i], ln_tok_b[i], alpha)
        else:
            x2 = x.reshape(nb * seq, d)
            qkv, f = _fox_proj(x2, mod, w_qkv16[jl], wf_hi[jl], wf_lo[jl], seq, float(dh) ** -0.5)
            cum = _decay_cumsum(f.reshape(nb, seq, hpad), b_f[jl].reshape(1, hpad))[:, :, :nheads]
            cum_t = cum.transpose(0, 2, 1)
            o = _fox_attention(qkv.reshape(nb, seq, 3 * d), cum_t[:, :, :, None], cum_t[:, :, None, :], nheads, dh)
            x = _fox_out_layer(o.reshape(nb * seq, d), w_o16[jl], x2, mod, ln_tok_g[i], ln_tok_b[i], seq, alpha)
            x = x.reshape(nb, seq, d)
        x2 = x.reshape(nb * seq, d)
        hbf, *routes = _peer_router(x2, mod, w_q16[i], keys16[i], seq)
        x = _peer_experts(hbf, u16[i], v16[i], routes, x2, mod, ln_ch_g[i], ln_ch_b[i], seq, alpha)
        x = x.reshape(nb, seq, d)
    return x
```

```python
import functools

import jax
import jax.numpy as jnp
from jax import lax
from jax.experimental import pallas as pl
from jax.experimental.pallas import tpu as pltpu

F32 = jnp.float32
BF16 = jnp.bfloat16

LN_EPS = 1e-5
N_MOD = 6
POOL_WINDOWS = (2, 4, 8, 16)
POOL_HALO = 16
PEER_TOPK = 16
SQRT_HALF = 0.7071067811865476
LANES = 128
SUBLANES = 8
VMEM_LIMIT_BYTES = 56 * 1024 * 1024


ONE_BUFFER = dict(pipeline_mode=pl.Buffered(1))


def _params(*semantics):
    return pltpu.CompilerParams(dimension_semantics=semantics, vmem_limit_bytes=VMEM_LIMIT_BYTES)


def _tile(dim, target, align):
    if dim <= target:
        return dim
    t = (target // align) * align
    while t > align and dim % t:
        t -= align
    assert dim % t == 0, (dim, target, align)
    return t


def _layer_norm(r, g, b):
    mu = jnp.mean(r, axis=-1, keepdims=True)
    xc = r - mu
    var = jnp.mean(xc * xc, axis=-1, keepdims=True)
    return xc * lax.rsqrt(var + LN_EPS) * g + b


def _adaln_kernel(ct_ref, w_ref, b_ref, ada_ref, o_ref, acc_ref, *, nb, nk):
    k = pl.program_id(1)

    @pl.when(k == 0)
    def _():
        acc_ref[...] = jnp.zeros_like(acc_ref)

    s = jax.nn.silu(ct_ref[...])
    w = w_ref[...]
    tk, tn = w.shape
    for b in range(nb):
        prod = w * s[:, b:b + 1]
        acc_ref[b] += jnp.sum(prod.reshape(tk // 8, 8, tn), axis=0)

    @pl.when(k == nk - 1)
    def _():
        for b in range(nb):
            t0 = jnp.sum(acc_ref[b], axis=0, keepdims=True) + b_ref[...]
            o_ref[b] = t0 + ada_ref[...]


def _adaln(c, w_c, b_c, ada_table):
    nb, d = c.shape
    n = w_c.shape[1]
    depth = ada_table.shape[0]
    tk = _tile(d, 512, 8)
    tn = _tile(n, 2048, 128)
    nk = d // tk
    return pl.pallas_call(
        functools.partial(_adaln_kernel, nb=nb, nk=nk),
        grid=(n // tn, nk),
        in_specs=[
            pl.BlockSpec((tk, nb), lambda j, k: (k, 0)),
            pl.BlockSpec((tk, tn), lambda j, k: (k, j)),
            pl.BlockSpec((1, tn), lambda j, k: (0, j)),
            pl.BlockSpec((depth, tn), lambda j, k: (0, j)),
        ],
        out_specs=pl.BlockSpec((nb, depth, tn), lambda j, k: (0, 0, j)),
        out_shape=jax.ShapeDtypeStruct((nb, depth, n), F32),
        scratch_shapes=[pltpu.VMEM((nb, 8, tn), F32)],
        compiler_params=_params("parallel", "arbitrary"),
        name="adaln",
    )(c.T, w_c, b_c.reshape(1, n), ada_table)


def _pool_kernel(x_ref, xp_ref, mod_ref, w_ref, scale_ref, g_ref, b_ref, o_ref, hh_ref, *, alpha, ts, dg):
    s_idx = pl.program_id(1)
    m = mod_ref[0]
    sh, sc, gate = m[0:1], m[1:2], m[2:3]
    x = x_ref[0]
    h = x * (1.0 + sc) + sh
    hp = xp_ref[0] * (1.0 + sc) + sh
    hh_ref[0:POOL_HALO] = jnp.where(s_idx > 0, hp, 0.0)
    hh_ref[POOL_HALO:] = h
    pos = lax.broadcasted_iota(jnp.int32, (ts, 1), 0) + s_idx * ts
    ys = []
    for gi, win in enumerate(POOL_WINDOWS):
        cols = slice(gi * dg, (gi + 1) * dg)
        acc = hh_ref[pl.ds(POOL_HALO, ts), cols]
        for k in range(1, win):
            acc = acc + hh_ref[pl.ds(POOL_HALO - k, ts), cols]
        cnt = jnp.minimum(pos + 1, win).astype(F32)
        z = acc / cnt - h[:, cols]
        ys.append(jnp.dot(z.astype(BF16), w_ref[gi], preferred_element_type=F32))
    y = jnp.concatenate(ys, axis=1) * scale_ref[...]
    r = alpha * x + (1.0 + gate) * y
    o_ref[0] = _layer_norm(r, g_ref[...], b_ref[...])


def _pool_layer(x, mod, w, scale, ln_g, ln_b, alpha):
    nb, s, d = x.shape
    g, dg, _ = w.shape
    ts = _tile(s, 256, POOL_HALO)
    hb = ts // POOL_HALO
    return pl.pallas_call(
        functools.partial(_pool_kernel, alpha=alpha, ts=ts, dg=dg),
        grid=(nb, s // ts),
        in_specs=[
            pl.BlockSpec((1, ts, d), lambda b, i: (b, i, 0)),
            pl.BlockSpec((1, POOL_HALO, d), lambda b, i: (b, jnp.maximum(i * hb - 1, 0), 0)),
            pl.BlockSpec((1, N_MOD, d), lambda b, i: (b, 0, 0)),
            pl.BlockSpec((g, dg, dg), lambda b, i: (0, 0, 0)),
            pl.BlockSpec((1, d), lambda b, i: (0, 0)),
            pl.BlockSpec((1, d), lambda b, i: (0, 0)),
            pl.BlockSpec((1, d), lambda b, i: (0, 0)),
        ],
        out_specs=pl.BlockSpec((1, ts, d), lambda b, i: (b, i, 0)),
        out_shape=jax.ShapeDtypeStruct((nb, s, d), F32),
        scratch_shapes=[pltpu.VMEM((ts + POOL_HALO, d), F32)],
        compiler_params=_params("parallel", "parallel"),
        name="pool_layer",
    )(x, x, mod, w, scale.reshape(1, d), ln_g.reshape(1, d), ln_b.reshape(1, d))


def _fox_proj_kernel(x_ref, mod_ref, w_ref, wfh_ref, wfl_ref, qkv_ref, f_ref, h_ref, *, nq_tiles, sm_scale):
    j = pl.program_id(1)

    @pl.when(j == 0)
    def _():
        m = mod_ref[0]
        h = x_ref[...] * (1.0 + m[1:2]) + m[0:1]
        h_hi = h.astype(BF16)
        h_ref[...] = h_hi
        h_lo = (h - h_hi.astype(F32)).astype(BF16)
        f_ref[...] = (jnp.dot(h_hi, wfh_ref[...], preferred_element_type=F32)
                      + jnp.dot(h_lo, wfh_ref[...], preferred_element_type=F32)
                      + jnp.dot(h_hi, wfl_ref[...], preferred_element_type=F32))

    y = jnp.dot(h_ref[...], w_ref[...], preferred_element_type=F32)
    y = y * jnp.where(j < nq_tiles, sm_scale, 1.0)
    qkv_ref[...] = y.astype(BF16)


def _fox_proj(x2, mod, w_qkv, wf_hi, wf_lo, seq, sm_scale):
    t, d = x2.shape
    n = w_qkv.shape[1]
    hp = wf_hi.shape[1]
    tm = _tile(seq, 512, 16)
    tn = _tile(d, 1024, 128)
    bps = seq // tm
    return pl.pallas_call(
        functools.partial(_fox_proj_kernel, nq_tiles=d // tn, sm_scale=sm_scale),
        grid=(t // tm, n // tn),
        in_specs=[
            pl.BlockSpec((tm, d), lambda i, j: (i, 0)),
            pl.BlockSpec((1, N_MOD, d), lambda i, j: (i // bps, 0, 0)),
            pl.BlockSpec((d, tn), lambda i, j: (0, j)),
            pl.BlockSpec((d, hp), lambda i, j: (0, 0)),
            pl.BlockSpec((d, hp), lambda i, j: (0, 0)),
        ],
        out_specs=[
            pl.BlockSpec((tm, tn), lambda i, j: (i, j)),
            pl.BlockSpec((tm, hp), lambda i, j: (i, 0)),
        ],
        out_shape=[jax.ShapeDtypeStruct((t, n), BF16), jax.ShapeDtypeStruct((t, hp), F32)],
        scratch_shapes=[pltpu.VMEM((tm, d), BF16)],
        compiler_params=_params("parallel", "arbitrary"),
        name="fox_proj",
    )(x2, mod, w_qkv, wf_hi, wf_lo)


def _split3(v):
    p1 = v.astype(BF16)
    r1 = v - p1.astype(F32)
    p2 = r1.astype(BF16)
    p3 = (r1 - p2.astype(F32)).astype(BF16)
    return p1, p2, p3


def _decay_kernel(f_ref, bf_ref, o_ref, *, chunk):
    s = f_ref.shape[1]
    row = lax.broadcasted_iota(jnp.int32, (chunk, chunk), 0)
    col = lax.broadcasted_iota(jnp.int32, (chunk, chunk), 1)
    tri = (col <= row).astype(BF16)
    carry = jnp.zeros((1, f_ref.shape[2]), F32)
    for c in range(s // chunk):
        rows = slice(c * chunk, (c + 1) * chunk)
        lf = jax.nn.log_sigmoid(f_ref[0, rows, :] + bf_ref[...])
        p1, p2, p3 = _split3(lf)
        cs = (jnp.dot(tri, p1, preferred_element_type=F32)
              + jnp.dot(tri, p2, preferred_element_type=F32)
              + jnp.dot(tri, p3, preferred_element_type=F32)) + carry
        o_ref[0, rows, :] = cs
        carry = cs[chunk - 1:chunk, :]


def _decay_cumsum(f, b_f):
    nb, s, hp = f.shape
    chunk = _tile(s, 256, 16)
    return pl.pallas_call(
        functools.partial(_decay_kernel, chunk=chunk),
        grid=(nb,),
        in_specs=[pl.BlockSpec((1, s, hp), lambda b: (b, 0, 0)), pl.BlockSpec((1, hp), lambda b: (0, 0))],
        out_specs=pl.BlockSpec((1, s, hp), lambda b: (b, 0, 0)),
        out_shape=jax.ShapeDtypeStruct((nb, s, hp), F32),
        compiler_params=_params("parallel"),
        name="fox_decay",
    )(f, b_f)


def _attn_kernel(q_ref, k_ref, v_ref, cq_ref, ck_ref, o_ref, m_ref, l_ref, acc_ref, cqb_ref, *, tq, rs):
    qi = pl.program_id(2)
    nch = tq // LANES
    m_ref[...] = jnp.full_like(m_ref, -jnp.inf)
    l_ref[...] = jnp.zeros_like(l_ref)
    acc_ref[...] = jnp.zeros_like(acc_ref)
    cqb_ref[...] = jnp.broadcast_to(cq_ref[0, 0], cqb_ref.shape)

    def step(off, masked):
        kb = k_ref[0, pl.ds(off, tq), :]
        vb = v_ref[0, pl.ds(off, tq), :]
        ck = ck_ref[0, 0, :, pl.ds(off, tq)]
        for sb in range(tq // rs):
            rows = slice(sb * rs, (sb + 1) * rs)
            s = lax.dot_general(q_ref[0, rows, :], kb, (((1,), (1,)), ((), ())), preferred_element_type=F32)
            u = []
            for c in range(nch):
                uc = s[:, c * LANES:(c + 1) * LANES] - ck[:, c * LANES:(c + 1) * LANES]
                if masked:
                    row = lax.broadcasted_iota(jnp.int32, (rs, LANES), 0) + sb * rs
                    col = lax.broadcasted_iota(jnp.int32, (rs, LANES), 1) + c * LANES
                    uc = jnp.where(col <= row, uc, -jnp.inf)
                u.append(uc)
            bm = functools.reduce(jnp.maximum, u)
            cqb = cqb_ref[rows, :]
            m_old = m_ref[rows, :]
            m_new = jnp.maximum(m_old, jnp.max(bm, axis=1, keepdims=True) + cqb)
            a = jnp.exp(m_old - m_new)
            r = cqb - m_new
            p = [jnp.exp(uc + r) for uc in u]
            l_ref[rows, :] = a * l_ref[rows, :] + functools.reduce(jnp.add, p)
            pb = jnp.concatenate([pc.astype(BF16) for pc in p], axis=1)
            acc_ref[rows, :] = a * acc_ref[rows, :] + jnp.dot(pb, vb, preferred_element_type=F32)
            m_ref[rows, :] = m_new

    def body(ki, carry):
        step(pl.multiple_of(ki * tq, tq), False)
        return carry

    lax.fori_loop(0, qi, body, 0)
    step(pl.multiple_of(qi * tq, tq), True)
    l = jnp.sum(l_ref[...], axis=1, keepdims=True)
    o_ref[0] = (acc_ref[...] / l).astype(o_ref.dtype)


def _fox_attention(qkv, cq, ck, nheads, dh):
    nb, s, _ = qkv.shape
    d = nheads * dh
    assert dh == LANES, dh
    tq = _tile(s, 1024, LANES)
    rs = _tile(tq, 512, LANES)
    stat = pltpu.VMEM((tq, LANES), F32)
    return pl.pallas_call(
        functools.partial(_attn_kernel, tq=tq, rs=rs),
        grid=(nb, nheads, s // tq),
        in_specs=[
            pl.BlockSpec((1, tq, dh), lambda b, h, i: (b, i, h)),
            pl.BlockSpec((1, s, dh), lambda b, h, i: (b, 0, nheads + h)),
            pl.BlockSpec((1, s, dh), lambda b, h, i: (b, 0, 2 * nheads + h)),
            pl.BlockSpec((1, 1, tq, 1), lambda b, h, i: (b, h, i, 0)),
            pl.BlockSpec((1, 1, 1, s), lambda b, h, i: (b, h, 0, 0)),
        ],
        out_specs=pl.BlockSpec((1, tq, dh), lambda b, h, i: (b, i, h)),
        out_shape=jax.ShapeDtypeStruct((nb, s, d), BF16),
        scratch_shapes=[stat, stat, pltpu.VMEM((tq, dh), F32), stat],
        compiler_params=_params("parallel", "parallel", "parallel"),
        name="fox_attention",
    )(qkv, qkv, qkv, cq, ck)


def _out_ln_kernel(o_ref, w_ref, x_ref, mod_ref, g_ref, b_ref, out_ref, *, alpha, tn, nj, gate_row):
    j = pl.program_id(1)
    y = jnp.dot(o_ref[...], w_ref[...], preferred_element_type=F32)
    out_ref[:, pl.ds(pl.multiple_of(j * tn, tn), tn)] = y

    @pl.when(j == nj - 1)
    def _():
        gate = mod_ref[0][gate_row:gate_row + 1]
        r = alpha * x_ref[...] + (1.0 + gate) * out_ref[...]
        out_ref[...] = _layer_norm(r, g_ref[...], b_ref[...])


def _fox_out_layer(o2, w_o, x2, mod, ln_g, ln_b, seq, alpha):
    t, d = x2.shape
    tm = _tile(seq, 512, 16)
    tn = _tile(d, 512, 128)
    bps = seq // tm
    nj = d // tn
    return pl.pallas_call(
        functools.partial(_out_ln_kernel, alpha=alpha, tn=tn, nj=nj, gate_row=2),
        grid=(t // tm, nj),
        in_specs=[
            pl.BlockSpec((tm, d), lambda i, j: (i, 0)),
            pl.BlockSpec((d, tn), lambda i, j: (0, j)),
            pl.BlockSpec((tm, d), lambda i, j: (i, 0), **ONE_BUFFER),
            pl.BlockSpec((1, N_MOD, d), lambda i, j: (i // bps, 0, 0)),
            pl.BlockSpec((1, d), lambda i, j: (0, 0)),
            pl.BlockSpec((1, d), lambda i, j: (0, 0)),
        ],
        out_specs=pl.BlockSpec((tm, d), lambda i, j: (i, 0), **ONE_BUFFER),
        out_shape=jax.ShapeDtypeStruct((t, d), F32),
        compiler_params=_params("parallel", "arbitrary"),
        name="fox_out_layer",
    )(o2, w_o, x2, mod, ln_g.reshape(1, d), ln_b.reshape(1, d))


def _top_ranked(s, k):
    rows = s.shape[0]
    row = lax.broadcasted_iota(jnp.int32, s.shape, 0)
    rank = jnp.full(s.shape, float(k), F32)
    vals = []
    for r in range(k):
        m = jnp.max(s, axis=0, keepdims=True)
        first = jnp.min(jnp.where(s == m, row, rows), axis=0, keepdims=True)
        sel = row == first
        rank = jnp.where(sel, float(r), rank)
        vals.append(m)
        s = jnp.where(sel, -jnp.inf, s)
    return jnp.concatenate(vals, axis=0), rank


def _router_kernel(x_ref, mod_ref, wq_ref, keys_ref, hbf_ref, n1_ref, e1_ref, r2_ref, e2_ref, *, nkeys, topk):
    hd = pl.program_id(1)

    @pl.when(hd == 0)
    def _():
        m = mod_ref[0]
        hbf_ref[...] = (x_ref[...] * (1.0 + m[4:5]) + m[3:4]).astype(BF16)

    q = jnp.dot(hbf_ref[...], wq_ref[...], preferred_element_type=F32).astype(BF16)
    dhalf = q.shape[1] // 2
    scores, tops, ranks = [], [], []
    for p in range(2):
        qp = q[:, p * dhalf:(p + 1) * dhalf]
        s = lax.dot_general(keys_ref[p], qp, (((1,), (1,)), ((), ())), preferred_element_type=F32)
        tv, rk = _top_ranked(s, topk)
        scores.append(s)
        tops.append(tv)
        ranks.append(rk)
    tv1, tv2 = tops

    half = topk // 2
    sub = lax.broadcasted_iota(jnp.int32, (8, 1), 0)
    pieces = [tv1[0:1] + tv2]
    for a in range(1, half):
        pieces.append(jnp.where(sub < topk // (a + 1), tv1[a:a + 1] + tv2[0:8], -jnp.inf))
    pieces.append(tv1[half:topk] + tv2[0:1])
    cand = jnp.concatenate(pieces, axis=0)
    cvals, crank = _top_ranked(cand, topk)
    taken = crank < float(topk)
    z = jnp.sum(jnp.where(taken, jnp.exp(cand - cvals[0:1]), 0.0), axis=0, keepdims=True)
    takenf = taken.astype(F32)

    rank1, rank2 = ranks
    n1 = jnp.zeros_like(rank1)
    start = 0
    for a in range(half):
        size = topk if a == 0 else 8
        n_a = jnp.sum(takenf[start:start + size], axis=0, keepdims=True)
        n1 = jnp.where(rank1 == float(a), n_a, n1)
        start += size
    for a in range(half, topk):
        n1 = jnp.where(rank1 == float(a), takenf[start + a - half:start + a - half + 1], n1)

    n1_ref[0] = n1
    e1_ref[0] = jnp.exp(scores[0] - tv1[0:1]) / z
    r2_ref[0] = rank2
    e2_ref[0] = jnp.exp(scores[1] - tv2[0:1])


def _peer_router(x2, mod, w_q, keys, seq):
    t, d = x2.shape
    _, nkeys, dhalf = keys.shape
    ph = w_q.shape[1] // (2 * dhalf)
    tr = _tile(seq, 256, 128)
    bps = seq // tr
    route = jax.ShapeDtypeStruct((ph, nkeys, t), F32)
    route_spec = pl.BlockSpec((1, nkeys, tr), lambda i, h: (h, 0, i))
    return pl.pallas_call(
        functools.partial(_router_kernel, nkeys=nkeys, topk=PEER_TOPK),
        grid=(t // tr, ph),
        in_specs=[
            pl.BlockSpec((tr, d), lambda i, h: (i, 0)),
            pl.BlockSpec((1, N_MOD, d), lambda i, h: (i // bps, 0, 0)),
            pl.BlockSpec((d, 2 * dhalf), lambda i, h: (0, h)),
            pl.BlockSpec((2, nkeys, dhalf), lambda i, h: (0, 0, 0)),
        ],
        out_specs=[pl.BlockSpec((tr, d), lambda i, h: (i, 0)), route_spec, route_spec, route_spec, route_spec],
        out_shape=[jax.ShapeDtypeStruct((t, d), BF16), route, route, route, route],
        compiler_params=_params("parallel", "arbitrary"),
        name="peer_router",
    )(x2, mod, w_q, keys)


def _gate_block(a_scr, wt_scr, n1_ref, e1_ref, r2_ref, e2_ref, key0, nkeys, ph):
    ts, tm = a_scr.shape
    for il in range(ts // nkeys):
        i_key = key0 + il
        for lc in range(tm // LANES):
            cols = slice(lc * LANES, (lc + 1) * LANES)
            n1t = n1_ref[i_key, :, cols]
            e1t = e1_ref[i_key, :, cols]
            n1b = [n1t[hd:hd + 1, :] for hd in range(ph)]
            e1b = [e1t[hd:hd + 1, :] for hd in range(ph)]
            tiles = []
            for rc in range(nkeys // SUBLANES):
                rows = slice(rc * SUBLANES, (rc + 1) * SUBLANES)
                w = None
                for hd in range(ph):
                    t = jnp.where(r2_ref[hd, rows, cols] < n1b[hd], e2_ref[hd, rows, cols], 0.0) * e1b[hd]
                    w = t if w is None else w + t
                a = a_scr[il * nkeys + rc * SUBLANES:il * nkeys + (rc + 1) * SUBLANES, cols]
                tiles.append(w * (0.5 * a * (1.0 + lax.erf(a * SQRT_HALF))))
            blk = jnp.concatenate(tiles, axis=0).astype(BF16)
            wt_scr[cols, il * nkeys:(il + 1) * nkeys] = blk.T


def _expert_kernel(h_ref, u_ref, v_ref, n1_ref, e1_ref, r2_ref, e2_ref, x_ref, mod_ref, g_ref, b_ref, o_ref,
                   a_e, a_o, wt_e, wt_o, *, alpha, nkeys, ph, nj, ts):
    j = pl.program_id(1)
    tm, d = o_ref.shape
    gps = ts // nkeys

    @pl.when(j == 0)
    def _():
        o_ref[...] = jnp.zeros_like(o_ref)
        wt_e[...] = jnp.zeros_like(wt_e)
        a_o[...] = jnp.zeros_like(a_o)

    def mm1(rows, a_scr):
        half = tm // 2
        for s in range(2):
            a_scr[:, s * half:(s + 1) * half] = lax.dot_general(
                u_ref[rows, :], h_ref[s * half:(s + 1) * half, :], (((1,), (1,)), ((), ())),
                preferred_element_type=F32)

    def mm2(wt_scr, rows):
        half = d // 2
        for s in range(2):
            o_ref[:, s * half:(s + 1) * half] += jnp.dot(
                wt_scr[...], v_ref[rows, s * half:(s + 1) * half], preferred_element_type=F32)

    lo, hi = slice(0, ts), slice(ts, 2 * ts)
    last = 2 * nj - 1
    mm2(wt_e, lo)
    _gate_block(a_o, wt_o, n1_ref, e1_ref, r2_ref, e2_ref, jnp.maximum(2 * j - 1, 0) * gps, nkeys, ph)
    mm1(lo, a_e)
    mm2(wt_o, hi)
    _gate_block(a_e, wt_e, n1_ref, e1_ref, r2_ref, e2_ref, jnp.minimum(2 * j, last) * gps, nkeys, ph)
    mm1(hi, a_o)

    @pl.when(j == nj)
    def _():
        g_c = mod_ref[0][5:6]
        r = alpha * x_ref[...] + (1.0 + g_c) * o_ref[...]
        o_ref[...] = _layer_norm(r, g_ref[...], b_ref[...])


def _peer_experts(hbf, u, v, n1t, e1t, r2, e2, x2, mod, ln_g, ln_b, seq, alpha):
    t, d = x2.shape
    ne = u.shape[0]
    ph, nkeys, _ = r2.shape
    tm = _tile(seq, 512, 2 * LANES)
    ts = _tile(ne // 2, 256, nkeys)
    bps = seq // tm
    nj = ne // (2 * ts)
    route_spec = pl.BlockSpec((ph, nkeys, tm), lambda i, j: (0, 0, i), **ONE_BUFFER)
    key_spec = pl.BlockSpec((nkeys, ph, tm), lambda i, j: (0, 0, i), **ONE_BUFFER)
    return pl.pallas_call(
        functools.partial(_expert_kernel, alpha=alpha, nkeys=nkeys, ph=ph, nj=nj, ts=ts),
        grid=(t // tm, nj + 1),
        in_specs=[
            pl.BlockSpec((tm, d), lambda i, j: (i, 0), **ONE_BUFFER),
            pl.BlockSpec((2 * ts, d), lambda i, j: (jnp.minimum(j, nj - 1), 0)),
            pl.BlockSpec((2 * ts, d), lambda i, j: (jnp.maximum(j - 1, 0), 0)),
            key_spec, key_spec, route_spec, route_spec,
            pl.BlockSpec((tm, d), lambda i, j: (i, 0), **ONE_BUFFER),
            pl.BlockSpec((1, N_MOD, d), lambda i, j: (i // bps, 0, 0)),
            pl.BlockSpec((1, d), lambda i, j: (0, 0)),
            pl.BlockSpec((1, d), lambda i, j: (0, 0)),
        ],
        out_specs=pl.BlockSpec((tm, d), lambda i, j: (i, 0), **ONE_BUFFER),
        out_shape=jax.ShapeDtypeStruct((t, d), F32),
        scratch_shapes=[pltpu.VMEM((ts, tm), F32), pltpu.VMEM((ts, tm), F32),
                        pltpu.VMEM((tm, ts), BF16), pltpu.VMEM((tm, ts), BF16)],
        compiler_params=_params("parallel", "arbitrary"),
        name="peer_experts",
    )(hbf, u, v, n1t, e1t, r2, e2, x2, mod, ln_g.reshape(1, d), ln_b.reshape(1, d))


def kernel(x, c, w_c, b_c, ada_table, ln_tok_g, ln_tok_b, ln_ch_g, ln_ch_b, pool_w, pool_scale,
           fox_w_in, fox_b_f, fox_w_o, peer_w_q, peer_keys, peer_u, peer_v):
    nb, seq, d = x.shape
    depth = ada_table.shape[0]
    alpha = float((2 * depth) ** 0.25)
    nheads = fox_b_f.shape[1]
    dh = d // nheads
    hpad = -(-nheads // 128) * 128

    mod_all = _adaln(c, w_c, b_c, ada_table)

    pool_w16 = pool_w.astype(BF16)
    w_qkv16 = fox_w_in[:, :, :3 * d].astype(BF16)
    w_f = jnp.pad(fox_w_in[:, :, 3 * d:], ((0, 0), (0, 0), (0, hpad - nheads)))
    wf_hi = w_f.astype(BF16)
    wf_lo = (w_f - wf_hi.astype(F32)).astype(BF16)
    b_f = jnp.pad(fox_b_f, ((0, 0), (0, hpad - nheads)))
    w_o16 = fox_w_o.astype(BF16)
    w_q16 = peer_w_q.astype(BF16)
    keys16 = peer_keys.astype(BF16)
    u16 = peer_u.astype(BF16)
    v16 = peer_v.astype(BF16)

    n_mixers = 2
    for i in range(depth):
        mod = mod_all[:, i].reshape(nb, N_MOD, d)
        jl = i // n_mixers
        if i % n_mixers == 0:
            x = _pool_layer(x, mod, pool_w16[jl], pool_scale[jl], ln_tok_g[i], ln_tok_b[i], alpha)
        else:
            x2 = x.reshape(nb * seq, d)
            qkv, f = _fox_proj(x2, mod, w_qkv16[jl], wf_hi[jl], wf_lo[jl], seq, float(dh) ** -0.5)
            cum = _decay_cumsum(f.reshape(nb, seq, hpad), b_f[jl].reshape(1, hpad))[:, :, :nheads]
            cum_t = cum.transpose(0, 2, 1)
            o = _fox_attention(qkv.reshape(nb, seq, 3 * d), cum_t[:, :, :, None], cum_t[:, :, None, :], nheads, dh)
            x = _fox_out_layer(o.reshape(nb * seq, d), w_o16[jl], x2, mod, ln_tok_g[i], ln_tok_b[i], seq, alpha)
            x = x.reshape(nb, seq, d)
        x2 = x.reshape(nb * seq, d)
        hbf, n1, e1, r2, e2 = _peer_router(x2, mod, w_q16[i], keys16[i], seq)
        x = _peer_experts(hbf, u16[i], v16[i], n1.transpose(1, 0, 2), e1.transpose(1, 0, 2), r2, e2,
                          x2, mod, ln_ch_g[i], ln_ch_b[i], seq, alpha)
        x = x.reshape(nb, seq, d)
    return x
```

```python
import functools

import jax
import jax.numpy as jnp
from jax import lax
from jax.experimental import pallas as pl
from jax.experimental.pallas import tpu as pltpu

F32 = jnp.float32
BF16 = jnp.bfloat16

LN_EPS = 1e-5
N_MOD = 6
POOL_WINDOWS = (2, 4, 8, 16)
POOL_HALO = 16
PEER_TOPK = 16
SQRT_HALF = 0.7071067811865476
LANES = 128
SUBLANES = 8
VMEM_LIMIT_BYTES = 56 * 1024 * 1024


ONE_BUFFER = dict(pipeline_mode=pl.Buffered(1))


def _params(*semantics):
    return pltpu.CompilerParams(dimension_semantics=semantics, vmem_limit_bytes=VMEM_LIMIT_BYTES)


def _tile(dim, target, align):
    if dim <= target:
        return dim
    t = (target // align) * align
    while t > align and dim % t:
        t -= align
    assert dim % t == 0, (dim, target, align)
    return t


def _layer_norm(r, g, b):
    mu = jnp.mean(r, axis=-1, keepdims=True)
    xc = r - mu
    var = jnp.mean(xc * xc, axis=-1, keepdims=True)
    return xc * lax.rsqrt(var + LN_EPS) * g + b


def _adaln_kernel(ct_ref, w_ref, b_ref, ada_ref, o_ref, acc_ref, *, nb, nk):
    k = pl.program_id(1)

    @pl.when(k == 0)
    def _():
        acc_ref[...] = jnp.zeros_like(acc_ref)

    s = jax.nn.silu(ct_ref[...])
    w = w_ref[...]
    tk, tn = w.shape
    for b in range(nb):
        prod = w * s[:, b:b + 1]
        acc_ref[b] += jnp.sum(prod.reshape(tk // 8, 8, tn), axis=0)

    @pl.when(k == nk - 1)
    def _():
        for b in range(nb):
            t0 = jnp.sum(acc_ref[b], axis=0, keepdims=True) + b_ref[...]
            o_ref[b] = t0 + ada_ref[...]


def _adaln(c, w_c, b_c, ada_table):
    nb, d = c.shape
    n = w_c.shape[1]
    depth = ada_table.shape[0]
    tk = _tile(d, 512, 8)
    tn = _tile(n, 2048, 128)
    nk = d // tk
    return pl.pallas_call(
        functools.partial(_adaln_kernel, nb=nb, nk=nk),
        grid=(n // tn, nk),
        in_specs=[
            pl.BlockSpec((tk, nb), lambda j, k: (k, 0)),
            pl.BlockSpec((tk, tn), lambda j, k: (k, j)),
            pl.BlockSpec((1, tn), lambda j, k: (0, j)),
            pl.BlockSpec((depth, tn), lambda j, k: (0, j)),
        ],
        out_specs=pl.BlockSpec((nb, depth, tn), lambda j, k: (0, 0, j)),
        out_shape=jax.ShapeDtypeStruct((nb, depth, n), F32),
        scratch_shapes=[pltpu.VMEM((nb, 8, tn), F32)],
        compiler_params=_params("parallel", "arbitrary"),
        name="adaln",
    )(c.T, w_c, b_c.reshape(1, n), ada_table)


def _pool_kernel(x_ref, xp_ref, mod_ref, w_ref, scale_ref, g_ref, b_ref, o_ref, hh_ref, *, alpha, ts, dg):
    s_idx = pl.program_id(1)
    m = mod_ref[0]
    sh, sc, gate = m[0:1], m[1:2], m[2:3]
    x = x_ref[0]
    h = x * (1.0 + sc) + sh
    hp = xp_ref[0] * (1.0 + sc) + sh
    hh_ref[0:POOL_HALO] = jnp.where(s_idx > 0, hp, 0.0)
    hh_ref[POOL_HALO:] = h
    pos = lax.broadcasted_iota(jnp.int32, (ts, 1), 0) + s_idx * ts
    ys = []
    for gi, win in enumerate(POOL_WINDOWS):
        cols = slice(gi * dg, (gi + 1) * dg)
        acc = hh_ref[pl.ds(POOL_HALO, ts), cols]
        for k in range(1, win):
            acc = acc + hh_ref[pl.ds(POOL_HALO - k, ts), cols]
        cnt = jnp.minimum(pos + 1, win).astype(F32)
        z = acc / cnt - h[:, cols]
        ys.append(jnp.dot(z.astype(BF16), w_ref[gi], preferred_element_type=F32))
    y = jnp.concatenate(ys, axis=1) * scale_ref[...]
    r = alpha * x + (1.0 + gate) * y
    o_ref[0] = _layer_norm(r, g_ref[...], b_ref[...])


def _pool_layer(x, mod, w, layer, scale, ln_g, ln_b, alpha):
    nb, s, d = x.shape
    _, g, dg, _ = w.shape
    ts = _tile(s, 256, POOL_HALO)
    hb = ts // POOL_HALO
    return pl.pallas_call(
        functools.partial(_pool_kernel, alpha=alpha, ts=ts, dg=dg),
        grid=(nb, s // ts),
        in_specs=[
            pl.BlockSpec((1, ts, d), lambda b, i: (b, i, 0)),
            pl.BlockSpec((1, POOL_HALO, d), lambda b, i: (b, jnp.maximum(i * hb - 1, 0), 0)),
            pl.BlockSpec((1, N_MOD, d), lambda b, i: (b, 0, 0)),
            pl.BlockSpec((None, g, dg, dg), lambda b, i: (layer, 0, 0, 0)),
            pl.BlockSpec((1, d), lambda b, i: (0, 0)),
            pl.BlockSpec((1, d), lambda b, i: (0, 0)),
            pl.BlockSpec((1, d), lambda b, i: (0, 0)),
        ],
        out_specs=pl.BlockSpec((1, ts, d), lambda b, i: (b, i, 0)),
        out_shape=jax.ShapeDtypeStruct((nb, s, d), F32),
        scratch_shapes=[pltpu.VMEM((ts + POOL_HALO, d), F32)],
        compiler_params=_params("parallel", "parallel"),
        name="pool_layer",
    )(x, x, mod, w, scale.reshape(1, d), ln_g.reshape(1, d), ln_b.reshape(1, d))


def _cast_kernel(x_ref, o_ref):
    o_ref[...] = x_ref[...].astype(o_ref.dtype)


def _cast_columns(w, ncols):
    nl, rows, _ = w.shape
    tk = _tile(rows, 512, 16)
    tn = _tile(ncols, 2048, LANES)
    return pl.pallas_call(
        _cast_kernel,
        grid=(nl, rows // tk, ncols // tn),
        in_specs=[pl.BlockSpec((None, tk, tn), lambda l, i, j: (l, i, j))],
        out_specs=pl.BlockSpec((None, tk, tn), lambda l, i, j: (l, i, j)),
        out_shape=jax.ShapeDtypeStruct((nl, rows, ncols), BF16),
        compiler_params=_params("parallel", "parallel", "parallel"),
        name="cast_columns",
    )(w)


def _fox_proj_kernel(x_ref, mod_ref, w_ref, wf_ref, qkv_ref, f_ref, h_ref, *, nq_tiles, sm_scale, nheads):
    j = pl.program_id(1)

    @pl.when(j == 0)
    def _():
        m = mod_ref[0]
        h = x_ref[...] * (1.0 + m[1:2]) + m[0:1]
        h_hi = h.astype(BF16)
        h_ref[...] = h_hi
        lane = lax.broadcasted_iota(jnp.int32, wf_ref.shape, 1)
        wf = jnp.where(lane < nheads, wf_ref[...], 0.0)
        wf_hi = wf.astype(BF16)
        wf_lo = (wf - wf_hi.astype(F32)).astype(BF16)
        h_lo = (h - h_hi.astype(F32)).astype(BF16)
        f_ref[...] = (jnp.dot(h_hi, wf_hi, preferred_element_type=F32)
                      + jnp.dot(h_lo, wf_hi, preferred_element_type=F32)
                      + jnp.dot(h_hi, wf_lo, preferred_element_type=F32))

    y = jnp.dot(h_ref[...], w_ref[...], preferred_element_type=F32)
    y = y * jnp.where(j < nq_tiles, sm_scale, 1.0)
    qkv_ref[...] = y.astype(BF16)


def _fox_proj(x2, mod, w_qkv, w_in, layer, nheads, seq, sm_scale):
    t, d = x2.shape
    n = w_qkv.shape[2]
    assert n % LANES == 0 and nheads <= LANES, (n, nheads)
    hp = LANES
    tm = _tile(seq, 512, 16)
    tn = _tile(d, 1024, 128)
    bps = seq // tm
    return pl.pallas_call(
        functools.partial(_fox_proj_kernel, nq_tiles=d // tn, sm_scale=sm_scale, nheads=nheads),
        grid=(t // tm, n // tn),
        in_specs=[
            pl.BlockSpec((tm, d), lambda i, j: (i, 0)),
            pl.BlockSpec((1, N_MOD, d), lambda i, j: (i // bps, 0, 0)),
            pl.BlockSpec((None, d, tn), lambda i, j: (layer, 0, j)),
            pl.BlockSpec((None, d, hp), lambda i, j: (layer, 0, n // hp)),
        ],
        out_specs=[
            pl.BlockSpec((tm, tn), lambda i, j: (i, j)),
            pl.BlockSpec((tm, hp), lambda i, j: (i, 0)),
        ],
        out_shape=[jax.ShapeDtypeStruct((t, n), BF16), jax.ShapeDtypeStruct((t, hp), F32)],
        scratch_shapes=[pltpu.VMEM((tm, d), BF16)],
        compiler_params=_params("parallel", "arbitrary"),
        name="fox_proj",
    )(x2, mod, w_qkv, w_in)


def _split3(v):
    p1 = v.astype(BF16)
    r1 = v - p1.astype(F32)
    p2 = r1.astype(BF16)
    p3 = (r1 - p2.astype(F32)).astype(BF16)
    return p1, p2, p3


def _decay_kernel(f_ref, bf_ref, o_ref, *, chunk):
    s = f_ref.shape[1]
    row = lax.broadcasted_iota(jnp.int32, (chunk, chunk), 0)
    col = lax.broadcasted_iota(jnp.int32, (chunk, chunk), 1)
    tri = (col <= row).astype(BF16)
    carry = jnp.zeros((1, f_ref.shape[2]), F32)
    for c in range(s // chunk):
        rows = slice(c * chunk, (c + 1) * chunk)
        lf = jax.nn.log_sigmoid(f_ref[0, rows, :] + bf_ref[...])
        p1, p2, p3 = _split3(lf)
        cs = (jnp.dot(tri, p1, preferred_element_type=F32)
              + jnp.dot(tri, p2, preferred_element_type=F32)
              + jnp.dot(tri, p3, preferred_element_type=F32)) + carry
        o_ref[0, rows, :] = cs
        carry = cs[chunk - 1:chunk, :]


def _decay_cumsum(f, b_f):
    nb, s, hp = f.shape
    chunk = _tile(s, 256, 16)
    return pl.pallas_call(
        functools.partial(_decay_kernel, chunk=chunk),
        grid=(nb,),
        in_specs=[pl.BlockSpec((1, s, hp), lambda b: (b, 0, 0)), pl.BlockSpec((1, hp), lambda b: (0, 0))],
        out_specs=pl.BlockSpec((1, s, hp), lambda b: (b, 0, 0)),
        out_shape=jax.ShapeDtypeStruct((nb, s, hp), F32),
        compiler_params=_params("parallel"),
        name="fox_decay",
    )(f, b_f)


def _attn_kernel(q_ref, k_ref, v_ref, cq_ref, ck_ref, o_ref, m_ref, l_ref, acc_ref, cqb_ref, *, tq, rs):
    qi = pl.program_id(2)
    nch = tq // LANES
    m_ref[...] = jnp.full_like(m_ref, -jnp.inf)
    l_ref[...] = jnp.zeros_like(l_ref)
    acc_ref[...] = jnp.zeros_like(acc_ref)
    cqb_ref[...] = jnp.broadcast_to(cq_ref[0, 0], cqb_ref.shape)

    def step(off, masked):
        kb = k_ref[0, pl.ds(off, tq), :]
        vb = v_ref[0, pl.ds(off, tq), :]
        ck = ck_ref[0, 0, :, pl.ds(off, tq)]
        for sb in range(tq // rs):
            rows = slice(sb * rs, (sb + 1) * rs)
            s = lax.dot_general(q_ref[0, rows, :], kb, (((1,), (1,)), ((), ())), preferred_element_type=F32)
            u = []
            for c in range(nch):
                uc = s[:, c * LANES:(c + 1) * LANES] - ck[:, c * LANES:(c + 1) * LANES]
                if masked:
                    row = lax.broadcasted_iota(jnp.int32, (rs, LANES), 0) + sb * rs
                    col = lax.broadcasted_iota(jnp.int32, (rs, LANES), 1) + c * LANES
                    uc = jnp.where(col <= row, uc, -jnp.inf)
                u.append(uc)
            bm = functools.reduce(jnp.maximum, u)
            cqb = cqb_ref[rows, :]
            m_old = m_ref[rows, :]
            m_new = jnp.maximum(m_old, jnp.max(bm, axis=1, keepdims=True) + cqb)
            a = jnp.exp(m_old - m_new)
            r = cqb - m_new
            p = [jnp.exp(uc + r) for uc in u]
            l_ref[rows, :] = a * l_ref[rows, :] + functools.reduce(jnp.add, p)
            pb = jnp.concatenate([pc.astype(BF16) for pc in p], axis=1)
            acc_ref[rows, :] = a * acc_ref[rows, :] + jnp.dot(pb, vb, preferred_element_type=F32)
            m_ref[rows, :] = m_new

    def body(ki, carry):
        step(pl.multiple_of(ki * tq, tq), False)
        return carry

    lax.fori_loop(0, qi, body, 0)
    step(pl.multiple_of(qi * tq, tq), True)
    l = jnp.sum(l_ref[...], axis=1, keepdims=True)
    o_ref[0] = (acc_ref[...] / l).astype(o_ref.dtype)


def _fox_attention(qkv, cq, ck, nheads, dh):
    nb, s, _ = qkv.shape
    d = nheads * dh
    assert dh == LANES, dh
    tq = _tile(s, 1024, LANES)
    rs = _tile(tq, 512, LANES)
    stat = pltpu.VMEM((tq, LANES), F32)
    return pl.pallas_call(
        functools.partial(_attn_kernel, tq=tq, rs=rs),
        grid=(nb, nheads, s // tq),
        in_specs=[
            pl.BlockSpec((1, tq, dh), lambda b, h, i: (b, i, h)),
            pl.BlockSpec((1, s, dh), lambda b, h, i: (b, 0, nheads + h)),
            pl.BlockSpec((1, s, dh), lambda b, h, i: (b, 0, 2 * nheads + h)),
            pl.BlockSpec((1, 1, tq, 1), lambda b, h, i: (b, h, i, 0)),
            pl.BlockSpec((1, 1, 1, s), lambda b, h, i: (b, h, 0, 0)),
        ],
        out_specs=pl.BlockSpec((1, tq, dh), lambda b, h, i: (b, i, h)),
        out_shape=jax.ShapeDtypeStruct((nb, s, d), BF16),
        scratch_shapes=[stat, stat, pltpu.VMEM((tq, dh), F32), stat],
        compiler_params=_params("parallel", "parallel", "parallel"),
        name="fox_attention",
    )(qkv, qkv, qkv, cq, ck)


def _out_ln_kernel(o_ref, w_ref, x_ref, mod_ref, g_ref, b_ref, out_ref, *, alpha, tn, nj, gate_row):
    j = pl.program_id(1)
    y = jnp.dot(o_ref[...], w_ref[...], preferred_element_type=F32)
    out_ref[:, pl.ds(pl.multiple_of(j * tn, tn), tn)] = y

    @pl.when(j == nj - 1)
    def _():
        gate = mod_ref[0][gate_row:gate_row + 1]
        r = alpha * x_ref[...] + (1.0 + gate) * out_ref[...]
        out_ref[...] = _layer_norm(r, g_ref[...], b_ref[...])


def _fox_out_layer(o2, w_o, layer, x2, mod, ln_g, ln_b, seq, alpha):
    t, d = x2.shape
    tm = _tile(seq, 512, 16)
    tn = _tile(d, 512, 128)
    bps = seq // tm
    nj = d // tn
    return pl.pallas_call(
        functools.partial(_out_ln_kernel, alpha=alpha, tn=tn, nj=nj, gate_row=2),
        grid=(t // tm, nj),
        in_specs=[
            pl.BlockSpec((tm, d), lambda i, j: (i, 0)),
            pl.BlockSpec((None, d, tn), lambda i, j: (layer, 0, j)),
            pl.BlockSpec((tm, d), lambda i, j: (i, 0), **ONE_BUFFER),
            pl.BlockSpec((1, N_MOD, d), lambda i, j: (i // bps, 0, 0)),
            pl.BlockSpec((1, d), lambda i, j: (0, 0)),
            pl.BlockSpec((1, d), lambda i, j: (0, 0)),
        ],
        out_specs=pl.BlockSpec((tm, d), lambda i, j: (i, 0), **ONE_BUFFER),
        out_shape=jax.ShapeDtypeStruct((t, d), F32),
        compiler_params=_params("parallel", "arbitrary"),
        name="fox_out_layer",
    )(o2, w_o, x2, mod, ln_g.reshape(1, d), ln_b.reshape(1, d))


def _top_ranked(s, k, exact):
    rows = s.shape[0]
    row = lax.broadcasted_iota(jnp.int32, s.shape, 0)
    rank = jnp.full(s.shape, float(k), F32)
    vals = []
    for r in range(k):
        m = jnp.max(s, axis=0, keepdims=True)
        sel = s == m
        if exact:
            first = jnp.min(jnp.where(sel, row, rows), axis=0, keepdims=True)
            sel = row == first
        rank = jnp.where(sel, float(r), rank)
        vals.append(m)
        s = jnp.where(sel, -jnp.inf, s)
    return jnp.concatenate(vals, axis=0), rank


def _route_head(q, keys_ref, n1_ref, e1_ref, r2_ref, e2_ref, hd, topk, exact):
    dhalf = q.shape[1] // 2
    fk = float(topk)
    scores, tops, ranks = [], [], []
    for p in range(2):
        qp = q[:, p * dhalf:(p + 1) * dhalf]
        s = lax.dot_general(keys_ref[p], qp, (((1,), (1,)), ((), ())), preferred_element_type=F32)
        tv, rk = _top_ranked(s, topk, exact)
        scores.append(s)
        tops.append(tv)
        ranks.append(rk)
    tv1, tv2 = tops

    half = topk // 2
    sub = lax.broadcasted_iota(jnp.int32, (8, 1), 0)
    pieces = [tv1[0:1] + tv2]
    for a in range(1, half):
        pieces.append(jnp.where(sub < topk // (a + 1), tv1[a:a + 1] + tv2[0:8], -jnp.inf))
    pieces.append(tv1[half:topk] + tv2[0:1])
    cand = jnp.concatenate(pieces, axis=0)
    cvals, crank = _top_ranked(cand, topk, exact)
    taken = crank < fk
    z = jnp.sum(jnp.where(taken, jnp.exp(cand - cvals[0:1]), 0.0), axis=0, keepdims=True)
    takenf = taken.astype(F32)

    rank1, rank2 = ranks
    n1 = jnp.zeros_like(rank1)
    start = 0
    for a in range(half):
        size = topk if a == 0 else 8
        n_a = jnp.sum(takenf[start:start + size], axis=0, keepdims=True)
        n1 = jnp.where(rank1 == float(a), n_a, n1)
        start += size
    for a in range(half, topk):
        n1 = jnp.where(rank1 == float(a), takenf[start + a - half:start + a - half + 1], n1)

    n1_ref[:, pl.ds(hd, 1), :] = n1[:, None, :]
    e1_ref[:, pl.ds(hd, 1), :] = (SQRT_HALF * jnp.exp(scores[0] - tv1[0:1]) / z)[:, None, :]
    r2_ref[0] = rank2
    e2_ref[0] = jnp.exp(scores[1] - tv2[0:1])

    if exact:
        return None
    counts = [jnp.sum((rk < fk).astype(F32), axis=0, keepdims=True) for rk in (rank1, rank2)] + [
        jnp.sum(takenf, axis=0, keepdims=True)]
    bad = functools.reduce(jnp.maximum, [jnp.abs(c - fk) for c in counts])
    return jnp.max(bad)


def _router_kernel(x_ref, mod_ref, wq_ref, keys_ref, hbf_ref, n1_ref, e1_ref, r2_ref, e2_ref, *, topk):
    hd = pl.program_id(1)

    @pl.when(hd == 0)
    def _():
        m = mod_ref[0]
        hbf_ref[...] = (x_ref[...] * (1.0 + m[4:5]) + m[3:4]).astype(BF16)

    kh = hbf_ref.shape[1] // 2
    q = (jnp.dot(hbf_ref[:, :kh], wq_ref[:kh, :], preferred_element_type=F32)
         + jnp.dot(hbf_ref[:, kh:], wq_ref[kh:, :], preferred_element_type=F32)).astype(BF16)

    tie = _route_head(q, keys_ref, n1_ref, e1_ref, r2_ref, e2_ref, hd, topk, exact=False)

    @pl.when(tie > 0.5)
    def _():
        _route_head(q, keys_ref, n1_ref, e1_ref, r2_ref, e2_ref, hd, topk, exact=True)


def _peer_router(x2, mod, w_q, keys, layer, seq):
    t, d = x2.shape
    _, _, nkeys, dhalf = keys.shape
    ph = w_q.shape[2] // (2 * dhalf)
    tr = _tile(seq, 256, 128)
    bps = seq // tr
    route = jax.ShapeDtypeStruct((ph, nkeys, t), F32)
    route_spec = pl.BlockSpec((1, nkeys, tr), lambda i, h: (h, 0, i))
    key_major = jax.ShapeDtypeStruct((nkeys, ph, t), F32)
    key_spec = pl.BlockSpec((nkeys, ph, tr), lambda i, h: (0, 0, i))
    return pl.pallas_call(
        functools.partial(_router_kernel, topk=PEER_TOPK),
        grid=(t // tr, ph),
        in_specs=[
            pl.BlockSpec((tr, d), lambda i, h: (i, 0)),
            pl.BlockSpec((1, N_MOD, d), lambda i, h: (i // bps, 0, 0)),
            pl.BlockSpec((None, d, 2 * dhalf), lambda i, h: (layer, 0, h)),
            pl.BlockSpec((None, 2, nkeys, dhalf), lambda i, h: (layer, 0, 0, 0)),
        ],
        out_specs=[pl.BlockSpec((tr, d), lambda i, h: (i, 0)), key_spec, key_spec, route_spec, route_spec],
        out_shape=[jax.ShapeDtypeStruct((t, d), BF16), key_major, key_major, route, route],
        compiler_params=_params("parallel", "arbitrary"),
        name="peer_router",
    )(x2, mod, w_q, keys)


def _gate_block(a_scr, wt_scr, n1_ref, e1_ref, r2_ref, e2_ref, key0, nkeys, ph):
    ts, tm = a_scr.shape
    for il in range(ts // nkeys):
        i_key = key0 + il
        for lc in range(tm // LANES):
            cols = slice(lc * LANES, (lc + 1) * LANES)
            n1t = n1_ref[i_key, :, cols]
            e1t = e1_ref[i_key, :, cols]
            n1b = [n1t[hd:hd + 1, :] for hd in range(ph)]
            e1b = [e1t[hd:hd + 1, :] for hd in range(ph)]
            tiles = []
            for rc in range(nkeys // SUBLANES):
                rows = slice(rc * SUBLANES, (rc + 1) * SUBLANES)
                w = None
                for hd in range(ph):
                    t = jnp.where(r2_ref[hd, rows, cols] < n1b[hd], e2_ref[hd, rows, cols], 0.0) * e1b[hd]
                    w = t if w is None else w + t
                b = a_scr[il * nkeys + rc * SUBLANES:il * nkeys + (rc + 1) * SUBLANES, cols]
                tiles.append(w * (b * (1.0 + lax.erf(b))))
            blk = jnp.concatenate(tiles, axis=0).astype(BF16)
            wt_scr[cols, il * nkeys:(il + 1) * nkeys] = blk.T


def _expert_kernel(h_ref, u_ref, v_ref, n1_ref, e1_ref, r2_ref, e2_ref, x_ref, mod_ref, g_ref, b_ref, o_ref,
                   a_e, a_o, wt_e, wt_o, *, alpha, nkeys, ph, nj, ts):
    j = pl.program_id(1)
    tm, d = o_ref.shape
    gps = ts // nkeys

    @pl.when(j == 0)
    def _():
        o_ref[...] = jnp.zeros_like(o_ref)
        wt_e[...] = jnp.zeros_like(wt_e)
        a_o[...] = jnp.zeros_like(a_o)

    def mm1(rows, a_scr):
        half = tm // 2
        for s in range(2):
            a_scr[:, s * half:(s + 1) * half] = lax.dot_general(
                u_ref[rows, :], h_ref[s * half:(s + 1) * half, :], (((1,), (1,)), ((), ())),
                preferred_element_type=F32)

    def mm2(wt_scr, rows):
        half = d // 2
        for s in range(2):
            o_ref[:, s * half:(s + 1) * half] += jnp.dot(
                wt_scr[...], v_ref[rows, s * half:(s + 1) * half], preferred_element_type=F32)

    lo, hi = slice(0, ts), slice(ts, 2 * ts)
    last = 2 * nj - 1
    mm2(wt_e, lo)
    _gate_block(a_o, wt_o, n1_ref, e1_ref, r2_ref, e2_ref, jnp.maximum(2 * j - 1, 0) * gps, nkeys, ph)
    mm1(lo, a_e)
    mm2(wt_o, hi)
    _gate_block(a_e, wt_e, n1_ref, e1_ref, r2_ref, e2_ref, jnp.minimum(2 * j, last) * gps, nkeys, ph)
    mm1(hi, a_o)

    @pl.when(j == nj)
    def _():
        g_c = mod_ref[0][5:6]
        r = alpha * x_ref[...] + (1.0 + g_c) * o_ref[...]
        o_ref[...] = _layer_norm(r, g_ref[...], b_ref[...])


def _peer_experts(hbf, u, v, layer, n1t, e1t, r2, e2, x2, mod, ln_g, ln_b, seq, alpha):
    t, d = x2.shape
    ne = u.shape[1]
    ph, nkeys, _ = r2.shape
    tm = _tile(seq, 512, 2 * LANES)
    ts = _tile(ne // 2, 256, nkeys)
    bps = seq // tm
    nj = ne // (2 * ts)
    route_spec = pl.BlockSpec((ph, nkeys, tm), lambda i, j: (0, 0, i), **ONE_BUFFER)
    key_spec = pl.BlockSpec((nkeys, ph, tm), lambda i, j: (0, 0, i), **ONE_BUFFER)
    return pl.pallas_call(
        functools.partial(_expert_kernel, alpha=alpha, nkeys=nkeys, ph=ph, nj=nj, ts=ts),
        grid=(t // tm, nj + 1),
        in_specs=[
            pl.BlockSpec((tm, d), lambda i, j: (i, 0), **ONE_BUFFER),
            pl.BlockSpec((None, 2 * ts, d), lambda i, j: (layer, jnp.minimum(j, nj - 1), 0)),
            pl.BlockSpec((None, 2 * ts, d), lambda i, j: (layer, jnp.maximum(j - 1, 0), 0)),
            key_spec, key_spec, route_spec, route_spec,
            pl.BlockSpec((tm, d), lambda i, j: (i, 0), **ONE_BUFFER),
            pl.BlockSpec((1, N_MOD, d), lambda i, j: (i // bps, 0, 0)),
            pl.BlockSpec((1, d), lambda i, j: (0, 0)),
            pl.BlockSpec((1, d), lambda i, j: (0, 0)),
        ],
        out_specs=pl.BlockSpec((tm, d), lambda i, j: (i, 0), **ONE_BUFFER),
        out_shape=jax.ShapeDtypeStruct((t, d), F32),
        scratch_shapes=[pltpu.VMEM((ts, tm), F32), pltpu.VMEM((ts, tm), F32),
                        pltpu.VMEM((tm, ts), BF16), pltpu.VMEM((tm, ts), BF16)],
        compiler_params=_params("parallel", "arbitrary"),
        name="peer_experts",
    )(hbf, u, v, n1t, e1t, r2, e2, x2, mod, ln_g.reshape(1, d), ln_b.reshape(1, d))


def kernel(x, c, w_c, b_c, ada_table, ln_tok_g, ln_tok_b, ln_ch_g, ln_ch_b, pool_w, pool_scale,
           fox_w_in, fox_b_f, fox_w_o, peer_w_q, peer_keys, peer_u, peer_v):
    nb, seq, d = x.shape
    depth = ada_table.shape[0]
    alpha = float((2 * depth) ** 0.25)
    nheads = fox_b_f.shape[1]
    dh = d // nheads
    hpad = -(-nheads // 128) * 128

    mod_all = _adaln(c, w_c, b_c, ada_table)

    pool_w16 = pool_w.astype(BF16)
    w_qkv16 = _cast_columns(fox_w_in, 3 * d)
    b_f = jnp.pad(fox_b_f, ((0, 0), (0, hpad - nheads)))
    w_o16 = fox_w_o.astype(BF16)
    w_q16 = peer_w_q.astype(BF16)
    keys16 = peer_keys.astype(BF16)
    u16 = (peer_u * SQRT_HALF).astype(BF16)
    v16 = peer_v.astype(BF16)

    n_mixers = 2
    for i in range(depth):
        mod = mod_all[:, i].reshape(nb, N_MOD, d)
        jl = i // n_mixers
        if i % n_mixers == 0:
            x = _pool_layer(x, mod, pool_w16, jl, pool_scale[jl], ln_tok_g[i], ln_tok_b[i], alpha)
        else:
            x2 = x.reshape(nb * seq, d)
            qkv, f = _fox_proj(x2, mod, w_qkv16, fox_w_in, jl, nheads, seq, float(dh) ** -0.5)
            cum = _decay_cumsum(f.reshape(nb, seq, hpad), b_f[jl].reshape(1, hpad))[:, :, :nheads]
            cum_t = cum.transpose(0, 2, 1)
            o = _fox_attention(qkv.reshape(nb, seq, 3 * d), cum_t[:, :, :, None], cum_t[:, :, None, :], nheads, dh)
            x = _fox_out_layer(o.reshape(nb * seq, d), w_o16, jl, x2, mod, ln_tok_g[i], ln_tok_b[i], seq, alpha)
            x = x.reshape(nb, seq, d)
        x2 = x.reshape(nb * seq, d)
        hbf, n1t, e1t, r2, e2 = _peer_router(x2, mod, w_q16, keys16, i, seq)
        x = _peer_experts(hbf, u16, v16, i, n1t, e1t, r2, e2, x2, mod, ln_ch_g[i], ln_ch_b[i], seq, alpha)
        x = x.reshape(nb, seq, d)
    return x
```

```python
import functools

import jax
import jax.numpy as jnp
from jax import lax
from jax.experimental import pallas as pl
from jax.experimental.pallas import tpu as pltpu

F32 = jnp.float32
BF16 = jnp.bfloat16

LN_EPS = 1e-5
N_MOD = 6
POOL_WINDOWS = (2, 4, 8, 16)
POOL_HALO = 16
PEER_TOPK = 16
SQRT_HALF = 0.7071067811865476
LANES = 128
SUBLANES = 8
VMEM_LIMIT_BYTES = 56 * 1024 * 1024


NT_DIMS = (((1,), (1,)), ((), ()))
ONE_BUFFER = dict(pipeline_mode=pl.Buffered(1))


def _params(*semantics):
    return pltpu.CompilerParams(dimension_semantics=semantics, vmem_limit_bytes=VMEM_LIMIT_BYTES)


def _tile(dim, target, align):
    if dim <= target:
        return dim
    t = (target // align) * align
    while t > align and dim % t:
        t -= align
    assert dim % t == 0, (dim, target, align)
    return t


def _layer_norm(r, g, b):
    mu = jnp.mean(r, axis=-1, keepdims=True)
    xc = r - mu
    var = jnp.mean(xc * xc, axis=-1, keepdims=True)
    return xc * lax.rsqrt(var + LN_EPS) * g + b


def _adaln_kernel(ct_ref, w_ref, b_ref, ada_ref, o_ref, acc_ref, *, nb, nk):
    k = pl.program_id(1)

    @pl.when(k == 0)
    def _():
        acc_ref[...] = jnp.zeros_like(acc_ref)

    s = jax.nn.silu(ct_ref[...])
    w = w_ref[...]
    tk, tn = w.shape
    for b in range(nb):
        prod = w * s[:, b:b + 1]
        acc_ref[b] += jnp.sum(prod.reshape(tk // 8, 8, tn), axis=0)

    @pl.when(k == nk - 1)
    def _():
        for b in range(nb):
            t0 = jnp.sum(acc_ref[b], axis=0, keepdims=True) + b_ref[...]
            o_ref[b] = t0 + ada_ref[...]


def _adaln(c, w_c, b_c, ada_table):
    nb, d = c.shape
    n = w_c.shape[1]
    depth = ada_table.shape[0]
    tk = _tile(d, 512, 8)
    tn = _tile(n, 2048, 128)
    nk = d // tk
    return pl.pallas_call(
        functools.partial(_adaln_kernel, nb=nb, nk=nk),
        grid=(n // tn, nk),
        in_specs=[
            pl.BlockSpec((tk, nb), lambda j, k: (k, 0)),
            pl.BlockSpec((tk, tn), lambda j, k: (k, j)),
            pl.BlockSpec((1, tn), lambda j, k: (0, j)),
            pl.BlockSpec((depth, tn), lambda j, k: (0, j)),
        ],
        out_specs=pl.BlockSpec((nb, depth, tn), lambda j, k: (0, 0, j)),
        out_shape=jax.ShapeDtypeStruct((nb, depth, n), F32),
        scratch_shapes=[pltpu.VMEM((nb, 8, tn), F32)],
        compiler_params=_params("parallel", "arbitrary"),
        name="adaln",
    )(c.T, w_c, b_c.reshape(1, n), ada_table)


def _pool_kernel(x_ref, xp_ref, mod_ref, w_ref, scale_ref, g_ref, b_ref, o_ref, hh_ref, *, alpha, ts, dg):
    s_idx = pl.program_id(1)
    m = mod_ref[0]
    sh, sc, gate = m[0:1], m[1:2], m[2:3]
    x = x_ref[0]
    h = x * (1.0 + sc) + sh
    hp = xp_ref[0] * (1.0 + sc) + sh
    hh_ref[0:POOL_HALO] = jnp.where(s_idx > 0, hp, 0.0)
    hh_ref[POOL_HALO:] = h
    pos = lax.broadcasted_iota(jnp.int32, (ts, 1), 0) + s_idx * ts
    ys = []
    for gi, win in enumerate(POOL_WINDOWS):
        cols = slice(gi * dg, (gi + 1) * dg)
        acc = hh_ref[pl.ds(POOL_HALO, ts), cols]
        for k in range(1, win):
            acc = acc + hh_ref[pl.ds(POOL_HALO - k, ts), cols]
        cnt = jnp.minimum(pos + 1, win).astype(F32)
        z = acc / cnt - h[:, cols]
        ys.append(jnp.dot(z.astype(BF16), w_ref[gi], preferred_element_type=F32))
    y = jnp.concatenate(ys, axis=1) * scale_ref[...]
    r = alpha * x + (1.0 + gate) * y
    o_ref[0] = _layer_norm(r, g_ref[...], b_ref[...])


def _pool_layer(x, mod, w, layer, scale, ln_g, ln_b, alpha):
    nb, s, d = x.shape
    _, g, dg, _ = w.shape
    ts = _tile(s, 256, POOL_HALO)
    hb = ts // POOL_HALO
    return pl.pallas_call(
        functools.partial(_pool_kernel, alpha=alpha, ts=ts, dg=dg),
        grid=(nb, s // ts),
        in_specs=[
            pl.BlockSpec((1, ts, d), lambda b, i: (b, i, 0)),
            pl.BlockSpec((1, POOL_HALO, d), lambda b, i: (b, jnp.maximum(i * hb - 1, 0), 0)),
            pl.BlockSpec((1, N_MOD, d), lambda b, i: (b, 0, 0)),
            pl.BlockSpec((None, g, dg, dg), lambda b, i: (layer, 0, 0, 0)),
            pl.BlockSpec((1, d), lambda b, i: (0, 0)),
            pl.BlockSpec((1, d), lambda b, i: (0, 0)),
            pl.BlockSpec((1, d), lambda b, i: (0, 0)),
        ],
        out_specs=pl.BlockSpec((1, ts, d), lambda b, i: (b, i, 0)),
        out_shape=jax.ShapeDtypeStruct((nb, s, d), F32),
        scratch_shapes=[pltpu.VMEM((ts + POOL_HALO, d), F32)],
        compiler_params=_params("parallel", "parallel"),
        name="pool_layer",
    )(x, x, mod, w, scale.reshape(1, d), ln_g.reshape(1, d), ln_b.reshape(1, d))


def _cast_kernel(x_ref, o_ref):
    o_ref[...] = x_ref[...].astype(o_ref.dtype)


def _cast_rows(w, nrows):
    nl, _, cols = w.shape
    tk = _tile(nrows, 512, 16)
    tn = _tile(cols, 2048, LANES)
    return pl.pallas_call(
        _cast_kernel,
        grid=(nl, nrows // tk, cols // tn),
        in_specs=[pl.BlockSpec((None, tk, tn), lambda l, i, j: (l, i, j))],
        out_specs=pl.BlockSpec((None, tk, tn), lambda l, i, j: (l, i, j)),
        out_shape=jax.ShapeDtypeStruct((nl, nrows, cols), BF16),
        compiler_params=_params("parallel", "parallel", "parallel"),
        name="cast_rows",
    )(w)


def _fox_proj_kernel(x_ref, mod_ref, w_ref, wf_ref, qkv_ref, f_ref, h_ref, *, nq_tiles, sm_scale, nheads):
    j = pl.program_id(1)

    @pl.when(j == 0)
    def _():
        m = mod_ref[0]
        h = x_ref[...] * (1.0 + m[1:2]) + m[0:1]
        h_hi = h.astype(BF16)
        h_ref[...] = h_hi
        row = lax.broadcasted_iota(jnp.int32, wf_ref.shape, 0)
        wf = jnp.where(row < nheads, wf_ref[...], 0.0)
        wf_hi = wf.astype(BF16)
        wf_lo = (wf - wf_hi.astype(F32)).astype(BF16)
        h_lo = (h - h_hi.astype(F32)).astype(BF16)
        f_ref[...] = (lax.dot_general(h_hi, wf_hi, NT_DIMS, preferred_element_type=F32)
                      + lax.dot_general(h_lo, wf_hi, NT_DIMS, preferred_element_type=F32)
                      + lax.dot_general(h_hi, wf_lo, NT_DIMS, preferred_element_type=F32))

    y = lax.dot_general(h_ref[...], w_ref[...], NT_DIMS, preferred_element_type=F32)
    y = y * jnp.where(j < nq_tiles, sm_scale, 1.0)
    qkv_ref[...] = y.astype(BF16)


def _fox_proj(x2, mod, w_qkv_t, w_in_t, layer, nheads, seq, sm_scale):
    t, d = x2.shape
    n = w_qkv_t.shape[1]
    assert n % LANES == 0 and nheads <= LANES, (n, nheads)
    hp = LANES
    tm = _tile(seq, 512, 16)
    tn = _tile(d, 1024, 128)
    bps = seq // tm
    return pl.pallas_call(
        functools.partial(_fox_proj_kernel, nq_tiles=d // tn, sm_scale=sm_scale, nheads=nheads),
        grid=(t // tm, n // tn),
        in_specs=[
            pl.BlockSpec((tm, d), lambda i, j: (i, 0)),
            pl.BlockSpec((1, N_MOD, d), lambda i, j: (i // bps, 0, 0)),
            pl.BlockSpec((None, tn, d), lambda i, j: (layer, j, 0)),
            pl.BlockSpec((None, hp, d), lambda i, j: (layer, n // hp, 0)),
        ],
        out_specs=[
            pl.BlockSpec((tm, tn), lambda i, j: (i, j)),
            pl.BlockSpec((tm, hp), lambda i, j: (i, 0)),
        ],
        out_shape=[jax.ShapeDtypeStruct((t, n), BF16), jax.ShapeDtypeStruct((t, hp), F32)],
        scratch_shapes=[pltpu.VMEM((tm, d), BF16)],
        compiler_params=_params("parallel", "arbitrary"),
        name="fox_proj",
    )(x2, mod, w_qkv_t, w_in_t)


def _split3(v):
    p1 = v.astype(BF16)
    r1 = v - p1.astype(F32)
    p2 = r1.astype(BF16)
    p3 = (r1 - p2.astype(F32)).astype(BF16)
    return p1, p2, p3


def _decay_kernel(f_ref, bf_ref, o_ref, *, chunk):
    s = f_ref.shape[1]
    row = lax.broadcasted_iota(jnp.int32, (chunk, chunk), 0)
    col = lax.broadcasted_iota(jnp.int32, (chunk, chunk), 1)
    tri = (col <= row).astype(BF16)
    carry = jnp.zeros((1, f_ref.shape[2]), F32)
    for c in range(s // chunk):
        rows = slice(c * chunk, (c + 1) * chunk)
        lf = jax.nn.log_sigmoid(f_ref[0, rows, :] + bf_ref[...])
        p1, p2, p3 = _split3(lf)
        cs = (jnp.dot(tri, p1, preferred_element_type=F32)
              + jnp.dot(tri, p2, preferred_element_type=F32)
              + jnp.dot(tri, p3, preferred_element_type=F32)) + carry
        o_ref[0, rows, :] = cs
        carry = cs[chunk - 1:chunk, :]


def _decay_cumsum(f, b_f):
    nb, s, hp = f.shape
    chunk = _tile(s, 256, 16)
    return pl.pallas_call(
        functools.partial(_decay_kernel, chunk=chunk),
        grid=(nb,),
        in_specs=[pl.BlockSpec((1, s, hp), lambda b: (b, 0, 0)), pl.BlockSpec((1, hp), lambda b: (0, 0))],
        out_specs=pl.BlockSpec((1, s, hp), lambda b: (b, 0, 0)),
        out_shape=jax.ShapeDtypeStruct((nb, s, hp), F32),
        compiler_params=_params("parallel"),
        name="fox_decay",
    )(f, b_f)


def _attn_kernel(q_ref, k_ref, v_ref, cq_ref, ck_ref, o_ref, m_ref, l_ref, acc_ref, cqb_ref, *, tq, rs):
    qi = pl.program_id(2)
    nch = tq // LANES
    m_ref[...] = jnp.full_like(m_ref, -jnp.inf)
    l_ref[...] = jnp.zeros_like(l_ref)
    acc_ref[...] = jnp.zeros_like(acc_ref)
    cqb_ref[...] = jnp.broadcast_to(cq_ref[0, 0], cqb_ref.shape)

    def step(off, masked):
        kb = k_ref[0, pl.ds(off, tq), :]
        vb = v_ref[0, pl.ds(off, tq), :]
        ck = ck_ref[0, 0, :, pl.ds(off, tq)]
        for sb in range(tq // rs):
            rows = slice(sb * rs, (sb + 1) * rs)
            s = lax.dot_general(q_ref[0, rows, :], kb, (((1,), (1,)), ((), ())), preferred_element_type=F32)
            u = []
            for c in range(nch):
                uc = s[:, c * LANES:(c + 1) * LANES] - ck[:, c * LANES:(c + 1) * LANES]
                if masked:
                    row = lax.broadcasted_iota(jnp.int32, (rs, LANES), 0) + sb * rs
                    col = lax.broadcasted_iota(jnp.int32, (rs, LANES), 1) + c * LANES
                    uc = jnp.where(col <= row, uc, -jnp.inf)
                u.append(uc)
            bm = functools.reduce(jnp.maximum, u)
            cqb = cqb_ref[rows, :]
            m_old = m_ref[rows, :]
            m_new = jnp.maximum(m_old, jnp.max(bm, axis=1, keepdims=True) + cqb)
            a = jnp.exp(m_old - m_new)
            r = cqb - m_new
            p = [jnp.exp(uc + r) for uc in u]
            l_ref[rows, :] = a * l_ref[rows, :] + functools.reduce(jnp.add, p)
            pb = jnp.concatenate([pc.astype(BF16) for pc in p], axis=1)
            acc_ref[rows, :] = a * acc_ref[rows, :] + jnp.dot(pb, vb, preferred_element_type=F32)
            m_ref[rows, :] = m_new

    def body(ki, carry):
        step(pl.multiple_of(ki * tq, tq), False)
        return carry

    lax.fori_loop(0, qi, body, 0)
    step(pl.multiple_of(qi * tq, tq), True)
    l = jnp.sum(l_ref[...], axis=1, keepdims=True)
    o_ref[0] = (acc_ref[...] / l).astype(o_ref.dtype)


def _fox_attention(qkv, cq, ck, nheads, dh):
    nb, s, _ = qkv.shape
    d = nheads * dh
    assert dh == LANES, dh
    tq = _tile(s, 1024, LANES)
    rs = _tile(tq, 512, LANES)
    stat = pltpu.VMEM((tq, LANES), F32)
    return pl.pallas_call(
        functools.partial(_attn_kernel, tq=tq, rs=rs),
        grid=(nb, nheads, s // tq),
        in_specs=[
            pl.BlockSpec((1, tq, dh), lambda b, h, i: (b, i, h)),
            pl.BlockSpec((1, s, dh), lambda b, h, i: (b, 0, nheads + h)),
            pl.BlockSpec((1, s, dh), lambda b, h, i: (b, 0, 2 * nheads + h)),
            pl.BlockSpec((1, 1, tq, 1), lambda b, h, i: (b, h, i, 0)),
            pl.BlockSpec((1, 1, 1, s), lambda b, h, i: (b, h, 0, 0)),
        ],
        out_specs=pl.BlockSpec((1, tq, dh), lambda b, h, i: (b, i, h)),
        out_shape=jax.ShapeDtypeStruct((nb, s, d), BF16),
        scratch_shapes=[stat, stat, pltpu.VMEM((tq, dh), F32), stat],
        compiler_params=_params("parallel", "parallel", "parallel"),
        name="fox_attention",
    )(qkv, qkv, qkv, cq, ck)


def _out_ln_kernel(o_ref, w_ref, x_ref, mod_ref, g_ref, b_ref, out_ref, *, alpha, tn, nj, gate_row):
    j = pl.program_id(1)
    y = jnp.dot(o_ref[...], w_ref[...], preferred_element_type=F32)
    out_ref[:, pl.ds(pl.multiple_of(j * tn, tn), tn)] = y

    @pl.when(j == nj - 1)
    def _():
        gate = mod_ref[0][gate_row:gate_row + 1]
        r = alpha * x_ref[...] + (1.0 + gate) * out_ref[...]
        out_ref[...] = _layer_norm(r, g_ref[...], b_ref[...])


def _fox_out_layer(o2, w_o, layer, x2, mod, ln_g, ln_b, seq, alpha):
    t, d = x2.shape
    tm = _tile(seq, 512, 16)
    tn = _tile(d, 512, 128)
    bps = seq // tm
    nj = d // tn
    return pl.pallas_call(
        functools.partial(_out_ln_kernel, alpha=alpha, tn=tn, nj=nj, gate_row=2),
        grid=(t // tm, nj),
        in_specs=[
            pl.BlockSpec((tm, d), lambda i, j: (i, 0)),
            pl.BlockSpec((None, d, tn), lambda i, j: (layer, 0, j)),
            pl.BlockSpec((tm, d), lambda i, j: (i, 0), **ONE_BUFFER),
            pl.BlockSpec((1, N_MOD, d), lambda i, j: (i // bps, 0, 0)),
            pl.BlockSpec((1, d), lambda i, j: (0, 0)),
            pl.BlockSpec((1, d), lambda i, j: (0, 0)),
        ],
        out_specs=pl.BlockSpec((tm, d), lambda i, j: (i, 0), **ONE_BUFFER),
        out_shape=jax.ShapeDtypeStruct((t, d), F32),
        compiler_params=_params("parallel", "arbitrary"),
        name="fox_out_layer",
    )(o2, w_o, x2, mod, ln_g.reshape(1, d), ln_b.reshape(1, d))


def _top_ranked(s, k, exact):
    rows = s.shape[0]
    row = lax.broadcasted_iota(jnp.int32, s.shape, 0)
    rank = jnp.full(s.shape, float(k), F32)
    vals = []
    for r in range(k):
        m = jnp.max(s, axis=0, keepdims=True)
        sel = s == m
        if exact:
            first = jnp.min(jnp.where(sel, row, rows), axis=0, keepdims=True)
            sel = row == first
        rank = jnp.where(sel, float(r), rank)
        vals.append(m)
        s = jnp.where(sel, -jnp.inf, s)
    return jnp.concatenate(vals, axis=0), rank


def _overfull(ranks, topk):
    fk = float(topk)
    counts = [jnp.sum((rk < fk).astype(F32), axis=0, keepdims=True) for rk in ranks]
    return jnp.max(functools.reduce(jnp.maximum, [jnp.abs(c - fk) for c in counts]))


def _key_stage(q, keys_ref, topk, exact):
    dhalf = q.shape[1] // 2
    out = []
    for p in range(2):
        qp = q[:, p * dhalf:(p + 1) * dhalf]
        s = lax.dot_general(keys_ref[p], qp, NT_DIMS, preferred_element_type=F32)
        out.append((s,) + _top_ranked(s, topk, exact))
    return out


def _pair_stage(s1, tv1, rank1, tv2, n1_ref, e1_ref, hd, topk, exact):
    half = topk // 2
    sub = lax.broadcasted_iota(jnp.int32, (8, 1), 0)
    pieces = [tv1[0:1] + tv2]
    for a in range(1, half):
        pieces.append(jnp.where(sub < topk // (a + 1), tv1[a:a + 1] + tv2[0:8], -jnp.inf))
    pieces.append(tv1[half:topk] + tv2[0:1])
    cand = jnp.concatenate(pieces, axis=0)
    cvals, crank = _top_ranked(cand, topk, exact)
    taken = crank < float(topk)
    z = jnp.sum(jnp.where(taken, jnp.exp(cand - cvals[0:1]), 0.0), axis=0, keepdims=True)
    takenf = taken.astype(F32)

    n1 = jnp.zeros_like(rank1)
    start = 0
    for a in range(half):
        size = topk if a == 0 else 8
        n_a = jnp.sum(takenf[start:start + size], axis=0, keepdims=True)
        n1 = jnp.where(rank1 == float(a), n_a, n1)
        start += size
    for a in range(half, topk):
        n1 = jnp.where(rank1 == float(a), takenf[start + a - half:start + a - half + 1], n1)

    n1_ref[:, pl.ds(hd, 1), :] = n1[:, None, :]
    e1_ref[:, pl.ds(hd, 1), :] = (SQRT_HALF * jnp.exp(s1 - tv1[0:1]) / z)[:, None, :]
    return crank


def _router_kernel(x_ref, mod_ref, wq_ref, keys_ref, hbf_ref, n1_ref, e1_ref, r2_ref, e2_ref, *, topk):
    hd = pl.program_id(1)

    @pl.when(hd == 0)
    def _():
        m = mod_ref[0]
        hbf_ref[...] = (x_ref[...] * (1.0 + m[4:5]) + m[3:4]).astype(BF16)

    kh = hbf_ref.shape[1] // 2
    q = (jnp.dot(hbf_ref[:, :kh], wq_ref[:kh, :], preferred_element_type=F32)
         + jnp.dot(hbf_ref[:, kh:], wq_ref[kh:, :], preferred_element_type=F32)).astype(BF16)

    def write_second_half(s2, tv2, rank2):
        r2_ref[0] = rank2
        e2_ref[0] = jnp.exp(s2 - tv2[0:1])

    (s1, tv1, rank1), (s2, tv2, rank2) = _key_stage(q, keys_ref, topk, exact=False)
    key_tie = _overfull([rank1, rank2], topk)
    crank = _pair_stage(s1, tv1, rank1, tv2, n1_ref, e1_ref, hd, topk, exact=False)
    pair_tie = _overfull([crank], topk)
    write_second_half(s2, tv2, rank2)

    @pl.when(key_tie > 0.5)
    def _():
        (s1x, tv1x, rank1x), (s2x, tv2x, rank2x) = _key_stage(q, keys_ref, topk, exact=True)
        _pair_stage(s1x, tv1x, rank1x, tv2x, n1_ref, e1_ref, hd, topk, exact=True)
        write_second_half(s2x, tv2x, rank2x)

    @pl.when(jnp.logical_and(key_tie <= 0.5, pair_tie > 0.5))
    def _():
        _pair_stage(s1, tv1, rank1, tv2, n1_ref, e1_ref, hd, topk, exact=True)


def _peer_router(x2, mod, w_q, keys, layer, seq):
    t, d = x2.shape
    _, _, nkeys, dhalf = keys.shape
    ph = w_q.shape[2] // (2 * dhalf)
    tr = _tile(seq, 256, 128)
    bps = seq // tr
    route = jax.ShapeDtypeStruct((ph, nkeys, t), F32)
    route_spec = pl.BlockSpec((1, nkeys, tr), lambda i, h: (h, 0, i))
    key_major = jax.ShapeDtypeStruct((nkeys, ph, t), F32)
    key_spec = pl.BlockSpec((nkeys, ph, tr), lambda i, h: (0, 0, i))
    return pl.pallas_call(
        functools.partial(_router_kernel, topk=PEER_TOPK),
        grid=(t // tr, ph),
        in_specs=[
            pl.BlockSpec((tr, d), lambda i, h: (i, 0)),
            pl.BlockSpec((1, N_MOD, d), lambda i, h: (i // bps, 0, 0)),
            pl.BlockSpec((None, d, 2 * dhalf), lambda i, h: (layer, 0, h)),
            pl.BlockSpec((None, 2, nkeys, dhalf), lambda i, h: (layer, 0, 0, 0)),
        ],
        out_specs=[pl.BlockSpec((tr, d), lambda i, h: (i, 0)), key_spec, key_spec, route_spec, route_spec],
        out_shape=[jax.ShapeDtypeStruct((t, d), BF16), key_major, key_major, route, route],
        compiler_params=_params("parallel", "arbitrary"),
        name="peer_router",
    )(x2, mod, w_q, keys)


def _gate_block(a_scr, wt_scr, n1_ref, e1_ref, r2_ref, e2_ref, key0, nkeys, ph):
    ts, tm = a_scr.shape
    for il in range(ts // nkeys):
        i_key = key0 + il
        for lc in range(tm // LANES):
            cols = slice(lc * LANES, (lc + 1) * LANES)
            n1t = n1_ref[i_key, :, cols]
            e1t = e1_ref[i_key, :, cols]
            n1b = [n1t[hd:hd + 1, :] for hd in range(ph)]
            e1b = [e1t[hd:hd + 1, :] for hd in range(ph)]
            tiles = []
            for rc in range(nkeys // SUBLANES):
                rows = slice(rc * SUBLANES, (rc + 1) * SUBLANES)
                w = None
                for hd in range(ph):
                    t = jnp.where(r2_ref[hd, rows, cols] < n1b[hd], e2_ref[hd, rows, cols], 0.0) * e1b[hd]
                    w = t if w is None else w + t
                b = a_scr[il * nkeys + rc * SUBLANES:il * nkeys + (rc + 1) * SUBLANES, cols]
                tiles.append(w * (b * (1.0 + lax.erf(b))))
            blk = jnp.concatenate(tiles, axis=0).astype(BF16)
            wt_scr[cols, il * nkeys:(il + 1) * nkeys] = blk.T


def _expert_kernel(h_ref, u_ref, v_ref, n1_ref, e1_ref, r2_ref, e2_ref, x_ref, mod_ref, g_ref, b_ref, o_ref,
                   a_e, a_o, wt_e, wt_o, *, alpha, nkeys, ph, nj, ts):
    j = pl.program_id(1)
    tm, d = o_ref.shape
    gps = ts // nkeys

    @pl.when(j == 0)
    def _():
        o_ref[...] = jnp.zeros_like(o_ref)
        wt_e[...] = jnp.zeros_like(wt_e)
        a_o[...] = jnp.zeros_like(a_o)

    def mm1(rows, a_scr):
        half = tm // 2
        for s in range(2):
            a_scr[:, s * half:(s + 1) * half] = lax.dot_general(
                u_ref[rows, :], h_ref[s * half:(s + 1) * half, :], (((1,), (1,)), ((), ())),
                preferred_element_type=F32)

    def mm2(wt_scr, rows):
        half = d // 2
        for s in range(2):
            o_ref[:, s * half:(s + 1) * half] += jnp.dot(
                wt_scr[...], v_ref[rows, s * half:(s + 1) * half], preferred_element_type=F32)

    lo, hi = slice(0, ts), slice(ts, 2 * ts)
    last = 2 * nj - 1
    mm2(wt_e, lo)
    _gate_block(a_o, wt_o, n1_ref, e1_ref, r2_ref, e2_ref, jnp.maximum(2 * j - 1, 0) * gps, nkeys, ph)
    mm1(lo, a_e)
    mm2(wt_o, hi)
    _gate_block(a_e, wt_e, n1_ref, e1_ref, r2_ref, e2_ref, jnp.minimum(2 * j, last) * gps, nkeys, ph)
    mm1(hi, a_o)

    @pl.when(j == nj)
    def _():
        g_c = mod_ref[0][5:6]
        r = alpha * x_ref[...] + (1.0 + g_c) * o_ref[...]
        o_ref[...] = _layer_norm(r, g_ref[...], b_ref[...])


def _peer_experts(hbf, u, v, layer, n1t, e1t, r2, e2, x2, mod, ln_g, ln_b, seq, alpha):
    t, d = x2.shape
    ne = u.shape[1]
    ph, nkeys, _ = r2.shape
    tm = _tile(seq, 512, 2 * LANES)
    ts = _tile(ne // 2, 256, nkeys)
    bps = seq // tm
    nj = ne // (2 * ts)
    route_spec = pl.BlockSpec((ph, nkeys, tm), lambda i, j: (0, 0, i), **ONE_BUFFER)
    key_spec = pl.BlockSpec((nkeys, ph, tm), lambda i, j: (0, 0, i), **ONE_BUFFER)
    return pl.pallas_call(
        functools.partial(_expert_kernel, alpha=alpha, nkeys=nkeys, ph=ph, nj=nj, ts=ts),
        grid=(t // tm, nj + 1),
        in_specs=[
            pl.BlockSpec((tm, d), lambda i, j: (i, 0), **ONE_BUFFER),
            pl.BlockSpec((None, 2 * ts, d), lambda i, j: (layer, jnp.minimum(j, nj - 1), 0)),
            pl.BlockSpec((None, 2 * ts, d), lambda i, j: (layer, jnp.maximum(j - 1, 0), 0)),
            key_spec, key_spec, route_spec, route_spec,
            pl.BlockSpec((tm, d), lambda i, j: (i, 0), **ONE_BUFFER),
            pl.BlockSpec((1, N_MOD, d), lambda i, j: (i // bps, 0, 0)),
            pl.BlockSpec((1, d), lambda i, j: (0, 0)),
            pl.BlockSpec((1, d), lambda i, j: (0, 0)),
        ],
        out_specs=pl.BlockSpec((tm, d), lambda i, j: (i, 0), **ONE_BUFFER),
        out_shape=jax.ShapeDtypeStruct((t, d), F32),
        scratch_shapes=[pltpu.VMEM((ts, tm), F32), pltpu.VMEM((ts, tm), F32),
                        pltpu.VMEM((tm, ts), BF16), pltpu.VMEM((tm, ts), BF16)],
        compiler_params=_params("parallel", "arbitrary"),
        name="peer_experts",
    )(hbf, u, v, n1t, e1t, r2, e2, x2, mod, ln_g.reshape(1, d), ln_b.reshape(1, d))


def kernel(x, c, w_c, b_c, ada_table, ln_tok_g, ln_tok_b, ln_ch_g, ln_ch_b, pool_w, pool_scale,
           fox_w_in, fox_b_f, fox_w_o, peer_w_q, peer_keys, peer_u, peer_v):
    nb, seq, d = x.shape
    depth = ada_table.shape[0]
    alpha = float((2 * depth) ** 0.25)
    nheads = fox_b_f.shape[1]
    dh = d // nheads
    hpad = -(-nheads // 128) * 128

    mod_all = _adaln(c, w_c, b_c, ada_table)

    pool_w16 = pool_w.astype(BF16)
    w_in_t = jnp.swapaxes(fox_w_in, 1, 2)
    w_qkv16 = _cast_rows(w_in_t, 3 * d)
    b_f = jnp.pad(fox_b_f, ((0, 0), (0, hpad - nheads)))
    w_o16 = fox_w_o.astype(BF16)
    w_q16 = peer_w_q.astype(BF16)
    keys16 = peer_keys.astype(BF16)
    u16 = (peer_u * SQRT_HALF).astype(BF16)
    v16 = peer_v.astype(BF16)

    n_mixers = 2
    for i in range(depth):
        mod = mod_all[:, i].reshape(nb, N_MOD, d)
        jl = i // n_mixers
        if i % n_mixers == 0:
            x = _pool_layer(x, mod, pool_w16, jl, pool_scale[jl], ln_tok_g[i], ln_tok_b[i], alpha)
        else:
            x2 = x.reshape(nb * seq, d)
            qkv, f = _fox_proj(x2, mod, w_qkv16, w_in_t, jl, nheads, seq, float(dh) ** -0.5)
            cum = _decay_cumsum(f.reshape(nb, seq, hpad), b_f[jl].reshape(1, hpad))[:, :, :nheads]
            cum_t = cum.transpose(0, 2, 1)
            o = _fox_attention(qkv.reshape(nb, seq, 3 * d), cum_t[:, :, :, None], cum_t[:, :, None, :], nheads, dh)
            x = _fox_out_layer(o.reshape(nb * seq, d), w_o16, jl, x2, mod, ln_tok_g[i], ln_tok_b[i], seq, alpha)
            x = x.reshape(nb, seq, d)
        x2 = x.reshape(nb * seq, d)
        hbf, n1t, e1t, r2, e2 = _peer_router(x2, mod, w_q16, keys16, i, seq)
        x = _peer_experts(hbf, u16, v16, i, n1t, e1t, r2, e2, x2, mod, ln_ch_g[i], ln_ch_b[i], seq, alpha)
        x = x.reshape(nb, seq, d)
    return x
```

```python
import functools

import jax
import jax.numpy as jnp
from jax import lax
from jax.experimental import pallas as pl
from jax.experimental.pallas import tpu as pltpu

F32 = jnp.float32
BF16 = jnp.bfloat16

LN_EPS = 1e-5
N_MOD = 6
POOL_WINDOWS = (2, 4, 8, 16)
POOL_HALO = 16
PEER_TOPK = 16
SQRT_HALF = 0.7071067811865476
LANES = 128
SUBLANES = 8
VMEM_LIMIT_BYTES = 56 * 1024 * 1024


NT_DIMS = (((1,), (1,)), ((), ()))
ONE_BUFFER = dict(pipeline_mode=pl.Buffered(1))


def _params(*semantics):
    return pltpu.CompilerParams(dimension_semantics=semantics, vmem_limit_bytes=VMEM_LIMIT_BYTES)


def _tile(dim, target, align):
    if dim <= target:
        return dim
    t = (target // align) * align
    while t > align and dim % t:
        t -= align
    assert dim % t == 0, (dim, target, align)
    return t


def _layer_norm(r, g, b):
    mu = jnp.mean(r, axis=-1, keepdims=True)
    xc = r - mu
    var = jnp.mean(xc * xc, axis=-1, keepdims=True)
    return xc * lax.rsqrt(var + LN_EPS) * g + b


def _adaln_kernel(ct_ref, w_ref, b_ref, ada_ref, o_ref, acc_ref, *, nb, nk):
    k = pl.program_id(1)

    @pl.when(k == 0)
    def _():
        acc_ref[...] = jnp.zeros_like(acc_ref)

    s = jax.nn.silu(ct_ref[...])
    w = w_ref[...]
    tk, tn = w.shape
    for b in range(nb):
        prod = w * s[:, b:b + 1]
        acc_ref[b] += jnp.sum(prod.reshape(tk // 8, 8, tn), axis=0)

    @pl.when(k == nk - 1)
    def _():
        for b in range(nb):
            t0 = jnp.sum(acc_ref[b], axis=0, keepdims=True) + b_ref[...]
            o_ref[b] = t0 + ada_ref[...]


def _adaln(c, w_c, b_c, ada_table):
    nb, d = c.shape
    n = w_c.shape[1]
    depth = ada_table.shape[0]
    tk = _tile(d, 512, 8)
    tn = _tile(n, 2048, 128)
    nk = d // tk
    return pl.pallas_call(
        functools.partial(_adaln_kernel, nb=nb, nk=nk),
        grid=(n // tn, nk),
        in_specs=[
            pl.BlockSpec((tk, nb), lambda j, k: (k, 0)),
            pl.BlockSpec((tk, tn), lambda j, k: (k, j)),
            pl.BlockSpec((1, tn), lambda j, k: (0, j)),
            pl.BlockSpec((depth, tn), lambda j, k: (0, j)),
        ],
        out_specs=pl.BlockSpec((nb, depth, tn), lambda j, k: (0, 0, j)),
        out_shape=jax.ShapeDtypeStruct((nb, depth, n), F32),
        scratch_shapes=[pltpu.VMEM((nb, 8, tn), F32)],
        compiler_params=_params("parallel", "arbitrary"),
        name="adaln",
    )(c.T, w_c, b_c.reshape(1, n), ada_table)


def _pool_kernel(x_ref, xp_ref, mod_ref, w_ref, scale_ref, g_ref, b_ref, o_ref, hh_ref, *, alpha, ts, dg):
    s_idx = pl.program_id(1)
    m = mod_ref[0]
    sh, sc, gate = m[0:1], m[1:2], m[2:3]
    x = x_ref[0]
    h = x * (1.0 + sc) + sh
    hp = xp_ref[0] * (1.0 + sc) + sh
    hh_ref[0:POOL_HALO] = jnp.where(s_idx > 0, hp, 0.0)
    hh_ref[POOL_HALO:] = h
    pos = lax.broadcasted_iota(jnp.int32, (ts, 1), 0) + s_idx * ts
    ys = []
    for gi, win in enumerate(POOL_WINDOWS):
        cols = slice(gi * dg, (gi + 1) * dg)
        acc = hh_ref[pl.ds(POOL_HALO, ts), cols]
        for k in range(1, win):
            acc = acc + hh_ref[pl.ds(POOL_HALO - k, ts), cols]
        cnt = jnp.minimum(pos + 1, win).astype(F32)
        z = acc / cnt - h[:, cols]
        ys.append(jnp.dot(z.astype(BF16), w_ref[gi], preferred_element_type=F32))
    y = jnp.concatenate(ys, axis=1) * scale_ref[...]
    r = alpha * x + (1.0 + gate) * y
    o_ref[0] = _layer_norm(r, g_ref[...], b_ref[...])


def _pool_layer(x, mod, w, layer, scale, ln_g, ln_b, alpha):
    nb, s, d = x.shape
    _, g, dg, _ = w.shape
    ts = _tile(s, 256, POOL_HALO)
    hb = ts // POOL_HALO
    return pl.pallas_call(
        functools.partial(_pool_kernel, alpha=alpha, ts=ts, dg=dg),
        grid=(nb, s // ts),
        in_specs=[
            pl.BlockSpec((1, ts, d), lambda b, i: (b, i, 0)),
            pl.BlockSpec((1, POOL_HALO, d), lambda b, i: (b, jnp.maximum(i * hb - 1, 0), 0)),
            pl.BlockSpec((1, N_MOD, d), lambda b, i: (b, 0, 0)),
            pl.BlockSpec((None, g, dg, dg), lambda b, i: (layer, 0, 0, 0)),
            pl.BlockSpec((1, d), lambda b, i: (0, 0)),
            pl.BlockSpec((1, d), lambda b, i: (0, 0)),
            pl.BlockSpec((1, d), lambda b, i: (0, 0)),
        ],
        out_specs=pl.BlockSpec((1, ts, d), lambda b, i: (b, i, 0)),
        out_shape=jax.ShapeDtypeStruct((nb, s, d), F32),
        scratch_shapes=[pltpu.VMEM((ts + POOL_HALO, d), F32)],
        compiler_params=_params("parallel", "parallel"),
        name="pool_layer",
    )(x, x, mod, w, scale.reshape(1, d), ln_g.reshape(1, d), ln_b.reshape(1, d))


def _cast_kernel(x_ref, o_ref):
    o_ref[...] = x_ref[...].astype(o_ref.dtype)


def _cast_rows(w, nrows):
    nl, _, cols = w.shape
    tk = _tile(nrows, 512, 16)
    tn = _tile(cols, 2048, LANES)
    return pl.pallas_call(
        _cast_kernel,
        grid=(nl, nrows // tk, cols // tn),
        in_specs=[pl.BlockSpec((None, tk, tn), lambda l, i, j: (l, i, j))],
        out_specs=pl.BlockSpec((None, tk, tn), lambda l, i, j: (l, i, j)),
        out_shape=jax.ShapeDtypeStruct((nl, nrows, cols), BF16),
        compiler_params=_params("parallel", "parallel", "parallel"),
        name="cast_rows",
    )(w)


def _fox_proj_kernel(x_ref, mod_ref, w_ref, wf_ref, qkv_ref, f_ref, h_ref, *, nq_tiles, sm_scale, nheads):
    j = pl.program_id(1)

    @pl.when(j == 0)
    def _():
        m = mod_ref[0]
        h = x_ref[...] * (1.0 + m[1:2]) + m[0:1]
        h_hi = h.astype(BF16)
        h_ref[...] = h_hi
        row = lax.broadcasted_iota(jnp.int32, wf_ref.shape, 0)
        wf = jnp.where(row < nheads, wf_ref[...], 0.0)
        wf_hi = wf.astype(BF16)
        wf_lo = (wf - wf_hi.astype(F32)).astype(BF16)
        h_lo = (h - h_hi.astype(F32)).astype(BF16)
        f_ref[...] = (lax.dot_general(h_hi, wf_hi, NT_DIMS, preferred_element_type=F32)
                      + lax.dot_general(h_lo, wf_hi, NT_DIMS, preferred_element_type=F32)
                      + lax.dot_general(h_hi, wf_lo, NT_DIMS, preferred_element_type=F32))

    y = lax.dot_general(h_ref[...], w_ref[...], NT_DIMS, preferred_element_type=F32)
    y = y * jnp.where(j < nq_tiles, sm_scale, 1.0)
    qkv_ref[...] = y.astype(BF16)


def _fox_proj(x2, mod, w_qkv_t, w_in_t, layer, nheads, seq, sm_scale):
    t, d = x2.shape
    n = w_qkv_t.shape[1]
    assert n % LANES == 0 and nheads <= LANES, (n, nheads)
    hp = LANES
    tm = _tile(seq, 512, 16)
    tn = _tile(d, 1024, 128)
    bps = seq // tm
    return pl.pallas_call(
        functools.partial(_fox_proj_kernel, nq_tiles=d // tn, sm_scale=sm_scale, nheads=nheads),
        grid=(t // tm, n // tn),
        in_specs=[
            pl.BlockSpec((tm, d), lambda i, j: (i, 0)),
            pl.BlockSpec((1, N_MOD, d), lambda i, j: (i // bps, 0, 0)),
            pl.BlockSpec((None, tn, d), lambda i, j: (layer, j, 0)),
            pl.BlockSpec((None, hp, d), lambda i, j: (layer, n // hp, 0)),
        ],
        out_specs=[
            pl.BlockSpec((tm, tn), lambda i, j: (i, j)),
            pl.BlockSpec((tm, hp), lambda i, j: (i, 0)),
        ],
        out_shape=[jax.ShapeDtypeStruct((t, n), BF16), jax.ShapeDtypeStruct((t, hp), F32)],
        scratch_shapes=[pltpu.VMEM((tm, d), BF16)],
        compiler_params=_params("parallel", "arbitrary"),
        name="fox_proj",
    )(x2, mod, w_qkv_t, w_in_t)


def _split3(v):
    p1 = v.astype(BF16)
    r1 = v - p1.astype(F32)
    p2 = r1.astype(BF16)
    p3 = (r1 - p2.astype(F32)).astype(BF16)
    return p1, p2, p3


def _decay_kernel(f_ref, bf_ref, o_ref, *, chunk):
    s = f_ref.shape[1]
    row = lax.broadcasted_iota(jnp.int32, (chunk, chunk), 0)
    col = lax.broadcasted_iota(jnp.int32, (chunk, chunk), 1)
    tri = (col <= row).astype(BF16)
    carry = jnp.zeros((1, f_ref.shape[2]), F32)
    for c in range(s // chunk):
        rows = slice(c * chunk, (c + 1) * chunk)
        lf = jax.nn.log_sigmoid(f_ref[0, rows, :] + bf_ref[...])
        p1, p2, p3 = _split3(lf)
        cs = (jnp.dot(tri, p1, preferred_element_type=F32)
              + jnp.dot(tri, p2, preferred_element_type=F32)
              + jnp.dot(tri, p3, preferred_element_type=F32)) + carry
        o_ref[0, rows, :] = cs
        carry = cs[chunk - 1:chunk, :]


def _decay_cumsum(f, b_f):
    nb, s, hp = f.shape
    chunk = _tile(s, 256, 16)
    return pl.pallas_call(
        functools.partial(_decay_kernel, chunk=chunk),
        grid=(nb,),
        in_specs=[pl.BlockSpec((1, s, hp), lambda b: (b, 0, 0)), pl.BlockSpec((1, hp), lambda b: (0, 0))],
        out_specs=pl.BlockSpec((1, s, hp), lambda b: (b, 0, 0)),
        out_shape=jax.ShapeDtypeStruct((nb, s, hp), F32),
        compiler_params=_params("parallel"),
        name="fox_decay",
    )(f, b_f)


def _attn_kernel(q_ref, k_ref, v_ref, cq_ref, ck_ref, o_ref, m_ref, l_ref, acc_ref, cqb_ref, *, tq, rs):
    qi = pl.program_id(2)
    nch = tq // LANES
    m_ref[...] = jnp.full_like(m_ref, -jnp.inf)
    l_ref[...] = jnp.zeros_like(l_ref)
    acc_ref[...] = jnp.zeros_like(acc_ref)
    cqb_ref[...] = jnp.broadcast_to(cq_ref[0, 0], cqb_ref.shape)

    def step(off, masked):
        kb = k_ref[0, pl.ds(off, tq), :]
        vb = v_ref[0, pl.ds(off, tq), :]
        ck = ck_ref[0, 0, :, pl.ds(off, tq)]
        for sb in range(tq // rs):
            rows = slice(sb * rs, (sb + 1) * rs)
            s = lax.dot_general(q_ref[0, rows, :], kb, (((1,), (1,)), ((), ())), preferred_element_type=F32)
            u = []
            for c in range(nch):
                uc = s[:, c * LANES:(c + 1) * LANES] - ck[:, c * LANES:(c + 1) * LANES]
                if masked:
                    row = lax.broadcasted_iota(jnp.int32, (rs, LANES), 0) + sb * rs
                    col = lax.broadcasted_iota(jnp.int32, (rs, LANES), 1) + c * LANES
                    uc = jnp.where(col <= row, uc, -jnp.inf)
                u.append(uc)
            bm = functools.reduce(jnp.maximum, u)
            cqb = cqb_ref[rows, :]
            m_old = m_ref[rows, :]
            m_new = jnp.maximum(m_old, jnp.max(bm, axis=1, keepdims=True) + cqb)
            a = jnp.exp(m_old - m_new)
            r = cqb - m_new
            p = [jnp.exp(uc + r) for uc in u]
            l_ref[rows, :] = a * l_ref[rows, :] + functools.reduce(jnp.add, p)
            pb = jnp.concatenate([pc.astype(BF16) for pc in p], axis=1)
            acc_ref[rows, :] = a * acc_ref[rows, :] + jnp.dot(pb, vb, preferred_element_type=F32)
            m_ref[rows, :] = m_new

    def body(ki, carry):
        step(pl.multiple_of(ki * tq, tq), False)
        return carry

    lax.fori_loop(0, qi, body, 0)
    step(pl.multiple_of(qi * tq, tq), True)
    l = jnp.sum(l_ref[...], axis=1, keepdims=True)
    o_ref[0] = (acc_ref[...] / l).astype(o_ref.dtype)


def _fox_attention(qkv, cq, ck, nheads, dh):
    nb, s, _ = qkv.shape
    d = nheads * dh
    assert dh == LANES, dh
    tq = _tile(s, 1024, LANES)
    rs = tq
    stat = pltpu.VMEM((tq, LANES), F32)
    return pl.pallas_call(
        functools.partial(_attn_kernel, tq=tq, rs=rs),
        grid=(nb, nheads, s // tq),
        in_specs=[
            pl.BlockSpec((1, tq, dh), lambda b, h, i: (b, i, h)),
            pl.BlockSpec((1, s, dh), lambda b, h, i: (b, 0, nheads + h)),
            pl.BlockSpec((1, s, dh), lambda b, h, i: (b, 0, 2 * nheads + h)),
            pl.BlockSpec((1, 1, tq, 1), lambda b, h, i: (b, h, i, 0)),
            pl.BlockSpec((1, 1, 1, s), lambda b, h, i: (b, h, 0, 0)),
        ],
        out_specs=pl.BlockSpec((1, tq, dh), lambda b, h, i: (b, i, h)),
        out_shape=jax.ShapeDtypeStruct((nb, s, d), BF16),
        scratch_shapes=[stat, stat, pltpu.VMEM((tq, dh), F32), stat],
        compiler_params=_params("parallel", "parallel", "parallel"),
        name="fox_attention",
    )(qkv, qkv, qkv, cq, ck)


def _out_ln_kernel(o_ref, w_ref, x_ref, mod_ref, g_ref, b_ref, out_ref, *, alpha, tn, nj, gate_row):
    j = pl.program_id(1)
    y = jnp.dot(o_ref[...], w_ref[...], preferred_element_type=F32)
    out_ref[:, pl.ds(pl.multiple_of(j * tn, tn), tn)] = y

    @pl.when(j == nj - 1)
    def _():
        gate = mod_ref[0][gate_row:gate_row + 1]
        r = alpha * x_ref[...] + (1.0 + gate) * out_ref[...]
        out_ref[...] = _layer_norm(r, g_ref[...], b_ref[...])


def _fox_out_layer(o2, w_o, layer, x2, mod, ln_g, ln_b, seq, alpha):
    t, d = x2.shape
    tm = _tile(seq, 512, 16)
    tn = _tile(d, 512, 128)
    bps = seq // tm
    nj = d // tn
    return pl.pallas_call(
        functools.partial(_out_ln_kernel, alpha=alpha, tn=tn, nj=nj, gate_row=2),
        grid=(t // tm, nj),
        in_specs=[
            pl.BlockSpec((tm, d), lambda i, j: (i, 0)),
            pl.BlockSpec((None, d, tn), lambda i, j: (layer, 0, j)),
            pl.BlockSpec((tm, d), lambda i, j: (i, 0), **ONE_BUFFER),
            pl.BlockSpec((1, N_MOD, d), lambda i, j: (i // bps, 0, 0)),
            pl.BlockSpec((1, d), lambda i, j: (0, 0)),
            pl.BlockSpec((1, d), lambda i, j: (0, 0)),
        ],
        out_specs=pl.BlockSpec((tm, d), lambda i, j: (i, 0), **ONE_BUFFER),
        out_shape=jax.ShapeDtypeStruct((t, d), F32),
        compiler_params=_params("parallel", "arbitrary"),
        name="fox_out_layer",
    )(o2, w_o, x2, mod, ln_g.reshape(1, d), ln_b.reshape(1, d))


def _top_ranked(s, k, exact):
    rows = s.shape[0]
    row = lax.broadcasted_iota(jnp.int32, s.shape, 0)
    rank = jnp.full(s.shape, float(k), F32)
    vals = []
    for r in range(k):
        m = jnp.max(s, axis=0, keepdims=True)
        sel = s == m
        if exact:
            first = jnp.min(jnp.where(sel, row, rows), axis=0, keepdims=True)
            sel = row == first
        rank = jnp.where(sel, float(r), rank)
        vals.append(m)
        s = jnp.where(sel, -jnp.inf, s)
    return jnp.concatenate(vals, axis=0), rank


def _overfull(ranks, topk):
    fk = float(topk)
    counts = [jnp.sum((rk < fk).astype(F32), axis=0, keepdims=True) for rk in ranks]
    return jnp.max(functools.reduce(jnp.maximum, [jnp.abs(c - fk) for c in counts]))


def _key_stage(q, keys_ref, topk, exact):
    dhalf = q.shape[1] // 2
    out = []
    for p in range(2):
        qp = q[:, p * dhalf:(p + 1) * dhalf]
        s = lax.dot_general(keys_ref[p], qp, NT_DIMS, preferred_element_type=F32)
        out.append((s,) + _top_ranked(s, topk, exact))
    return out


def _pair_stage(s1, tv1, rank1, tv2, n1_ref, e1_ref, hd, topk, exact):
    half = topk // 2
    sub = lax.broadcasted_iota(jnp.int32, (8, 1), 0)
    pieces = [tv1[0:1] + tv2]
    for a in range(1, half):
        pieces.append(jnp.where(sub < topk // (a + 1), tv1[a:a + 1] + tv2[0:8], -jnp.inf))
    pieces.append(tv1[half:topk] + tv2[0:1])
    cand = jnp.concatenate(pieces, axis=0)
    cvals, crank = _top_ranked(cand, topk, exact)
    taken = crank < float(topk)
    z = jnp.sum(jnp.where(taken, jnp.exp(cand - cvals[0:1]), 0.0), axis=0, keepdims=True)
    takenf = taken.astype(F32)

    n1 = jnp.zeros_like(rank1)
    start = 0
    for a in range(half):
        size = topk if a == 0 else 8
        n_a = jnp.sum(takenf[start:start + size], axis=0, keepdims=True)
        n1 = jnp.where(rank1 == float(a), n_a, n1)
        start += size
    for a in range(half, topk):
        n1 = jnp.where(rank1 == float(a), takenf[start + a - half:start + a - half + 1], n1)

    n1_ref[:, pl.ds(hd, 1), :] = n1[:, None, :]
    e1_ref[:, pl.ds(hd, 1), :] = (SQRT_HALF * jnp.exp(s1 - tv1[0:1]) / z)[:, None, :]
    return crank


def _router_kernel(x_ref, mod_ref, wq_ref, keys_ref, hbf_ref, n1_ref, e1_ref, r2_ref, e2_ref, *, topk):
    hd = pl.program_id(1)

    @pl.when(hd == 0)
    def _():
        m = mod_ref[0]
        hbf_ref[...] = (x_ref[...] * (1.0 + m[4:5]) + m[3:4]).astype(BF16)

    kh = hbf_ref.shape[1] // 2
    q = (jnp.dot(hbf_ref[:, :kh], wq_ref[:kh, :], preferred_element_type=F32)
         + jnp.dot(hbf_ref[:, kh:], wq_ref[kh:, :], preferred_element_type=F32)).astype(BF16)

    def write_second_half(s2, tv2, rank2):
        r2_ref[0] = rank2
        e2_ref[0] = jnp.exp(s2 - tv2[0:1])

    (s1, tv1, rank1), (s2, tv2, rank2) = _key_stage(q, keys_ref, topk, exact=False)
    key_tie = _overfull([rank1, rank2], topk)
    crank = _pair_stage(s1, tv1, rank1, tv2, n1_ref, e1_ref, hd, topk, exact=False)
    pair_tie = _overfull([crank], topk)
    write_second_half(s2, tv2, rank2)

    @pl.when(key_tie > 0.5)
    def _():
        (s1x, tv1x, rank1x), (s2x, tv2x, rank2x) = _key_stage(q, keys_ref, topk, exact=True)
        _pair_stage(s1x, tv1x, rank1x, tv2x, n1_ref, e1_ref, hd, topk, exact=True)
        write_second_half(s2x, tv2x, rank2x)

    @pl.when(jnp.logical_and(key_tie <= 0.5, pair_tie > 0.5))
    def _():
        _pair_stage(s1, tv1, rank1, tv2, n1_ref, e1_ref, hd, topk, exact=True)


def _peer_router(x2, mod, w_q, keys, layer, seq):
    t, d = x2.shape
    _, _, nkeys, dhalf = keys.shape
    ph = w_q.shape[2] // (2 * dhalf)
    tr = _tile(seq, 512, 128)
    bps = seq // tr
    route = jax.ShapeDtypeStruct((ph, nkeys, t), F32)
    route_spec = pl.BlockSpec((1, nkeys, tr), lambda i, h: (h, 0, i))
    key_major = jax.ShapeDtypeStruct((nkeys, ph, t), F32)
    key_spec = pl.BlockSpec((nkeys, ph, tr), lambda i, h: (0, 0, i))
    return pl.pallas_call(
        functools.partial(_router_kernel, topk=PEER_TOPK),
        grid=(t // tr, ph),
        in_specs=[
            pl.BlockSpec((tr, d), lambda i, h: (i, 0)),
            pl.BlockSpec((1, N_MOD, d), lambda i, h: (i // bps, 0, 0)),
            pl.BlockSpec((None, d, 2 * dhalf), lambda i, h: (layer, 0, h)),
            pl.BlockSpec((None, 2, nkeys, dhalf), lambda i, h: (layer, 0, 0, 0)),
        ],
        out_specs=[pl.BlockSpec((tr, d), lambda i, h: (i, 0)), key_spec, key_spec, route_spec, route_spec],
        out_shape=[jax.ShapeDtypeStruct((t, d), BF16), key_major, key_major, route, route],
        compiler_params=_params("parallel", "arbitrary"),
        name="peer_router",
    )(x2, mod, w_q, keys)


def _gate_block(a_scr, wt_scr, n1_ref, e1_ref, r2_ref, e2_ref, key0, nkeys, ph):
    ts, tm = a_scr.shape
    for il in range(ts // nkeys):
        i_key = key0 + il
        for lc in range(tm // LANES):
            cols = slice(lc * LANES, (lc + 1) * LANES)
            n1t = n1_ref[i_key, :, cols]
            e1t = e1_ref[i_key, :, cols]
            n1b = [n1t[hd:hd + 1, :] for hd in range(ph)]
            e1b = [e1t[hd:hd + 1, :] for hd in range(ph)]
            tiles = []
            for rc in range(nkeys // SUBLANES):
                rows = slice(rc * SUBLANES, (rc + 1) * SUBLANES)
                w = None
                for hd in range(ph):
                    t = jnp.where(r2_ref[hd, rows, cols] < n1b[hd], e2_ref[hd, rows, cols], 0.0) * e1b[hd]
                    w = t if w is None else w + t
                b = a_scr[il * nkeys + rc * SUBLANES:il * nkeys + (rc + 1) * SUBLANES, cols]
                tiles.append(w * (b * (1.0 + lax.erf(b))))
            blk = jnp.concatenate(tiles, axis=0).astype(BF16)
            wt_scr[cols, il * nkeys:(il + 1) * nkeys] = blk.T


def _expert_kernel(h_ref, u_ref, v_ref, n1_ref, e1_ref, r2_ref, e2_ref, x_ref, mod_ref, g_ref, b_ref, o_ref,
                   a_e, a_o, wt_e, wt_o, *, alpha, nkeys, ph, nj, ts):
    j = pl.program_id(1)
    tm, d = o_ref.shape
    gps = ts // nkeys

    @pl.when(j == 0)
    def _():
        o_ref[...] = jnp.zeros_like(o_ref)
        wt_e[...] = jnp.zeros_like(wt_e)
        a_o[...] = jnp.zeros_like(a_o)

    def mm1(rows, a_scr):
        half = tm // 2
        for s in range(2):
            a_scr[:, s * half:(s + 1) * half] = lax.dot_general(
                u_ref[rows, :], h_ref[s * half:(s + 1) * half, :], (((1,), (1,)), ((), ())),
                preferred_element_type=F32)

    def mm2(wt_scr, rows):
        half = d // 2
        for s in range(2):
            o_ref[:, s * half:(s + 1) * half] += jnp.dot(
                wt_scr[...], v_ref[rows, s * half:(s + 1) * half], preferred_element_type=F32)

    lo, hi = slice(0, ts), slice(ts, 2 * ts)
    last = 2 * nj - 1
    mm2(wt_e, lo)
    _gate_block(a_o, wt_o, n1_ref, e1_ref, r2_ref, e2_ref, jnp.maximum(2 * j - 1, 0) * gps, nkeys, ph)
    mm1(lo, a_e)
    mm2(wt_o, hi)
    _gate_block(a_e, wt_e, n1_ref, e1_ref, r2_ref, e2_ref, jnp.minimum(2 * j, last) * gps, nkeys, ph)
    mm1(hi, a_o)

    @pl.when(j == nj)
    def _():
        g_c = mod_ref[0][5:6]
        r = alpha * x_ref[...] + (1.0 + g_c) * o_ref[...]
        o_ref[...] = _layer_norm(r, g_ref[...], b_ref[...])


def _peer_experts(hbf, u, v, layer, n1t, e1t, r2, e2, x2, mod, ln_g, ln_b, seq, alpha):
    t, d = x2.shape
    ne = u.shape[1]
    ph, nkeys, _ = r2.shape
    tm = _tile(seq, 512, 2 * LANES)
    ts = _tile(ne // 2, 256, nkeys)
    bps = seq // tm
    nj = ne // (2 * ts)
    route_spec = pl.BlockSpec((ph, nkeys, tm), lambda i, j: (0, 0, i), **ONE_BUFFER)
    key_spec = pl.BlockSpec((nkeys, ph, tm), lambda i, j: (0, 0, i), **ONE_BUFFER)
    return pl.pallas_call(
        functools.partial(_expert_kernel, alpha=alpha, nkeys=nkeys, ph=ph, nj=nj, ts=ts),
        grid=(t // tm, nj + 1),
        in_specs=[
            pl.BlockSpec((tm, d), lambda i, j: (i, 0), **ONE_BUFFER),
            pl.BlockSpec((None, 2 * ts, d), lambda i, j: (layer, jnp.minimum(j, nj - 1), 0)),
            pl.BlockSpec((None, 2 * ts, d), lambda i, j: (layer, jnp.maximum(j - 1, 0), 0)),
            key_spec, key_spec, route_spec, route_spec,
            pl.BlockSpec((tm, d), lambda i, j: (i, 0), **ONE_BUFFER),
            pl.BlockSpec((1, N_MOD, d), lambda i, j: (i // bps, 0, 0)),
            pl.BlockSpec((1, d), lambda i, j: (0, 0)),
            pl.BlockSpec((1, d), lambda i, j: (0, 0)),
        ],
        out_specs=pl.BlockSpec((tm, d), lambda i, j: (i, 0), **ONE_BUFFER),
        out_shape=jax.ShapeDtypeStruct((t, d), F32),
        scratch_shapes=[pltpu.VMEM((ts, tm), F32), pltpu.VMEM((ts, tm), F32),
                        pltpu.VMEM((tm, ts), BF16), pltpu.VMEM((tm, ts), BF16)],
        compiler_params=_params("parallel", "arbitrary"),
        name="peer_experts",
    )(hbf, u, v, n1t, e1t, r2, e2, x2, mod, ln_g.reshape(1, d), ln_b.reshape(1, d))


def kernel(x, c, w_c, b_c, ada_table, ln_tok_g, ln_tok_b, ln_ch_g, ln_ch_b, pool_w, pool_scale,
           fox_w_in, fox_b_f, fox_w_o, peer_w_q, peer_keys, peer_u, peer_v):
    nb, seq, d = x.shape
    depth = ada_table.shape[0]
    alpha = float((2 * depth) ** 0.25)
    nheads = fox_b_f.shape[1]
    dh = d // nheads
    hpad = -(-nheads // 128) * 128

    mod_all = _adaln(c, w_c, b_c, ada_table)

    pool_w16 = pool_w.astype(BF16)
    w_in_t = jnp.swapaxes(fox_w_in, 1, 2)
    w_qkv16 = _cast_rows(w_in_t, 3 * d)
    b_f = jnp.pad(fox_b_f, ((0, 0), (0, hpad - nheads)))
    w_o16 = fox_w_o.astype(BF16)
    w_q16 = peer_w_q.astype(BF16)
    keys16 = peer_keys.astype(BF16)
    u16 = (peer_u * SQRT_HALF).astype(BF16)
    v16 = peer_v.astype(BF16)

    n_mixers = 2
    for i in range(depth):
        mod = mod_all[:, i].reshape(nb, N_MOD, d)
        jl = i // n_mixers
        if i % n_mixers == 0:
            x = _pool_layer(x, mod, pool_w16, jl, pool_scale[jl], ln_tok_g[i], ln_tok_b[i], alpha)
        else:
            x2 = x.reshape(nb * seq, d)
            qkv, f = _fox_proj(x2, mod, w_qkv16, w_in_t, jl, nheads, seq, float(dh) ** -0.5)
            cum = _decay_cumsum(f.reshape(nb, seq, hpad), b_f[jl].reshape(1, hpad))[:, :, :nheads]
            cum_t = cum.transpose(0, 2, 1)
            o = _fox_attention(qkv.reshape(nb, seq, 3 * d), cum_t[:, :, :, None], cum_t[:, :, None, :], nheads, dh)
            x = _fox_out_layer(o.reshape(nb * seq, d), w_o16, jl, x2, mod, ln_tok_g[i], ln_tok_b[i], seq, alpha)
            x = x.reshape(nb, seq, d)
        x2 = x.reshape(nb * seq, d)
        hbf, n1t, e1t, r2, e2 = _peer_router(x2, mod, w_q16, keys16, i, seq)
        x = _peer_experts(hbf, u16, v16, i, n1t, e1t, r2, e2, x2, mod, ln_ch_g[i], ln_ch_b[i], seq, alpha)
        x = x.reshape(nb, seq, d)
    return x
```

```python
import functools

import jax
import jax.numpy as jnp
from jax import lax
from jax.experimental import pallas as pl
from jax.experimental.pallas import tpu as pltpu

F32 = jnp.float32
BF16 = jnp.bfloat16

LN_EPS = 1e-5
N_MOD = 6
POOL_WINDOWS = (2, 4, 8, 16)
POOL_HALO = 16
PEER_TOPK = 16
SQRT_HALF = 0.7071067811865476
LANES = 128
SUBLANES = 8
PACKED_ROWS = 16
VMEM_LIMIT_BYTES = 56 * 1024 * 1024


NT_DIMS = (((1,), (1,)), ((), ()))
ONE_BUFFER = dict(pipeline_mode=pl.Buffered(1))


def _params(*semantics):
    return pltpu.CompilerParams(dimension_semantics=semantics, vmem_limit_bytes=VMEM_LIMIT_BYTES)


def _tile(dim, target, align):
    if dim <= target:
        return dim
    t = (target // align) * align
    while t > align and dim % t:
        t -= align
    assert dim % t == 0, (dim, target, align)
    return t


def _layer_norm(r, g, b):
    mu = jnp.mean(r, axis=-1, keepdims=True)
    xc = r - mu
    var = jnp.mean(xc * xc, axis=-1, keepdims=True)
    return xc * lax.rsqrt(var + LN_EPS) * g + b


def _adaln_kernel(ct_ref, w_ref, b_ref, ada_ref, o_ref, acc_ref, *, nb, nk):
    k = pl.program_id(1)

    @pl.when(k == 0)
    def _():
        acc_ref[...] = jnp.zeros_like(acc_ref)

    s = jax.nn.silu(ct_ref[...])
    w = w_ref[...]
    tk, tn = w.shape
    for b in range(nb):
        prod = w * s[:, b:b + 1]
        acc_ref[b] += jnp.sum(prod.reshape(tk // SUBLANES, SUBLANES, tn), axis=0)

    @pl.when(k == nk - 1)
    def _():
        for b in range(nb):
            t0 = jnp.sum(acc_ref[b], axis=0, keepdims=True) + b_ref[...]
            o_ref[b] = t0 + ada_ref[...]


def _adaln(c, w_c, b_c, ada_table):
    nb, d = c.shape
    n = w_c.shape[1]
    depth = ada_table.shape[0]
    tk = _tile(d, 512, SUBLANES)
    tn = _tile(n, 2048, LANES)
    nk = d // tk
    return pl.pallas_call(
        functools.partial(_adaln_kernel, nb=nb, nk=nk),
        grid=(n // tn, nk),
        in_specs=[
            pl.BlockSpec((tk, nb), lambda j, k: (k, 0)),
            pl.BlockSpec((tk, tn), lambda j, k: (k, j)),
            pl.BlockSpec((1, tn), lambda j, k: (0, j)),
            pl.BlockSpec((depth, tn), lambda j, k: (0, j)),
        ],
        out_specs=pl.BlockSpec((nb, depth, tn), lambda j, k: (0, 0, j)),
        out_shape=jax.ShapeDtypeStruct((nb, depth, n), F32),
        scratch_shapes=[pltpu.VMEM((nb, SUBLANES, tn), F32)],
        compiler_params=_params("parallel", "arbitrary"),
        name="adaln",
    )(c.T, w_c, b_c.reshape(1, n), ada_table)


def _pool_kernel(x_ref, xp_ref, mod_ref, w_ref, scale_ref, g_ref, b_ref, o_ref, hh_ref, *, alpha, ts, dg):
    s_idx = pl.program_id(1)
    m = mod_ref[0]
    sh, sc, gate = m[0:1], m[1:2], m[2:3]
    x = x_ref[0]
    h = x * (1.0 + sc) + sh
    hp = xp_ref[0] * (1.0 + sc) + sh
    hh_ref[0:POOL_HALO] = jnp.where(s_idx > 0, hp, 0.0)
    hh_ref[POOL_HALO:] = h
    pos = lax.broadcasted_iota(jnp.int32, (ts, 1), 0) + s_idx * ts
    ys = []
    for gi, win in enumerate(POOL_WINDOWS):
        cols = slice(gi * dg, (gi + 1) * dg)
        acc = hh_ref[pl.ds(POOL_HALO, ts), cols]
        for k in range(1, win):
            acc = acc + hh_ref[pl.ds(POOL_HALO - k, ts), cols]
        cnt = jnp.minimum(pos + 1, win).astype(F32)
        z = acc / cnt - h[:, cols]
        ys.append(jnp.dot(z.astype(BF16), w_ref[gi], preferred_element_type=F32))
    y = jnp.concatenate(ys, axis=1) * scale_ref[...]
    r = alpha * x + (1.0 + gate) * y
    o_ref[0] = _layer_norm(r, g_ref[...], b_ref[...])


def _pool_layer(x, mod, w, layer, scale, ln_g, ln_b, alpha):
    nb, s, d = x.shape
    _, g, dg, _ = w.shape
    ts = _tile(s, 256, POOL_HALO)
    hb = ts // POOL_HALO
    return pl.pallas_call(
        functools.partial(_pool_kernel, alpha=alpha, ts=ts, dg=dg),
        grid=(nb, s // ts),
        in_specs=[
            pl.BlockSpec((1, ts, d), lambda b, i: (b, i, 0)),
            pl.BlockSpec((1, POOL_HALO, d), lambda b, i: (b, jnp.maximum(i * hb - 1, 0), 0)),
            pl.BlockSpec((1, N_MOD, d), lambda b, i: (b, 0, 0)),
            pl.BlockSpec((None, g, dg, dg), lambda b, i: (layer, 0, 0, 0)),
            pl.BlockSpec((1, d), lambda b, i: (0, 0)),
            pl.BlockSpec((1, d), lambda b, i: (0, 0)),
            pl.BlockSpec((1, d), lambda b, i: (0, 0)),
        ],
        out_specs=pl.BlockSpec((1, ts, d), lambda b, i: (b, i, 0)),
        out_shape=jax.ShapeDtypeStruct((nb, s, d), F32),
        scratch_shapes=[pltpu.VMEM((ts + POOL_HALO, d), F32)],
        compiler_params=_params("parallel", "parallel"),
        name="pool_layer",
    )(x, x, mod, w, scale.reshape(1, d), ln_g.reshape(1, d), ln_b.reshape(1, d))


def _cast_kernel(x_ref, o_ref):
    o_ref[...] = x_ref[...].astype(o_ref.dtype)


def _cast_rows(w, nrows):
    nl, _, cols = w.shape
    tk = _tile(nrows, 512, PACKED_ROWS)
    tn = _tile(cols, 2048, LANES)
    return pl.pallas_call(
        _cast_kernel,
        grid=(nl, nrows // tk, cols // tn),
        in_specs=[pl.BlockSpec((None, tk, tn), lambda l, i, j: (l, i, j))],
        out_specs=pl.BlockSpec((None, tk, tn), lambda l, i, j: (l, i, j)),
        out_shape=jax.ShapeDtypeStruct((nl, nrows, cols), BF16),
        compiler_params=_params("parallel", "parallel", "parallel"),
        name="cast_rows",
    )(w)


def _fox_proj_kernel(x_ref, mod_ref, w_ref, wf_ref, qkv_ref, f_ref, h_ref, *, nq_tiles, sm_scale, nheads):
    j = pl.program_id(1)

    @pl.when(j == 0)
    def _():
        m = mod_ref[0]
        h = x_ref[...] * (1.0 + m[1:2]) + m[0:1]
        h_hi = h.astype(BF16)
        h_ref[...] = h_hi
        row = lax.broadcasted_iota(jnp.int32, wf_ref.shape, 0)
        wf = jnp.where(row < nheads, wf_ref[...], 0.0)
        wf_hi = wf.astype(BF16)
        wf_lo = (wf - wf_hi.astype(F32)).astype(BF16)
        h_lo = (h - h_hi.astype(F32)).astype(BF16)
        f_ref[...] = (lax.dot_general(h_hi, wf_hi, NT_DIMS, preferred_element_type=F32)
                      + lax.dot_general(h_lo, wf_hi, NT_DIMS, preferred_element_type=F32)
                      + lax.dot_general(h_hi, wf_lo, NT_DIMS, preferred_element_type=F32))

    y = lax.dot_general(h_ref[...], w_ref[...], NT_DIMS, preferred_element_type=F32)
    y = y * jnp.where(j < nq_tiles, sm_scale, 1.0)
    qkv_ref[...] = y.astype(BF16)


def _fox_proj(x2, mod, w_qkv_t, w_in_t, layer, nheads, seq, sm_scale):
    t, d = x2.shape
    n = w_qkv_t.shape[1]
    assert n % LANES == 0 and nheads <= LANES, (n, nheads)
    hp = LANES
    tm = _tile(seq, 512, PACKED_ROWS)
    tn = _tile(d, 1024, LANES)
    bps = seq // tm
    return pl.pallas_call(
        functools.partial(_fox_proj_kernel, nq_tiles=d // tn, sm_scale=sm_scale, nheads=nheads),
        grid=(t // tm, n // tn),
        in_specs=[
            pl.BlockSpec((tm, d), lambda i, j: (i, 0)),
            pl.BlockSpec((1, N_MOD, d), lambda i, j: (i // bps, 0, 0)),
            pl.BlockSpec((None, tn, d), lambda i, j: (layer, j, 0)),
            pl.BlockSpec((None, hp, d), lambda i, j: (layer, n // hp, 0)),
        ],
        out_specs=[
            pl.BlockSpec((tm, tn), lambda i, j: (i, j)),
            pl.BlockSpec((tm, hp), lambda i, j: (i, 0)),
        ],
        out_shape=[jax.ShapeDtypeStruct((t, n), BF16), jax.ShapeDtypeStruct((t, hp), F32)],
        scratch_shapes=[pltpu.VMEM((tm, d), BF16)],
        compiler_params=_params("parallel", "arbitrary"),
        name="fox_proj",
    )(x2, mod, w_qkv_t, w_in_t)


def _split3(v):
    p1 = v.astype(BF16)
    r1 = v - p1.astype(F32)
    p2 = r1.astype(BF16)
    p3 = (r1 - p2.astype(F32)).astype(BF16)
    return p1, p2, p3


def _decay_kernel(f_ref, bf_ref, o_ref, *, chunk):
    s = f_ref.shape[1]
    row = lax.broadcasted_iota(jnp.int32, (chunk, chunk), 0)
    col = lax.broadcasted_iota(jnp.int32, (chunk, chunk), 1)
    tri = (col <= row).astype(BF16)
    carry = jnp.zeros((1, f_ref.shape[2]), F32)
    for c in range(s // chunk):
        rows = slice(c * chunk, (c + 1) * chunk)
        lf = jax.nn.log_sigmoid(f_ref[0, rows, :] + bf_ref[...])
        p1, p2, p3 = _split3(lf)
        cs = (jnp.dot(tri, p1, preferred_element_type=F32)
              + jnp.dot(tri, p2, preferred_element_type=F32)
              + jnp.dot(tri, p3, preferred_element_type=F32)) + carry
        o_ref[0, rows, :] = cs
        carry = cs[chunk - 1:chunk, :]


def _decay_cumsum(f, b_f):
    nb, s, hp = f.shape
    chunk = _tile(s, 256, PACKED_ROWS)
    return pl.pallas_call(
        functools.partial(_decay_kernel, chunk=chunk),
        grid=(nb,),
        in_specs=[pl.BlockSpec((1, s, hp), lambda b: (b, 0, 0)), pl.BlockSpec((1, hp), lambda b: (0, 0))],
        out_specs=pl.BlockSpec((1, s, hp), lambda b: (b, 0, 0)),
        out_shape=jax.ShapeDtypeStruct((nb, s, hp), F32),
        compiler_params=_params("parallel"),
        name="fox_decay",
    )(f, b_f)


def _attn_kernel(q_ref, k_ref, v_ref, cq_ref, ck_ref, o_ref, m_ref, l_ref, acc_ref, cqb_ref, *, tq):
    qi = pl.program_id(2)
    nch = tq // LANES
    m_ref[...] = jnp.full_like(m_ref, -jnp.inf)
    l_ref[...] = jnp.zeros_like(l_ref)
    acc_ref[...] = jnp.zeros_like(acc_ref)
    cqb_ref[...] = jnp.broadcast_to(cq_ref[0, 0], cqb_ref.shape)

    def step(off, masked):
        kb = k_ref[0, pl.ds(off, tq), :]
        vb = v_ref[0, pl.ds(off, tq), :]
        ck = ck_ref[0, 0, :, pl.ds(off, tq)]
        s = lax.dot_general(q_ref[0], kb, NT_DIMS, preferred_element_type=F32)
        u = []
        for c in range(nch):
            uc = s[:, c * LANES:(c + 1) * LANES] - ck[:, c * LANES:(c + 1) * LANES]
            if masked:
                row = lax.broadcasted_iota(jnp.int32, (tq, LANES), 0)
                col = lax.broadcasted_iota(jnp.int32, (tq, LANES), 1) + c * LANES
                uc = jnp.where(col <= row, uc, -jnp.inf)
            u.append(uc)
        bm = functools.reduce(jnp.maximum, u)
        cqb = cqb_ref[...]
        m_old = m_ref[...]
        m_new = jnp.maximum(m_old, jnp.max(bm, axis=1, keepdims=True) + cqb)
        a = jnp.exp(m_old - m_new)
        r = cqb - m_new
        p = [jnp.exp(uc + r) for uc in u]
        l_ref[...] = a * l_ref[...] + functools.reduce(jnp.add, p)
        pb = jnp.concatenate([pc.astype(BF16) for pc in p], axis=1)
        acc_ref[...] = a * acc_ref[...] + jnp.dot(pb, vb, preferred_element_type=F32)
        m_ref[...] = m_new

    def body(ki, carry):
        step(pl.multiple_of(ki * tq, tq), False)
        return carry

    lax.fori_loop(0, qi, body, 0)
    step(pl.multiple_of(qi * tq, tq), True)
    l = jnp.sum(l_ref[...], axis=1, keepdims=True)
    o_ref[0] = (acc_ref[...] / l).astype(o_ref.dtype)


def _fox_attention(qkv, cq, ck, nheads, dh):
    nb, s, _ = qkv.shape
    d = nheads * dh
    assert dh == LANES, dh
    tq = _tile(s, 1024, LANES)
    stat = pltpu.VMEM((tq, LANES), F32)
    return pl.pallas_call(
        functools.partial(_attn_kernel, tq=tq),
        grid=(nb, nheads, s // tq),
        in_specs=[
            pl.BlockSpec((1, tq, dh), lambda b, h, i: (b, i, h)),
            pl.BlockSpec((1, s, dh), lambda b, h, i: (b, 0, nheads + h)),
            pl.BlockSpec((1, s, dh), lambda b, h, i: (b, 0, 2 * nheads + h)),
            pl.BlockSpec((1, 1, tq, 1), lambda b, h, i: (b, h, i, 0)),
            pl.BlockSpec((1, 1, 1, s), lambda b, h, i: (b, h, 0, 0)),
        ],
        out_specs=pl.BlockSpec((1, tq, dh), lambda b, h, i: (b, i, h)),
        out_shape=jax.ShapeDtypeStruct((nb, s, d), BF16),
        scratch_shapes=[stat, stat, pltpu.VMEM((tq, dh), F32), stat],
        compiler_params=_params("parallel", "parallel", "parallel"),
        name="fox_attention",
    )(qkv, qkv, qkv, cq, ck)


def _out_ln_kernel(o_ref, w_ref, x_ref, mod_ref, g_ref, b_ref, out_ref, *, alpha, tn, nj, gate_row):
    j = pl.program_id(1)
    y = jnp.dot(o_ref[...], w_ref[...], preferred_element_type=F32)
    out_ref[:, pl.ds(pl.multiple_of(j * tn, tn), tn)] = y

    @pl.when(j == nj - 1)
    def _():
        gate = mod_ref[0][gate_row:gate_row + 1]
        r = alpha * x_ref[...] + (1.0 + gate) * out_ref[...]
        out_ref[...] = _layer_norm(r, g_ref[...], b_ref[...])


def _fox_out_layer(o2, w_o, layer, x2, mod, ln_g, ln_b, seq, alpha):
    t, d = x2.shape
    tm = _tile(seq, 512, PACKED_ROWS)
    tn = _tile(d, 1024, LANES)
    bps = seq // tm
    nj = d // tn
    return pl.pallas_call(
        functools.partial(_out_ln_kernel, alpha=alpha, tn=tn, nj=nj, gate_row=2),
        grid=(t // tm, nj),
        in_specs=[
            pl.BlockSpec((tm, d), lambda i, j: (i, 0)),
            pl.BlockSpec((None, d, tn), lambda i, j: (layer, 0, j)),
            pl.BlockSpec((tm, d), lambda i, j: (i, 0), **ONE_BUFFER),
            pl.BlockSpec((1, N_MOD, d), lambda i, j: (i // bps, 0, 0)),
            pl.BlockSpec((1, d), lambda i, j: (0, 0)),
            pl.BlockSpec((1, d), lambda i, j: (0, 0)),
        ],
        out_specs=pl.BlockSpec((tm, d), lambda i, j: (i, 0), **ONE_BUFFER),
        out_shape=jax.ShapeDtypeStruct((t, d), F32),
        compiler_params=_params("parallel", "arbitrary"),
        name="fox_out_layer",
    )(o2, w_o, x2, mod, ln_g.reshape(1, d), ln_b.reshape(1, d))


def _top_ranked(s, k, exact):
    rows = s.shape[0]
    row = lax.broadcasted_iota(jnp.int32, s.shape, 0)
    rank = jnp.full(s.shape, float(k), F32)
    vals = []
    for r in range(k):
        m = jnp.max(s, axis=0, keepdims=True)
        sel = s == m
        if exact:
            first = jnp.min(jnp.where(sel, row, rows), axis=0, keepdims=True)
            sel = row == first
        rank = jnp.where(sel, float(r), rank)
        vals.append(m)
        s = jnp.where(sel, -jnp.inf, s)
    return jnp.concatenate(vals, axis=0), rank


def _overfull(ranks, topk):
    fk = float(topk)
    counts = [jnp.sum((rk < fk).astype(F32), axis=0, keepdims=True) for rk in ranks]
    return jnp.max(functools.reduce(jnp.maximum, [jnp.abs(c - fk) for c in counts]))


def _key_stage(q, keys_ref, topk, exact):
    dhalf = q.shape[1] // 2
    out = []
    for p in range(2):
        qp = q[:, p * dhalf:(p + 1) * dhalf]
        s = lax.dot_general(keys_ref[p], qp, NT_DIMS, preferred_element_type=F32)
        out.append((s,) + _top_ranked(s, topk, exact))
    return out


def _pair_stage(s1, tv1, rank1, tv2, n1_ref, e1_ref, hd, topk, exact):
    half = topk // 2
    sub = lax.broadcasted_iota(jnp.int32, (8, 1), 0)
    pieces = [tv1[0:1] + tv2]
    for a in range(1, half):
        pieces.append(jnp.where(sub < topk // (a + 1), tv1[a:a + 1] + tv2[0:8], -jnp.inf))
    pieces.append(tv1[half:topk] + tv2[0:1])
    cand = jnp.concatenate(pieces, axis=0)
    cvals, crank = _top_ranked(cand, topk, exact)
    taken = crank < float(topk)
    z = jnp.sum(jnp.where(taken, jnp.exp(cand - cvals[0:1]), 0.0), axis=0, keepdims=True)
    takenf = taken.astype(F32)

    n1 = jnp.zeros_like(rank1)
    start = 0
    for a in range(half):
        size = topk if a == 0 else 8
        n_a = jnp.sum(takenf[start:start + size], axis=0, keepdims=True)
        n1 = jnp.where(rank1 == float(a), n_a, n1)
        start += size
    for a in range(half, topk):
        n1 = jnp.where(rank1 == float(a), takenf[start + a - half:start + a - half + 1], n1)

    n1_ref[:, pl.ds(hd, 1), :] = n1[:, None, :]
    e1_ref[:, pl.ds(hd, 1), :] = (SQRT_HALF * jnp.exp(s1 - tv1[0:1]) / z)[:, None, :]
    return crank


def _router_kernel(x_ref, mod_ref, wq_ref, keys_ref, hbf_ref, n1_ref, e1_ref, r2_ref, e2_ref, *, topk):
    hd = pl.program_id(1)

    @pl.when(hd == 0)
    def _():
        m = mod_ref[0]
        hbf_ref[...] = (x_ref[...] * (1.0 + m[4:5]) + m[3:4]).astype(BF16)

    kh = hbf_ref.shape[1] // 2
    q = (jnp.dot(hbf_ref[:, :kh], wq_ref[:kh, :], preferred_element_type=F32)
         + jnp.dot(hbf_ref[:, kh:], wq_ref[kh:, :], preferred_element_type=F32)).astype(BF16)

    def write_second_half(s2, tv2, rank2):
        r2_ref[0] = rank2
        e2_ref[0] = jnp.exp(s2 - tv2[0:1])

    (s1, tv1, rank1), (s2, tv2, rank2) = _key_stage(q, keys_ref, topk, exact=False)
    key_tie = _overfull([rank1, rank2], topk)
    crank = _pair_stage(s1, tv1, rank1, tv2, n1_ref, e1_ref, hd, topk, exact=False)
    pair_tie = _overfull([crank], topk)
    write_second_half(s2, tv2, rank2)

    @pl.when(key_tie > 0.5)
    def _():
        (s1x, tv1x, rank1x), (s2x, tv2x, rank2x) = _key_stage(q, keys_ref, topk, exact=True)
        _pair_stage(s1x, tv1x, rank1x, tv2x, n1_ref, e1_ref, hd, topk, exact=True)
        write_second_half(s2x, tv2x, rank2x)

    @pl.when(jnp.logical_and(key_tie <= 0.5, pair_tie > 0.5))
    def _():
        _pair_stage(s1, tv1, rank1, tv2, n1_ref, e1_ref, hd, topk, exact=True)


def _peer_router(x2, mod, w_q, keys, layer, seq):
    t, d = x2.shape
    _, _, nkeys, dhalf = keys.shape
    ph = w_q.shape[2] // (2 * dhalf)
    tr = _tile(seq, 512, LANES)
    bps = seq // tr
    route = jax.ShapeDtypeStruct((ph, nkeys, t), F32)
    route_spec = pl.BlockSpec((1, nkeys, tr), lambda i, h: (h, 0, i))
    key_major = jax.ShapeDtypeStruct((nkeys, ph, t), F32)
    key_spec = pl.BlockSpec((nkeys, ph, tr), lambda i, h: (0, 0, i))
    return pl.pallas_call(
        functools.partial(_router_kernel, topk=PEER_TOPK),
        grid=(t // tr, ph),
        in_specs=[
            pl.BlockSpec((tr, d), lambda i, h: (i, 0)),
            pl.BlockSpec((1, N_MOD, d), lambda i, h: (i // bps, 0, 0)),
            pl.BlockSpec((None, d, 2 * dhalf), lambda i, h: (layer, 0, h)),
            pl.BlockSpec((None, 2, nkeys, dhalf), lambda i, h: (layer, 0, 0, 0)),
        ],
        out_specs=[pl.BlockSpec((tr, d), lambda i, h: (i, 0)), key_spec, key_spec, route_spec, route_spec],
        out_shape=[jax.ShapeDtypeStruct((t, d), BF16), key_major, key_major, route, route],
        compiler_params=_params("parallel", "arbitrary"),
        name="peer_router",
    )(x2, mod, w_q, keys)


def _gate_block(a_scr, wt_scr, n1_ref, e1_ref, r2_ref, e2_ref, key0, nkeys, ph):
    ts, tm = a_scr.shape
    for il in range(ts // nkeys):
        i_key = key0 + il
        for lc in range(tm // LANES):
            cols = slice(lc * LANES, (lc + 1) * LANES)
            n1t = n1_ref[i_key, :, cols]
            e1t = e1_ref[i_key, :, cols]
            n1b = [n1t[hd:hd + 1, :] for hd in range(ph)]
            e1b = [e1t[hd:hd + 1, :] for hd in range(ph)]
            tiles = []
            for rc in range(nkeys // SUBLANES):
                rows = slice(rc * SUBLANES, (rc + 1) * SUBLANES)
                w = None
                for hd in range(ph):
                    t = jnp.where(r2_ref[hd, rows, cols] < n1b[hd], e2_ref[hd, rows, cols], 0.0) * e1b[hd]
                    w = t if w is None else w + t
                b = a_scr[il * nkeys + rc * SUBLANES:il * nkeys + (rc + 1) * SUBLANES, cols]
                tiles.append(w * (b * (1.0 + lax.erf(b))))
            blk = jnp.concatenate(tiles, axis=0).astype(BF16)
            wt_scr[cols, il * nkeys:(il + 1) * nkeys] = blk.T


def _expert_kernel(h_ref, u_ref, v_ref, n1_ref, e1_ref, r2_ref, e2_ref, x_ref, mod_ref, g_ref, b_ref, o_ref,
                   a_e, a_o, wt_e, wt_o, *, alpha, nkeys, ph, nj, ts):
    j = pl.program_id(1)
    tm, d = o_ref.shape
    gps = ts // nkeys

    @pl.when(j == 0)
    def _():
        o_ref[...] = jnp.zeros_like(o_ref)
        wt_e[...] = jnp.zeros_like(wt_e)
        a_o[...] = jnp.zeros_like(a_o)

    def mm1(rows, a_scr):
        half = tm // 2
        for s in range(2):
            a_scr[:, s * half:(s + 1) * half] = lax.dot_general(
                u_ref[rows, :], h_ref[s * half:(s + 1) * half, :], NT_DIMS, preferred_element_type=F32)

    def mm2(wt_scr, rows):
        half = d // 2
        for s in range(2):
            o_ref[:, s * half:(s + 1) * half] += jnp.dot(
                wt_scr[...], v_ref[rows, s * half:(s + 1) * half], preferred_element_type=F32)

    lo, hi = slice(0, ts), slice(ts, 2 * ts)
    last = 2 * nj - 1
    mm2(wt_e, lo)
    _gate_block(a_o, wt_o, n1_ref, e1_ref, r2_ref, e2_ref, jnp.maximum(2 * j - 1, 0) * gps, nkeys, ph)
    mm1(lo, a_e)
    mm2(wt_o, hi)
    _gate_block(a_e, wt_e, n1_ref, e1_ref, r2_ref, e2_ref, jnp.minimum(2 * j, last) * gps, nkeys, ph)
    mm1(hi, a_o)

    @pl.when(j == nj)
    def _():
        g_c = mod_ref[0][5:6]
        r = alpha * x_ref[...] + (1.0 + g_c) * o_ref[...]
        o_ref[...] = _layer_norm(r, g_ref[...], b_ref[...])


def _peer_experts(hbf, u, v, layer, n1t, e1t, r2, e2, x2, mod, ln_g, ln_b, seq, alpha):
    t, d = x2.shape
    ne = u.shape[1]
    ph, nkeys, _ = r2.shape
    tm = _tile(seq, 512, 2 * LANES)
    ts = _tile(ne // 2, 256, nkeys)
    bps = seq // tm
    nj = ne // (2 * ts)
    route_spec = pl.BlockSpec((ph, nkeys, tm), lambda i, j: (0, 0, i), **ONE_BUFFER)
    key_spec = pl.BlockSpec((nkeys, ph, tm), lambda i, j: (0, 0, i), **ONE_BUFFER)
    return pl.pallas_call(
        functools.partial(_expert_kernel, alpha=alpha, nkeys=nkeys, ph=ph, nj=nj, ts=ts),
        grid=(t // tm, nj + 1),
        in_specs=[
            pl.BlockSpec((tm, d), lambda i, j: (i, 0), **ONE_BUFFER),
            pl.BlockSpec((None, 2 * ts, d), lambda i, j: (layer, jnp.minimum(j, nj - 1), 0)),
            pl.BlockSpec((None, 2 * ts, d), lambda i, j: (layer, jnp.maximum(j - 1, 0), 0)),
            key_spec, key_spec, route_spec, route_spec,
            pl.BlockSpec((tm, d), lambda i, j: (i, 0), **ONE_BUFFER),
            pl.BlockSpec((1, N_MOD, d), lambda i, j: (i // bps, 0, 0)),
            pl.BlockSpec((1, d), lambda i, j: (0, 0)),
            pl.BlockSpec((1, d), lambda i, j: (0, 0)),
        ],
        out_specs=pl.BlockSpec((tm, d), lambda i, j: (i, 0), **ONE_BUFFER),
        out_shape=jax.ShapeDtypeStruct((t, d), F32),
        scratch_shapes=[pltpu.VMEM((ts, tm), F32), pltpu.VMEM((ts, tm), F32),
                        pltpu.VMEM((tm, ts), BF16), pltpu.VMEM((tm, ts), BF16)],
        compiler_params=_params("parallel", "arbitrary"),
        name="peer_experts",
    )(hbf, u, v, n1t, e1t, r2, e2, x2, mod, ln_g.reshape(1, d), ln_b.reshape(1, d))


def kernel(x, c, w_c, b_c, ada_table, ln_tok_g, ln_tok_b, ln_ch_g, ln_ch_b, pool_w, pool_scale,
           fox_w_in, fox_b_f, fox_w_o, peer_w_q, peer_keys, peer_u, peer_v):
    nb, seq, d = x.shape
    depth = ada_table.shape[0]
    alpha = float((2 * depth) ** 0.25)
    nheads = fox_b_f.shape[1]
    dh = d // nheads
    hpad = -(-nheads // LANES) * LANES

    mod_all = _adaln(c, w_c, b_c, ada_table)

    pool_w16 = pool_w.astype(BF16)
    w_in_t = jnp.swapaxes(fox_w_in, 1, 2)
    w_qkv16 = _cast_rows(w_in_t, 3 * d)
    b_f = jnp.pad(fox_b_f, ((0, 0), (0, hpad - nheads)))
    w_o16 = fox_w_o.astype(BF16)
    w_q16 = peer_w_q.astype(BF16)
    keys16 = peer_keys.astype(BF16)
    u16 = (peer_u * SQRT_HALF).astype(BF16)
    v16 = peer_v.astype(BF16)

    n_mixers = 2
    for i in range(depth):
        mod = mod_all[:, i].reshape(nb, N_MOD, d)
        jl = i // n_mixers
        if i % n_mixers == 0:
            x = _pool_layer(x, mod, pool_w16, jl, pool_scale[jl], ln_tok_g[i], ln_tok_b[i], alpha)
        else:
            x2 = x.reshape(nb * seq, d)
            qkv, f = _fox_proj(x2, mod, w_qkv16, w_in_t, jl, nheads, seq, float(dh) ** -0.5)
            cum = _decay_cumsum(f.reshape(nb, seq, hpad), b_f[jl].reshape(1, hpad))[:, :, :nheads]
            cum_t = cum.transpose(0, 2, 1)
            o = _fox_attention(qkv.reshape(nb, seq, 3 * d), cum_t[:, :, :, None], cum_t[:, :, None, :], nheads, dh)
            x = _fox_out_layer(o.reshape(nb * seq, d), w_o16, jl, x2, mod, ln_tok_g[i], ln_tok_b[i], seq, alpha)
            x = x.reshape(nb, seq, d)
        x2 = x.reshape(nb * seq, d)
        hbf, n1t, e1t, r2, e2 = _peer_router(x2, mod, w_q16, keys16, i, seq)
        x = _peer_experts(hbf, u16, v16, i, n1t, e1t, r2, e2, x2, mod, ln_ch_g[i], ln_ch_b[i], seq, alpha)
        x = x.reshape(nb, seq, d)
    return x
```

```python
import functools

import jax
import jax.numpy as jnp
from jax import lax
from jax.experimental import pallas as pl
from jax.experimental.pallas import tpu as pltpu

F32 = jnp.float32
BF16 = jnp.bfloat16

LN_EPS = 1e-5
N_MOD = 6
POOL_WINDOWS = (2, 4, 8, 16)
POOL_HALO = 16
PEER_TOPK = 16
SQRT_HALF = 0.7071067811865476
LANES = 128
SUBLANES = 8
PACKED_ROWS = 16
LN_ROWS = 32
VMEM_LIMIT_BYTES = 56 * 1024 * 1024


NT_DIMS = (((1,), (1,)), ((), ()))
ONE_BUFFER = dict(pipeline_mode=pl.Buffered(1))


def _params(*semantics):
    return pltpu.CompilerParams(dimension_semantics=semantics, vmem_limit_bytes=VMEM_LIMIT_BYTES)


def _tile(dim, target, align):
    if dim <= target:
        return dim
    t = (target // align) * align
    while t > align and dim % t:
        t -= align
    assert dim % t == 0, (dim, target, align)
    return t


def _layer_norm(r, g, b):
    mu = jnp.mean(r, axis=-1, keepdims=True)
    xc = r - mu
    var = jnp.mean(xc * xc, axis=-1, keepdims=True)
    return xc * lax.rsqrt(var + LN_EPS) * g + b


def _residual_layer_norm_rows(x_ref, y_ref, gate, g, b, alpha):
    scale = 1.0 + gate

    def body(i, carry):
        rows = pl.ds(pl.multiple_of(i * LN_ROWS, LN_ROWS), LN_ROWS)
        y_ref[rows, :] = _layer_norm(alpha * x_ref[rows, :] + scale * y_ref[rows, :], g, b)
        return carry

    lax.fori_loop(0, y_ref.shape[0] // LN_ROWS, body, 0)


def _adaln_kernel(ct_ref, w_ref, b_ref, ada_ref, o_ref, acc_ref, *, nb, nk):
    k = pl.program_id(1)

    @pl.when(k == 0)
    def _():
        acc_ref[...] = jnp.zeros_like(acc_ref)

    s = jax.nn.silu(ct_ref[...])
    w = w_ref[...]
    tk, tn = w.shape
    for b in range(nb):
        prod = w * s[:, b:b + 1]
        acc_ref[b] += jnp.sum(prod.reshape(tk // SUBLANES, SUBLANES, tn), axis=0)

    @pl.when(k == nk - 1)
    def _():
        for b in range(nb):
            t0 = jnp.sum(acc_ref[b], axis=0, keepdims=True) + b_ref[...]
            o_ref[b] = t0 + ada_ref[...]


def _adaln(c, w_c, b_c, ada_table):
    nb, d = c.shape
    n = w_c.shape[1]
    depth = ada_table.shape[0]
    tk = _tile(d, 512, SUBLANES)
    tn = _tile(n, 2048, LANES)
    nk = d // tk
    return pl.pallas_call(
        functools.partial(_adaln_kernel, nb=nb, nk=nk),
        grid=(n // tn, nk),
        in_specs=[
            pl.BlockSpec((tk, nb), lambda j, k: (k, 0)),
            pl.BlockSpec((tk, tn), lambda j, k: (k, j)),
            pl.BlockSpec((1, tn), lambda j, k: (0, j)),
            pl.BlockSpec((depth, tn), lambda j, k: (0, j)),
        ],
        out_specs=pl.BlockSpec((nb, depth, tn), lambda j, k: (0, 0, j)),
        out_shape=jax.ShapeDtypeStruct((nb, depth, n), F32),
        scratch_shapes=[pltpu.VMEM((nb, SUBLANES, tn), F32)],
        compiler_params=_params("parallel", "arbitrary"),
        name="adaln",
    )(c.T, w_c, b_c.reshape(1, n), ada_table)


def _pool_kernel(x_ref, xp_ref, mod_ref, w_ref, scale_ref, g_ref, b_ref, o_ref, hh_ref, *, alpha, ts, dg):
    s_idx = pl.program_id(1)
    m = mod_ref[0]
    sh, sc, gate = m[0:1], m[1:2], m[2:3]
    x = x_ref[0]
    h = x * (1.0 + sc) + sh
    hp = xp_ref[0] * (1.0 + sc) + sh
    hh_ref[0:POOL_HALO] = jnp.where(s_idx > 0, hp, 0.0)
    hh_ref[POOL_HALO:] = h
    pos = lax.broadcasted_iota(jnp.int32, (ts, 1), 0) + s_idx * ts
    ys = []
    for gi, win in enumerate(POOL_WINDOWS):
        cols = slice(gi * dg, (gi + 1) * dg)
        acc = hh_ref[pl.ds(POOL_HALO, ts), cols]
        for k in range(1, win):
            acc = acc + hh_ref[pl.ds(POOL_HALO - k, ts), cols]
        cnt = jnp.minimum(pos + 1, win).astype(F32)
        z = acc / cnt - h[:, cols]
        ys.append(jnp.dot(z.astype(BF16), w_ref[gi], preferred_element_type=F32))
    y = jnp.concatenate(ys, axis=1) * scale_ref[...]
    r = alpha * x + (1.0 + gate) * y
    o_ref[0] = _layer_norm(r, g_ref[...], b_ref[...])


def _pool_layer(x, mod, w, layer, scale, ln_g, ln_b, alpha):
    nb, s, d = x.shape
    _, g, dg, _ = w.shape
    ts = _tile(s, 256, POOL_HALO)
    hb = ts // POOL_HALO
    return pl.pallas_call(
        functools.partial(_pool_kernel, alpha=alpha, ts=ts, dg=dg),
        grid=(nb, s // ts),
        in_specs=[
            pl.BlockSpec((1, ts, d), lambda b, i: (b, i, 0)),
            pl.BlockSpec((1, POOL_HALO, d), lambda b, i: (b, jnp.maximum(i * hb - 1, 0), 0)),
            pl.BlockSpec((1, N_MOD, d), lambda b, i: (b, 0, 0)),
            pl.BlockSpec((None, g, dg, dg), lambda b, i: (layer, 0, 0, 0)),
            pl.BlockSpec((1, d), lambda b, i: (0, 0)),
            pl.BlockSpec((1, d), lambda b, i: (0, 0)),
            pl.BlockSpec((1, d), lambda b, i: (0, 0)),
        ],
        out_specs=pl.BlockSpec((1, ts, d), lambda b, i: (b, i, 0)),
        out_shape=jax.ShapeDtypeStruct((nb, s, d), F32),
        scratch_shapes=[pltpu.VMEM((ts + POOL_HALO, d), F32)],
        compiler_params=_params("parallel", "parallel"),
        name="pool_layer",
    )(x, x, mod, w, scale.reshape(1, d), ln_g.reshape(1, d), ln_b.reshape(1, d))


def _cast_kernel(x_ref, o_ref):
    o_ref[...] = x_ref[...].astype(o_ref.dtype)


def _cast_rows(w, nrows):
    nl, _, cols = w.shape
    tk = _tile(nrows, 512, PACKED_ROWS)
    tn = _tile(cols, 2048, LANES)
    return pl.pallas_call(
        _cast_kernel,
        grid=(nl, nrows // tk, cols // tn),
        in_specs=[pl.BlockSpec((None, tk, tn), lambda l, i, j: (l, i, j))],
        out_specs=pl.BlockSpec((None, tk, tn), lambda l, i, j: (l, i, j)),
        out_shape=jax.ShapeDtypeStruct((nl, nrows, cols), BF16),
        compiler_params=_params("parallel", "parallel", "parallel"),
        name="cast_rows",
    )(w)


def _fox_proj_kernel(x_ref, mod_ref, w_ref, wf_ref, qkv_ref, f_ref, h_ref, *, nq_tiles, sm_scale, nheads):
    j = pl.program_id(1)

    @pl.when(j == 0)
    def _():
        m = mod_ref[0]
        h = x_ref[...] * (1.0 + m[1:2]) + m[0:1]
        h_hi = h.astype(BF16)
        h_ref[...] = h_hi
        row = lax.broadcasted_iota(jnp.int32, wf_ref.shape, 0)
        wf = jnp.where(row < nheads, wf_ref[...], 0.0)
        wf_hi = wf.astype(BF16)
        wf_lo = (wf - wf_hi.astype(F32)).astype(BF16)
        h_lo = (h - h_hi.astype(F32)).astype(BF16)
        f_ref[...] = (lax.dot_general(h_hi, wf_hi, NT_DIMS, preferred_element_type=F32)
                      + lax.dot_general(h_lo, wf_hi, NT_DIMS, preferred_element_type=F32)
                      + lax.dot_general(h_hi, wf_lo, NT_DIMS, preferred_element_type=F32))

    y = lax.dot_general(h_ref[...], w_ref[...], NT_DIMS, preferred_element_type=F32)
    y = y * jnp.where(j < nq_tiles, sm_scale, 1.0)
    qkv_ref[...] = y.astype(BF16)


def _fox_proj(x2, mod, w_qkv_t, w_in_t, layer, nheads, seq, sm_scale):
    t, d = x2.shape
    n = w_qkv_t.shape[1]
    assert n % LANES == 0 and nheads <= LANES, (n, nheads)
    hp = LANES
    tm = _tile(seq, 512, PACKED_ROWS)
    tn = _tile(d, 1024, LANES)
    bps = seq // tm
    return pl.pallas_call(
        functools.partial(_fox_proj_kernel, nq_tiles=d // tn, sm_scale=sm_scale, nheads=nheads),
        grid=(t // tm, n // tn),
        in_specs=[
            pl.BlockSpec((tm, d), lambda i, j: (i, 0)),
            pl.BlockSpec((1, N_MOD, d), lambda i, j: (i // bps, 0, 0)),
            pl.BlockSpec((None, tn, d), lambda i, j: (layer, j, 0)),
            pl.BlockSpec((None, hp, d), lambda i, j: (layer, n // hp, 0)),
        ],
        out_specs=[
            pl.BlockSpec((tm, tn), lambda i, j: (i, j)),
            pl.BlockSpec((tm, hp), lambda i, j: (i, 0)),
        ],
        out_shape=[jax.ShapeDtypeStruct((t, n), BF16), jax.ShapeDtypeStruct((t, hp), F32)],
        scratch_shapes=[pltpu.VMEM((tm, d), BF16)],
        compiler_params=_params("parallel", "arbitrary"),
        name="fox_proj",
    )(x2, mod, w_qkv_t, w_in_t)


def _split3(v):
    p1 = v.astype(BF16)
    r1 = v - p1.astype(F32)
    p2 = r1.astype(BF16)
    p3 = (r1 - p2.astype(F32)).astype(BF16)
    return p1, p2, p3


def _decay_kernel(f_ref, bf_ref, o_ref, *, chunk):
    s = f_ref.shape[1]
    row = lax.broadcasted_iota(jnp.int32, (chunk, chunk), 0)
    col = lax.broadcasted_iota(jnp.int32, (chunk, chunk), 1)
    tri = (col <= row).astype(BF16)
    carry = jnp.zeros((1, f_ref.shape[2]), F32)
    for c in range(s // chunk):
        rows = slice(c * chunk, (c + 1) * chunk)
        lf = jax.nn.log_sigmoid(f_ref[0, rows, :] + bf_ref[...])
        p1, p2, p3 = _split3(lf)
        cs = (jnp.dot(tri, p1, preferred_element_type=F32)
              + jnp.dot(tri, p2, preferred_element_type=F32)
              + jnp.dot(tri, p3, preferred_element_type=F32)) + carry
        o_ref[0, rows, :] = cs
        carry = cs[chunk - 1:chunk, :]


def _decay_cumsum(f, b_f):
    nb, s, hp = f.shape
    chunk = _tile(s, 256, PACKED_ROWS)
    return pl.pallas_call(
        functools.partial(_decay_kernel, chunk=chunk),
        grid=(nb,),
        in_specs=[pl.BlockSpec((1, s, hp), lambda b: (b, 0, 0)), pl.BlockSpec((1, hp), lambda b: (0, 0))],
        out_specs=pl.BlockSpec((1, s, hp), lambda b: (b, 0, 0)),
        out_shape=jax.ShapeDtypeStruct((nb, s, hp), F32),
        compiler_params=_params("parallel"),
        name="fox_decay",
    )(f, b_f)


def _attn_kernel(q_ref, k_ref, v_ref, cq_ref, ck_ref, o_ref, m_ref, l_ref, acc_ref, cqb_ref, *, tq):
    qi = pl.program_id(2)
    nch = tq // LANES
    m_ref[...] = jnp.full_like(m_ref, -jnp.inf)
    l_ref[...] = jnp.zeros_like(l_ref)
    acc_ref[...] = jnp.zeros_like(acc_ref)
    cqb_ref[...] = jnp.broadcast_to(cq_ref[0, 0], cqb_ref.shape)

    def step(off, masked):
        kb = k_ref[0, pl.ds(off, tq), :]
        vb = v_ref[0, pl.ds(off, tq), :]
        ck = ck_ref[0, 0, :, pl.ds(off, tq)]
        s = lax.dot_general(q_ref[0], kb, NT_DIMS, preferred_element_type=F32)
        u = []
        for c in range(nch):
            uc = s[:, c * LANES:(c + 1) * LANES] - ck[:, c * LANES:(c + 1) * LANES]
            if masked:
                row = lax.broadcasted_iota(jnp.int32, (tq, LANES), 0)
                col = lax.broadcasted_iota(jnp.int32, (tq, LANES), 1) + c * LANES
                uc = jnp.where(col <= row, uc, -jnp.inf)
            u.append(uc)
        bm = functools.reduce(jnp.maximum, u)
        cqb = cqb_ref[...]
        m_old = m_ref[...]
        m_new = jnp.maximum(m_old, jnp.max(bm, axis=1, keepdims=True) + cqb)
        a = jnp.exp(m_old - m_new)
        r = cqb - m_new
        p = [jnp.exp(uc + r) for uc in u]
        l_ref[...] = a * l_ref[...] + functools.reduce(jnp.add, p)
        pb = jnp.concatenate([pc.astype(BF16) for pc in p], axis=1)
        acc_ref[...] = a * acc_ref[...] + jnp.dot(pb, vb, preferred_element_type=F32)
        m_ref[...] = m_new

    def body(ki, carry):
        step(pl.multiple_of(ki * tq, tq), False)
        return carry

    lax.fori_loop(0, qi, body, 0)
    step(pl.multiple_of(qi * tq, tq), True)
    l = jnp.sum(l_ref[...], axis=1, keepdims=True)
    o_ref[0] = (acc_ref[...] / l).astype(o_ref.dtype)


def _fox_attention(qkv, cq, ck, nheads, dh):
    nb, s, _ = qkv.shape
    d = nheads * dh
    assert dh == LANES, dh
    tq = _tile(s, 1024, LANES)
    stat = pltpu.VMEM((tq, LANES), F32)
    return pl.pallas_call(
        functools.partial(_attn_kernel, tq=tq),
        grid=(nb, nheads, s // tq),
        in_specs=[
            pl.BlockSpec((1, tq, dh), lambda b, h, i: (b, i, h)),
            pl.BlockSpec((1, s, dh), lambda b, h, i: (b, 0, nheads + h)),
            pl.BlockSpec((1, s, dh), lambda b, h, i: (b, 0, 2 * nheads + h)),
            pl.BlockSpec((1, 1, tq, 1), lambda b, h, i: (b, h, i, 0)),
            pl.BlockSpec((1, 1, 1, s), lambda b, h, i: (b, h, 0, 0)),
        ],
        out_specs=pl.BlockSpec((1, tq, dh), lambda b, h, i: (b, i, h)),
        out_shape=jax.ShapeDtypeStruct((nb, s, d), BF16),
        scratch_shapes=[stat, stat, pltpu.VMEM((tq, dh), F32), stat],
        compiler_params=_params("parallel", "parallel", "parallel"),
        name="fox_attention",
    )(qkv, qkv, qkv, cq, ck)


def _out_ln_kernel(o_ref, w_ref, x_ref, mod_ref, g_ref, b_ref, out_ref, *, alpha, tn, nj, gate_row):
    j = pl.program_id(1)
    y = jnp.dot(o_ref[...], w_ref[...], preferred_element_type=F32)
    out_ref[:, pl.ds(pl.multiple_of(j * tn, tn), tn)] = y

    @pl.when(j == nj - 1)
    def _():
        gate = mod_ref[0][gate_row:gate_row + 1]
        r = alpha * x_ref[...] + (1.0 + gate) * out_ref[...]
        out_ref[...] = _layer_norm(r, g_ref[...], b_ref[...])


def _fox_out_layer(o2, w_o, layer, x2, mod, ln_g, ln_b, seq, alpha):
    t, d = x2.shape
    tm = _tile(seq, 512, PACKED_ROWS)
    tn = _tile(d, 1024, LANES)
    bps = seq // tm
    nj = d // tn
    return pl.pallas_call(
        functools.partial(_out_ln_kernel, alpha=alpha, tn=tn, nj=nj, gate_row=2),
        grid=(t // tm, nj),
        in_specs=[
            pl.BlockSpec((tm, d), lambda i, j: (i, 0)),
            pl.BlockSpec((None, d, tn), lambda i, j: (layer, 0, j)),
            pl.BlockSpec((tm, d), lambda i, j: (i, 0), **ONE_BUFFER),
            pl.BlockSpec((1, N_MOD, d), lambda i, j: (i // bps, 0, 0)),
            pl.BlockSpec((1, d), lambda i, j: (0, 0)),
            pl.BlockSpec((1, d), lambda i, j: (0, 0)),
        ],
        out_specs=pl.BlockSpec((tm, d), lambda i, j: (i, 0), **ONE_BUFFER),
        out_shape=jax.ShapeDtypeStruct((t, d), F32),
        compiler_params=_params("parallel", "arbitrary"),
        name="fox_out_layer",
    )(o2, w_o, x2, mod, ln_g.reshape(1, d), ln_b.reshape(1, d))


def _top_ranked(s, k, exact):
    rows = s.shape[0]
    row = lax.broadcasted_iota(jnp.int32, s.shape, 0)
    rank = jnp.full(s.shape, float(k), F32)
    vals = []
    for r in range(k):
        m = jnp.max(s, axis=0, keepdims=True)
        sel = s == m
        if exact:
            first = jnp.min(jnp.where(sel, row, rows), axis=0, keepdims=True)
            sel = row == first
        rank = jnp.where(sel, float(r), rank)
        vals.append(m)
        s = jnp.where(sel, -jnp.inf, s)
    return jnp.concatenate(vals, axis=0), rank


def _overfull(ranks, topk):
    fk = float(topk)
    counts = [jnp.sum((rk < fk).astype(F32), axis=0, keepdims=True) for rk in ranks]
    return jnp.max(functools.reduce(jnp.maximum, [jnp.abs(c - fk) for c in counts]))


def _key_stage(q, keys_ref, topk, exact):
    dhalf = q.shape[1] // 2
    out = []
    for p in range(2):
        qp = q[:, p * dhalf:(p + 1) * dhalf]
        s = lax.dot_general(keys_ref[p], qp, NT_DIMS, preferred_element_type=F32)
        out.append((s,) + _top_ranked(s, topk, exact))
    return out


def _pair_stage(s1, tv1, rank1, tv2, n1_ref, e1_ref, hd, topk, exact):
    half = topk // 2
    sub = lax.broadcasted_iota(jnp.int32, (8, 1), 0)
    pieces = [tv1[0:1] + tv2]
    for a in range(1, half):
        pieces.append(jnp.where(sub < topk // (a + 1), tv1[a:a + 1] + tv2[0:8], -jnp.inf))
    pieces.append(tv1[half:topk] + tv2[0:1])
    cand = jnp.concatenate(pieces, axis=0)
    cvals, crank = _top_ranked(cand, topk, exact)
    taken = crank < float(topk)
    z = jnp.sum(jnp.where(taken, jnp.exp(cand - cvals[0:1]), 0.0), axis=0, keepdims=True)
    takenf = taken.astype(F32)

    n1 = jnp.zeros_like(rank1)
    start = 0
    for a in range(half):
        size = topk if a == 0 else 8
        n_a = jnp.sum(takenf[start:start + size], axis=0, keepdims=True)
        n1 = jnp.where(rank1 == float(a), n_a, n1)
        start += size
    for a in range(half, topk):
        n1 = jnp.where(rank1 == float(a), takenf[start + a - half:start + a - half + 1], n1)

    n1_ref[:, pl.ds(hd, 1), :] = n1[:, None, :]
    e1_ref[:, pl.ds(hd, 1), :] = (SQRT_HALF * jnp.exp(s1 - tv1[0:1]) / z)[:, None, :]
    return crank


def _router_kernel(x_ref, mod_ref, wq_ref, keys_ref, hbf_ref, n1_ref, e1_ref, r2_ref, e2_ref, *, topk):
    hd = pl.program_id(1)

    @pl.when(hd == 0)
    def _():
        m = mod_ref[0]
        hbf_ref[...] = (x_ref[...] * (1.0 + m[4:5]) + m[3:4]).astype(BF16)

    kh = hbf_ref.shape[1] // 2
    q = (jnp.dot(hbf_ref[:, :kh], wq_ref[:kh, :], preferred_element_type=F32)
         + jnp.dot(hbf_ref[:, kh:], wq_ref[kh:, :], preferred_element_type=F32)).astype(BF16)

    def write_second_half(s2, tv2, rank2):
        r2_ref[0] = rank2
        e2_ref[0] = jnp.exp(s2 - tv2[0:1])

    (s1, tv1, rank1), (s2, tv2, rank2) = _key_stage(q, keys_ref, topk, exact=False)
    key_tie = _overfull([rank1, rank2], topk)
    crank = _pair_stage(s1, tv1, rank1, tv2, n1_ref, e1_ref, hd, topk, exact=False)
    pair_tie = _overfull([crank], topk)
    write_second_half(s2, tv2, rank2)

    @pl.when(key_tie > 0.5)
    def _():
        (s1x, tv1x, rank1x), (s2x, tv2x, rank2x) = _key_stage(q, keys_ref, topk, exact=True)
        _pair_stage(s1x, tv1x, rank1x, tv2x, n1_ref, e1_ref, hd, topk, exact=True)
        write_second_half(s2x, tv2x, rank2x)

    @pl.when(jnp.logical_and(key_tie <= 0.5, pair_tie > 0.5))
    def _():
        _pair_stage(s1, tv1, rank1, tv2, n1_ref, e1_ref, hd, topk, exact=True)


def _peer_router(x2, mod, w_q, keys, layer, seq):
    t, d = x2.shape
    _, _, nkeys, dhalf = keys.shape
    ph = w_q.shape[2] // (2 * dhalf)
    tr = _tile(seq, 512, LANES)
    bps = seq // tr
    route = jax.ShapeDtypeStruct((ph, nkeys, t), F32)
    route_spec = pl.BlockSpec((1, nkeys, tr), lambda i, h: (h, 0, i))
    key_major = jax.ShapeDtypeStruct((nkeys, ph, t), F32)
    key_spec = pl.BlockSpec((nkeys, ph, tr), lambda i, h: (0, 0, i))
    return pl.pallas_call(
        functools.partial(_router_kernel, topk=PEER_TOPK),
        grid=(t // tr, ph),
        in_specs=[
            pl.BlockSpec((tr, d), lambda i, h: (i, 0)),
            pl.BlockSpec((1, N_MOD, d), lambda i, h: (i // bps, 0, 0)),
            pl.BlockSpec((None, d, 2 * dhalf), lambda i, h: (layer, 0, h)),
            pl.BlockSpec((None, 2, nkeys, dhalf), lambda i, h: (layer, 0, 0, 0)),
        ],
        out_specs=[pl.BlockSpec((tr, d), lambda i, h: (i, 0)), key_spec, key_spec, route_spec, route_spec],
        out_shape=[jax.ShapeDtypeStruct((t, d), BF16), key_major, key_major, route, route],
        compiler_params=_params("parallel", "arbitrary"),
        name="peer_router",
    )(x2, mod, w_q, keys)


def _gate_block(a_scr, wt_scr, n1_ref, e1_ref, r2_ref, e2_ref, key0, nkeys, ph):
    ts, tm = a_scr.shape
    for il in range(ts // nkeys):
        i_key = key0 + il
        for lc in range(tm // LANES):
            cols = slice(lc * LANES, (lc + 1) * LANES)
            n1t = n1_ref[i_key, :, cols]
            e1t = e1_ref[i_key, :, cols]
            n1b = [n1t[hd:hd + 1, :] for hd in range(ph)]
            e1b = [e1t[hd:hd + 1, :] for hd in range(ph)]
            tiles = []
            for rc in range(nkeys // SUBLANES):
                rows = slice(rc * SUBLANES, (rc + 1) * SUBLANES)
                w = None
                for hd in range(ph):
                    t = jnp.where(r2_ref[hd, rows, cols] < n1b[hd], e2_ref[hd, rows, cols], 0.0) * e1b[hd]
                    w = t if w is None else w + t
                b = a_scr[il * nkeys + rc * SUBLANES:il * nkeys + (rc + 1) * SUBLANES, cols]
                tiles.append(w * (b * (1.0 + lax.erf(b))))
            blk = jnp.concatenate(tiles, axis=0).astype(BF16)
            wt_scr[cols, il * nkeys:(il + 1) * nkeys] = blk.T


def _expert_kernel(h_ref, u_ref, v_ref, n1_ref, e1_ref, r2_ref, e2_ref, x_ref, mod_ref, g_ref, b_ref, *rest,
                   alpha, nkeys, ph, nj, ts, cast_next):
    if cast_next:
        un_ref, vn_ref, o_ref, un16_ref, vn16_ref, a_e, a_o, wt_e, wt_o = rest
        un16_ref[...] = (un_ref[...] * SQRT_HALF).astype(BF16)
        vn16_ref[...] = vn_ref[...].astype(BF16)
    else:
        o_ref, a_e, a_o, wt_e, wt_o = rest
    j = pl.program_id(1)
    tm, d = o_ref.shape
    gps = ts // nkeys

    @pl.when(j == 0)
    def _():
        o_ref[...] = jnp.zeros_like(o_ref)
        wt_e[...] = jnp.zeros_like(wt_e)
        a_o[...] = jnp.zeros_like(a_o)

    def mm1(rows, a_scr):
        half = tm // 2
        for s in range(2):
            a_scr[:, s * half:(s + 1) * half] = lax.dot_general(
                u_ref[rows, :], h_ref[s * half:(s + 1) * half, :], NT_DIMS, preferred_element_type=F32)

    def mm2(wt_scr, rows):
        half = d // 2
        for s in range(2):
            o_ref[:, s * half:(s + 1) * half] += jnp.dot(
                wt_scr[...], v_ref[rows, s * half:(s + 1) * half], preferred_element_type=F32)

    lo, hi = slice(0, ts), slice(ts, 2 * ts)
    last = 2 * nj - 1
    mm2(wt_e, lo)
    _gate_block(a_o, wt_o, n1_ref, e1_ref, r2_ref, e2_ref, jnp.maximum(2 * j - 1, 0) * gps, nkeys, ph)
    mm1(lo, a_e)
    mm2(wt_o, hi)
    _gate_block(a_e, wt_e, n1_ref, e1_ref, r2_ref, e2_ref, jnp.minimum(2 * j, last) * gps, nkeys, ph)
    mm1(hi, a_o)

    @pl.when(j == nj)
    def _():
        _residual_layer_norm_rows(x_ref, o_ref, mod_ref[0][5:6], g_ref[...], b_ref[...], alpha)


def _peer_experts(hbf, u, v, n1t, e1t, r2, e2, x2, mod, ln_g, ln_b, seq, alpha, nxt=None):
    t, d = x2.shape
    ne = u.shape[0]
    ph, nkeys, _ = r2.shape
    tm = _tile(seq, 512, 2 * LANES)
    ts = _tile(ne // 2, 256, nkeys)
    bps = seq // tm
    nj = ne // (2 * ts)
    route_spec = pl.BlockSpec((ph, nkeys, tm), lambda i, j: (0, 0, i), **ONE_BUFFER)
    key_spec = pl.BlockSpec((nkeys, ph, tm), lambda i, j: (0, 0, i), **ONE_BUFFER)
    in_specs = [
        pl.BlockSpec((tm, d), lambda i, j: (i, 0)),
        pl.BlockSpec((2 * ts, d), lambda i, j: (jnp.minimum(j, nj - 1), 0)),
        pl.BlockSpec((2 * ts, d), lambda i, j: (jnp.maximum(j - 1, 0), 0)),
        key_spec, key_spec, route_spec, route_spec,
        pl.BlockSpec((tm, d), lambda i, j: (i, 0), **ONE_BUFFER),
        pl.BlockSpec((1, N_MOD, d), lambda i, j: (i // bps, 0, 0)),
        pl.BlockSpec((1, d), lambda i, j: (0, 0)),
        pl.BlockSpec((1, d), lambda i, j: (0, 0)),
    ]
    operands = [hbf, u, v, n1t, e1t, r2, e2, x2, mod, ln_g.reshape(1, d), ln_b.reshape(1, d)]
    out_specs = [pl.BlockSpec((tm, d), lambda i, j: (i, 0), **ONE_BUFFER)]
    out_shape = [jax.ShapeDtypeStruct((t, d), F32)]
    if nxt is not None:
        u_all, v_all, layer = nxt
        steps = (t // tm) * (nj + 1)
        per_step = -(-ne // steps)
        rows = -(-per_step // PACKED_ROWS) * PACKED_ROWS
        assert ne % rows == 0, (ne, rows)
        last = ne // rows - 1

        def chunk(i, j):
            return jnp.minimum(i * (nj + 1) + j, last)

        in_specs += [pl.BlockSpec((None, rows, d), lambda i, j: (layer, chunk(i, j), 0))] * 2
        operands += [u_all, v_all]
        out_specs += [pl.BlockSpec((rows, d), lambda i, j: (chunk(i, j), 0))] * 2
        out_shape += [jax.ShapeDtypeStruct((ne, d), BF16)] * 2
    return pl.pallas_call(
        functools.partial(_expert_kernel, alpha=alpha, nkeys=nkeys, ph=ph, nj=nj, ts=ts, cast_next=nxt is not None),
        grid=(t // tm, nj + 1),
        in_specs=in_specs,
        out_specs=out_specs,
        out_shape=out_shape,
        scratch_shapes=[pltpu.VMEM((ts, tm), F32), pltpu.VMEM((ts, tm), F32),
                        pltpu.VMEM((tm, ts), BF16), pltpu.VMEM((tm, ts), BF16)],
        compiler_params=_params("parallel", "arbitrary"),
        name="peer_experts",
    )(*operands)


def kernel(x, c, w_c, b_c, ada_table, ln_tok_g, ln_tok_b, ln_ch_g, ln_ch_b, pool_w, pool_scale,
           fox_w_in, fox_b_f, fox_w_o, peer_w_q, peer_keys, peer_u, peer_v):
    nb, seq, d = x.shape
    depth = ada_table.shape[0]
    alpha = float((2 * depth) ** 0.25)
    nheads = fox_b_f.shape[1]
    dh = d // nheads
    hpad = -(-nheads // LANES) * LANES

    mod_all = _adaln(c, w_c, b_c, ada_table)

    pool_w16 = pool_w.astype(BF16)
    w_in_t = jnp.swapaxes(fox_w_in, 1, 2)
    w_qkv16 = _cast_rows(w_in_t, 3 * d)
    b_f = jnp.pad(fox_b_f, ((0, 0), (0, hpad - nheads)))
    w_o16 = fox_w_o.astype(BF16)
    w_q16 = peer_w_q.astype(BF16)
    keys16 = peer_keys.astype(BF16)
    u16 = (peer_u[0] * SQRT_HALF).astype(BF16)
    v16 = peer_v[0].astype(BF16)

    n_mixers = 2
    for i in range(depth):
        mod = mod_all[:, i].reshape(nb, N_MOD, d)
        jl = i // n_mixers
        if i % n_mixers == 0:
            x = _pool_layer(x, mod, pool_w16, jl, pool_scale[jl], ln_tok_g[i], ln_tok_b[i], alpha)
        else:
            x2 = x.reshape(nb * seq, d)
            qkv, f = _fox_proj(x2, mod, w_qkv16, w_in_t, jl, nheads, seq, float(dh) ** -0.5)
            cum = _decay_cumsum(f.reshape(nb, seq, hpad), b_f[jl].reshape(1, hpad))[:, :, :nheads]
            cum_t = cum.transpose(0, 2, 1)
            o = _fox_attention(qkv.reshape(nb, seq, 3 * d), cum_t[:, :, :, None], cum_t[:, :, None, :], nheads, dh)
            x = _fox_out_layer(o.reshape(nb * seq, d), w_o16, jl, x2, mod, ln_tok_g[i], ln_tok_b[i], seq, alpha)
            x = x.reshape(nb, seq, d)
        x2 = x.reshape(nb * seq, d)
        hbf, n1t, e1t, r2, e2 = _peer_router(x2, mod, w_q16, keys16, i, seq)
        nxt = (peer_u, peer_v, i + 1) if i + 1 < depth else None
        x, *converted = _peer_experts(hbf, u16, v16, n1t, e1t, r2, e2, x2, mod, ln_ch_g[i], ln_ch_b[i], seq, alpha, nxt)
        if converted:
            u16, v16 = converted
        x = x.reshape(nb, seq, d)
    return x
```

```python
import functools

import jax
import jax.numpy as jnp
from jax import lax
from jax.experimental import pallas as pl
from jax.experimental.pallas import tpu as pltpu

F32 = jnp.float32
BF16 = jnp.bfloat16

LN_EPS = 1e-5
N_MOD = 6
POOL_WINDOWS = (2, 4, 8, 16)
POOL_HALO = 16
PEER_TOPK = 16
SQRT_HALF = 0.7071067811865476
LANES = 128
SUBLANES = 8
PACKED_ROWS = 16
LN_ROWS = 32
VMEM_LIMIT_BYTES = 56 * 1024 * 1024


NT_DIMS = (((1,), (1,)), ((), ()))
ONE_BUFFER = dict(pipeline_mode=pl.Buffered(1))


def _params(*semantics):
    return pltpu.CompilerParams(dimension_semantics=semantics, vmem_limit_bytes=VMEM_LIMIT_BYTES)


def _tile(dim, target, align):
    if dim <= target:
        return dim
    t = (target // align) * align
    while t > align and dim % t:
        t -= align
    assert dim % t == 0, (dim, target, align)
    return t


def _layer_norm(r, g, b):
    mu = jnp.mean(r, axis=-1, keepdims=True)
    xc = r - mu
    var = jnp.mean(xc * xc, axis=-1, keepdims=True)
    return xc * lax.rsqrt(var + LN_EPS) * g + b


def _residual_layer_norm_rows(x_ref, y_ref, gate, g, b, alpha):
    scale = 1.0 + gate

    def body(i, carry):
        rows = pl.ds(pl.multiple_of(i * LN_ROWS, LN_ROWS), LN_ROWS)
        y_ref[rows, :] = _layer_norm(alpha * x_ref[rows, :] + scale * y_ref[rows, :], g, b)
        return carry

    lax.fori_loop(0, y_ref.shape[0] // LN_ROWS, body, 0)


def _adaln_kernel(ct_ref, w_ref, b_ref, ada_ref, o_ref, acc_ref, *, nb, nk):
    k = pl.program_id(1)

    @pl.when(k == 0)
    def _():
        acc_ref[...] = jnp.zeros_like(acc_ref)

    s = jax.nn.silu(ct_ref[...])
    w = w_ref[...]
    tk, tn = w.shape
    for b in range(nb):
        prod = w * s[:, b:b + 1]
        acc_ref[b] += jnp.sum(prod.reshape(tk // SUBLANES, SUBLANES, tn), axis=0)

    @pl.when(k == nk - 1)
    def _():
        for b in range(nb):
            t0 = jnp.sum(acc_ref[b], axis=0, keepdims=True) + b_ref[...]
            o_ref[b] = t0 + ada_ref[...]


def _adaln(c, w_c, b_c, ada_table):
    nb, d = c.shape
    n = w_c.shape[1]
    depth = ada_table.shape[0]
    tk = _tile(d, 512, SUBLANES)
    tn = _tile(n, 2048, LANES)
    nk = d // tk
    return pl.pallas_call(
        functools.partial(_adaln_kernel, nb=nb, nk=nk),
        grid=(n // tn, nk),
        in_specs=[
            pl.BlockSpec((tk, nb), lambda j, k: (k, 0)),
            pl.BlockSpec((tk, tn), lambda j, k: (k, j)),
            pl.BlockSpec((1, tn), lambda j, k: (0, j)),
            pl.BlockSpec((depth, tn), lambda j, k: (0, j)),
        ],
        out_specs=pl.BlockSpec((nb, depth, tn), lambda j, k: (0, 0, j)),
        out_shape=jax.ShapeDtypeStruct((nb, depth, n), F32),
        scratch_shapes=[pltpu.VMEM((nb, SUBLANES, tn), F32)],
        compiler_params=_params("parallel", "arbitrary"),
        name="adaln",
    )(c.T, w_c, b_c.reshape(1, n), ada_table)


def _pool_kernel(x_ref, xp_ref, mod_ref, w_ref, scale_ref, g_ref, b_ref, o_ref, hh_ref, *, alpha, ts, dg):
    s_idx = pl.program_id(1)
    m = mod_ref[0]
    sh, sc, gate = m[0:1], m[1:2], m[2:3]
    x = x_ref[0]
    h = x * (1.0 + sc) + sh
    hp = xp_ref[0] * (1.0 + sc) + sh
    hh_ref[0:POOL_HALO] = jnp.where(s_idx > 0, hp, 0.0)
    hh_ref[POOL_HALO:] = h
    pos = lax.broadcasted_iota(jnp.int32, (ts, 1), 0) + s_idx * ts
    ys = []
    for gi, win in enumerate(POOL_WINDOWS):
        cols = slice(gi * dg, (gi + 1) * dg)
        acc = hh_ref[pl.ds(POOL_HALO, ts), cols]
        for k in range(1, win):
            acc = acc + hh_ref[pl.ds(POOL_HALO - k, ts), cols]
        cnt = jnp.minimum(pos + 1, win).astype(F32)
        z = acc / cnt - h[:, cols]
        ys.append(jnp.dot(z.astype(BF16), w_ref[gi], preferred_element_type=F32))
    y = jnp.concatenate(ys, axis=1) * scale_ref[...]
    r = alpha * x + (1.0 + gate) * y
    o_ref[0] = _layer_norm(r, g_ref[...], b_ref[...])


def _pool_layer(x, mod, w, layer, scale, ln_g, ln_b, alpha):
    nb, s, d = x.shape
    _, g, dg, _ = w.shape
    ts = _tile(s, 256, POOL_HALO)
    hb = ts // POOL_HALO
    return pl.pallas_call(
        functools.partial(_pool_kernel, alpha=alpha, ts=ts, dg=dg),
        grid=(nb, s // ts),
        in_specs=[
            pl.BlockSpec((1, ts, d), lambda b, i: (b, i, 0)),
            pl.BlockSpec((1, POOL_HALO, d), lambda b, i: (b, jnp.maximum(i * hb - 1, 0), 0)),
            pl.BlockSpec((1, N_MOD, d), lambda b, i: (b, 0, 0)),
            pl.BlockSpec((None, g, dg, dg), lambda b, i: (layer, 0, 0, 0)),
            pl.BlockSpec((1, d), lambda b, i: (0, 0)),
            pl.BlockSpec((1, d), lambda b, i: (0, 0)),
            pl.BlockSpec((1, d), lambda b, i: (0, 0)),
        ],
        out_specs=pl.BlockSpec((1, ts, d), lambda b, i: (b, i, 0)),
        out_shape=jax.ShapeDtypeStruct((nb, s, d), F32),
        scratch_shapes=[pltpu.VMEM((ts + POOL_HALO, d), F32)],
        compiler_params=_params("parallel", "parallel"),
        name="pool_layer",
    )(x, x, mod, w, scale.reshape(1, d), ln_g.reshape(1, d), ln_b.reshape(1, d))


def _cast_kernel(x_ref, o_ref):
    o_ref[...] = x_ref[...].astype(o_ref.dtype)


def _cast_rows(w, nrows):
    nl, _, cols = w.shape
    tk = _tile(nrows, 512, PACKED_ROWS)
    tn = _tile(cols, 2048, LANES)
    return pl.pallas_call(
        _cast_kernel,
        grid=(nl, nrows // tk, cols // tn),
        in_specs=[pl.BlockSpec((None, tk, tn), lambda l, i, j: (l, i, j))],
        out_specs=pl.BlockSpec((None, tk, tn), lambda l, i, j: (l, i, j)),
        out_shape=jax.ShapeDtypeStruct((nl, nrows, cols), BF16),
        compiler_params=_params("parallel", "parallel", "parallel"),
        name="cast_rows",
    )(w)


def _fox_proj_kernel(x_ref, mod_ref, w_ref, wf_ref, qkv_ref, f_ref, h_ref, *, nq_tiles, sm_scale, nheads):
    j = pl.program_id(1)

    @pl.when(j == 0)
    def _():
        m = mod_ref[0]
        h = x_ref[...] * (1.0 + m[1:2]) + m[0:1]
        h_hi = h.astype(BF16)
        h_ref[...] = h_hi
        row = lax.broadcasted_iota(jnp.int32, wf_ref.shape, 0)
        wf = jnp.where(row < nheads, wf_ref[...], 0.0)
        wf_hi = wf.astype(BF16)
        wf_lo = (wf - wf_hi.astype(F32)).astype(BF16)
        h_lo = (h - h_hi.astype(F32)).astype(BF16)
        f_ref[...] = (lax.dot_general(h_hi, wf_hi, NT_DIMS, preferred_element_type=F32)
                      + lax.dot_general(h_lo, wf_hi, NT_DIMS, preferred_element_type=F32)
                      + lax.dot_general(h_hi, wf_lo, NT_DIMS, preferred_element_type=F32))

    y = lax.dot_general(h_ref[...], w_ref[...], NT_DIMS, preferred_element_type=F32)
    y = y * jnp.where(j < nq_tiles, sm_scale, 1.0)
    qkv_ref[...] = y.astype(BF16)


def _fox_proj(x2, mod, w_qkv_t, w_in_t, layer, nheads, seq, sm_scale):
    t, d = x2.shape
    n = w_qkv_t.shape[1]
    assert n % LANES == 0 and nheads <= LANES, (n, nheads)
    hp = LANES
    tm = _tile(seq, 512, PACKED_ROWS)
    tn = _tile(d, 1024, LANES)
    bps = seq // tm
    return pl.pallas_call(
        functools.partial(_fox_proj_kernel, nq_tiles=d // tn, sm_scale=sm_scale, nheads=nheads),
        grid=(t // tm, n // tn),
        in_specs=[
            pl.BlockSpec((tm, d), lambda i, j: (i, 0)),
            pl.BlockSpec((1, N_MOD, d), lambda i, j: (i // bps, 0, 0)),
            pl.BlockSpec((None, tn, d), lambda i, j: (layer, j, 0)),
            pl.BlockSpec((None, hp, d), lambda i, j: (layer, n // hp, 0)),
        ],
        out_specs=[
            pl.BlockSpec((tm, tn), lambda i, j: (i, j)),
            pl.BlockSpec((tm, hp), lambda i, j: (i, 0)),
        ],
        out_shape=[jax.ShapeDtypeStruct((t, n), BF16), jax.ShapeDtypeStruct((t, hp), F32)],
        scratch_shapes=[pltpu.VMEM((tm, d), BF16)],
        compiler_params=_params("parallel", "arbitrary"),
        name="fox_proj",
    )(x2, mod, w_qkv_t, w_in_t)


def _split3(v):
    p1 = v.astype(BF16)
    r1 = v - p1.astype(F32)
    p2 = r1.astype(BF16)
    p3 = (r1 - p2.astype(F32)).astype(BF16)
    return p1, p2, p3


def _decay_kernel(f_ref, bf_ref, o_ref, *, chunk):
    s = f_ref.shape[1]
    row = lax.broadcasted_iota(jnp.int32, (chunk, chunk), 0)
    col = lax.broadcasted_iota(jnp.int32, (chunk, chunk), 1)
    tri = (col <= row).astype(BF16)
    carry = jnp.zeros((1, f_ref.shape[2]), F32)
    for c in range(s // chunk):
        rows = slice(c * chunk, (c + 1) * chunk)
        lf = jax.nn.log_sigmoid(f_ref[0, rows, :] + bf_ref[...])
        p1, p2, p3 = _split3(lf)
        cs = (jnp.dot(tri, p1, preferred_element_type=F32)
              + jnp.dot(tri, p2, preferred_element_type=F32)
              + jnp.dot(tri, p3, preferred_element_type=F32)) + carry
        o_ref[0, rows, :] = cs
        carry = cs[chunk - 1:chunk, :]


def _decay_cumsum(f, b_f):
    nb, s, hp = f.shape
    chunk = _tile(s, 256, PACKED_ROWS)
    return pl.pallas_call(
        functools.partial(_decay_kernel, chunk=chunk),
        grid=(nb,),
        in_specs=[pl.BlockSpec((1, s, hp), lambda b: (b, 0, 0)), pl.BlockSpec((1, hp), lambda b: (0, 0))],
        out_specs=pl.BlockSpec((1, s, hp), lambda b: (b, 0, 0)),
        out_shape=jax.ShapeDtypeStruct((nb, s, hp), F32),
        compiler_params=_params("parallel"),
        name="fox_decay",
    )(f, b_f)


def _attn_kernel(q_ref, k_ref, v_ref, cq_ref, ck_ref, o_ref, m_ref, l_ref, acc_ref, cqb_ref, *, tq):
    qi = pl.program_id(2)
    nch = tq // LANES
    m_ref[...] = jnp.full_like(m_ref, -jnp.inf)
    l_ref[...] = jnp.zeros_like(l_ref)
    acc_ref[...] = jnp.zeros_like(acc_ref)
    cqb_ref[...] = jnp.broadcast_to(cq_ref[0, 0], cqb_ref.shape)

    def step(off, masked):
        kb = k_ref[0, pl.ds(off, tq), :]
        vb = v_ref[0, pl.ds(off, tq), :]
        ck = ck_ref[0, 0, :, pl.ds(off, tq)]
        s = lax.dot_general(q_ref[0], kb, NT_DIMS, preferred_element_type=F32)
        u = []
        for c in range(nch):
            uc = s[:, c * LANES:(c + 1) * LANES] - ck[:, c * LANES:(c + 1) * LANES]
            if masked:
                row = lax.broadcasted_iota(jnp.int32, (tq, LANES), 0)
                col = lax.broadcasted_iota(jnp.int32, (tq, LANES), 1) + c * LANES
                uc = jnp.where(col <= row, uc, -jnp.inf)
            u.append(uc)
        bm = functools.reduce(jnp.maximum, u)
        cqb = cqb_ref[...]
        m_old = m_ref[...]
        m_new = jnp.maximum(m_old, jnp.max(bm, axis=1, keepdims=True) + cqb)
        a = jnp.exp(m_old - m_new)
        r = cqb - m_new
        p = [jnp.exp(uc + r) for uc in u]
        l_ref[...] = a * l_ref[...] + functools.reduce(jnp.add, p)
        pb = jnp.concatenate([pc.astype(BF16) for pc in p], axis=1)
        acc_ref[...] = a * acc_ref[...] + jnp.dot(pb, vb, preferred_element_type=F32)
        m_ref[...] = m_new

    def body(ki, carry):
        step(pl.multiple_of(ki * tq, tq), False)
        return carry

    lax.fori_loop(0, qi, body, 0)
    step(pl.multiple_of(qi * tq, tq), True)
    l = jnp.sum(l_ref[...], axis=1, keepdims=True)
    o_ref[0] = (acc_ref[...] / l).astype(o_ref.dtype)


def _fox_attention(qkv, cq, ck, nheads, dh):
    nb, s, _ = qkv.shape
    d = nheads * dh
    assert dh == LANES, dh
    tq = _tile(s, 1024, LANES)
    stat = pltpu.VMEM((tq, LANES), F32)
    return pl.pallas_call(
        functools.partial(_attn_kernel, tq=tq),
        grid=(nb, nheads, s // tq),
        in_specs=[
            pl.BlockSpec((1, tq, dh), lambda b, h, i: (b, i, h)),
            pl.BlockSpec((1, s, dh), lambda b, h, i: (b, 0, nheads + h)),
            pl.BlockSpec((1, s, dh), lambda b, h, i: (b, 0, 2 * nheads + h)),
            pl.BlockSpec((1, 1, tq, 1), lambda b, h, i: (b, h, i, 0)),
            pl.BlockSpec((1, 1, 1, s), lambda b, h, i: (b, h, 0, 0)),
        ],
        out_specs=pl.BlockSpec((1, tq, dh), lambda b, h, i: (b, i, h)),
        out_shape=jax.ShapeDtypeStruct((nb, s, d), BF16),
        scratch_shapes=[stat, stat, pltpu.VMEM((tq, dh), F32), stat],
        compiler_params=_params("parallel", "parallel", "parallel"),
        name="fox_attention",
    )(qkv, qkv, qkv, cq, ck)


def _out_ln_kernel(o_ref, w_ref, x_ref, mod_ref, g_ref, b_ref, out_ref, *, alpha, tn, nj, gate_row):
    j = pl.program_id(1)
    y = jnp.dot(o_ref[...], w_ref[...], preferred_element_type=F32)
    out_ref[:, pl.ds(pl.multiple_of(j * tn, tn), tn)] = y

    @pl.when(j == nj - 1)
    def _():
        gate = mod_ref[0][gate_row:gate_row + 1]
        r = alpha * x_ref[...] + (1.0 + gate) * out_ref[...]
        out_ref[...] = _layer_norm(r, g_ref[...], b_ref[...])


def _fox_out_layer(o2, w_o, layer, x2, mod, ln_g, ln_b, seq, alpha):
    t, d = x2.shape
    tm = _tile(seq, 512, PACKED_ROWS)
    tn = _tile(d, 1024, LANES)
    bps = seq // tm
    nj = d // tn
    return pl.pallas_call(
        functools.partial(_out_ln_kernel, alpha=alpha, tn=tn, nj=nj, gate_row=2),
        grid=(t // tm, nj),
        in_specs=[
            pl.BlockSpec((tm, d), lambda i, j: (i, 0)),
            pl.BlockSpec((None, d, tn), lambda i, j: (layer, 0, j)),
            pl.BlockSpec((tm, d), lambda i, j: (i, 0), **ONE_BUFFER),
            pl.BlockSpec((1, N_MOD, d), lambda i, j: (i // bps, 0, 0)),
            pl.BlockSpec((1, d), lambda i, j: (0, 0)),
            pl.BlockSpec((1, d), lambda i, j: (0, 0)),
        ],
        out_specs=pl.BlockSpec((tm, d), lambda i, j: (i, 0), **ONE_BUFFER),
        out_shape=jax.ShapeDtypeStruct((t, d), F32),
        compiler_params=_params("parallel", "arbitrary"),
        name="fox_out_layer",
    )(o2, w_o, x2, mod, ln_g.reshape(1, d), ln_b.reshape(1, d))


def _top_ranked(s, k, exact):
    rows = s.shape[0]
    row = lax.broadcasted_iota(jnp.int32, s.shape, 0)
    rank = jnp.full(s.shape, float(k), F32)
    vals = []
    for r in range(k):
        m = jnp.max(s, axis=0, keepdims=True)
        sel = s == m
        if exact:
            first = jnp.min(jnp.where(sel, row, rows), axis=0, keepdims=True)
            sel = row == first
        rank = jnp.where(sel, float(r), rank)
        vals.append(m)
        s = jnp.where(sel, -jnp.inf, s)
    return jnp.concatenate(vals, axis=0), rank


def _overfull(ranks, topk):
    fk = float(topk)
    counts = [jnp.sum((rk < fk).astype(F32), axis=0, keepdims=True) for rk in ranks]
    return jnp.max(functools.reduce(jnp.maximum, [jnp.abs(c - fk) for c in counts]))


def _key_stage(q, keys_ref, topk, exact):
    dhalf = q.shape[1] // 2
    out = []
    for p in range(2):
        qp = q[:, p * dhalf:(p + 1) * dhalf]
        s = lax.dot_general(keys_ref[p], qp, NT_DIMS, preferred_element_type=F32)
        out.append((s,) + _top_ranked(s, topk, exact))
    return out


def _pair_stage(s1, tv1, rank1, tv2, n1_ref, e1_ref, hd, topk, exact):
    half = topk // 2
    sub = lax.broadcasted_iota(jnp.int32, (8, 1), 0)
    pieces = [tv1[0:1] + tv2]
    for a in range(1, half):
        pieces.append(jnp.where(sub < topk // (a + 1), tv1[a:a + 1] + tv2[0:8], -jnp.inf))
    pieces.append(tv1[half:topk] + tv2[0:1])
    cand = jnp.concatenate(pieces, axis=0)
    cvals, crank = _top_ranked(cand, topk, exact)
    taken = crank < float(topk)
    z = jnp.sum(jnp.where(taken, jnp.exp(cand - cvals[0:1]), 0.0), axis=0, keepdims=True)
    takenf = taken.astype(F32)

    n1 = jnp.zeros_like(rank1)
    start = 0
    for a in range(half):
        size = topk if a == 0 else 8
        n_a = jnp.sum(takenf[start:start + size], axis=0, keepdims=True)
        n1 = jnp.where(rank1 == float(a), n_a, n1)
        start += size
    for a in range(half, topk):
        n1 = jnp.where(rank1 == float(a), takenf[start + a - half:start + a - half + 1], n1)

    n1_ref[:, pl.ds(hd, 1), :] = n1[:, None, :]
    e1_ref[:, pl.ds(hd, 1), :] = (SQRT_HALF * jnp.exp(s1 - tv1[0:1]) / z)[:, None, :]
    return crank


def _router_kernel(x_ref, mod_ref, wq_ref, keys_ref, *rest, topk, cast_weights):
    if cast_weights:
        uw_ref, vw_ref, hbf_ref, n1_ref, e1_ref, r2_ref, e2_ref, u16_ref, v16_ref = rest
        u16_ref[...] = (uw_ref[...] * SQRT_HALF).astype(BF16)
        v16_ref[...] = vw_ref[...].astype(BF16)
    else:
        hbf_ref, n1_ref, e1_ref, r2_ref, e2_ref = rest
    hd = pl.program_id(1)

    @pl.when(hd == 0)
    def _():
        m = mod_ref[0]
        hbf_ref[...] = (x_ref[...] * (1.0 + m[4:5]) + m[3:4]).astype(BF16)

    kh = hbf_ref.shape[1] // 2
    q = (jnp.dot(hbf_ref[:, :kh], wq_ref[:kh, :], preferred_element_type=F32)
         + jnp.dot(hbf_ref[:, kh:], wq_ref[kh:, :], preferred_element_type=F32)).astype(BF16)

    def write_second_half(s2, tv2, rank2):
        r2_ref[0] = rank2
        e2_ref[0] = jnp.exp(s2 - tv2[0:1])

    (s1, tv1, rank1), (s2, tv2, rank2) = _key_stage(q, keys_ref, topk, exact=False)
    key_tie = _overfull([rank1, rank2], topk)
    crank = _pair_stage(s1, tv1, rank1, tv2, n1_ref, e1_ref, hd, topk, exact=False)
    pair_tie = _overfull([crank], topk)
    write_second_half(s2, tv2, rank2)

    @pl.when(key_tie > 0.5)
    def _():
        (s1x, tv1x, rank1x), (s2x, tv2x, rank2x) = _key_stage(q, keys_ref, topk, exact=True)
        _pair_stage(s1x, tv1x, rank1x, tv2x, n1_ref, e1_ref, hd, topk, exact=True)
        write_second_half(s2x, tv2x, rank2x)

    @pl.when(jnp.logical_and(key_tie <= 0.5, pair_tie > 0.5))
    def _():
        _pair_stage(s1, tv1, rank1, tv2, n1_ref, e1_ref, hd, topk, exact=True)


def _peer_router(x2, mod, w_q, keys, layer, seq, weights=None):
    t, d = x2.shape
    _, _, nkeys, dhalf = keys.shape
    ph = w_q.shape[2] // (2 * dhalf)
    tr = _tile(seq, 512, LANES)
    bps = seq // tr
    route = jax.ShapeDtypeStruct((ph, nkeys, t), F32)
    route_spec = pl.BlockSpec((1, nkeys, tr), lambda i, h: (h, 0, i))
    key_major = jax.ShapeDtypeStruct((nkeys, ph, t), F32)
    key_spec = pl.BlockSpec((nkeys, ph, tr), lambda i, h: (0, 0, i))
    in_specs = [
        pl.BlockSpec((tr, d), lambda i, h: (i, 0)),
        pl.BlockSpec((1, N_MOD, d), lambda i, h: (i // bps, 0, 0)),
        pl.BlockSpec((None, d, 2 * dhalf), lambda i, h: (layer, 0, h)),
        pl.BlockSpec((None, 2, nkeys, dhalf), lambda i, h: (layer, 0, 0, 0)),
    ]
    operands = [x2, mod, w_q, keys]
    out_specs = [pl.BlockSpec((tr, d), lambda i, h: (i, 0)), key_spec, key_spec, route_spec, route_spec]
    out_shape = [jax.ShapeDtypeStruct((t, d), BF16), key_major, key_major, route, route]
    if weights is not None:
        ne = weights[0].shape[1]
        steps = (t // tr) * ph
        rows = ne // steps
        assert rows * steps == ne and rows % PACKED_ROWS == 0, (ne, steps)
        in_specs += [pl.BlockSpec((None, rows, d), lambda i, h: (layer, i * ph + h, 0))] * 2
        operands += list(weights)
        out_specs += [pl.BlockSpec((rows, d), lambda i, h: (i * ph + h, 0))] * 2
        out_shape += [jax.ShapeDtypeStruct((ne, d), BF16)] * 2
    return pl.pallas_call(
        functools.partial(_router_kernel, topk=PEER_TOPK, cast_weights=weights is not None),
        grid=(t // tr, ph),
        in_specs=in_specs,
        out_specs=out_specs,
        out_shape=out_shape,
        compiler_params=_params("parallel", "arbitrary"),
        name="peer_router",
    )(*operands)


def _gate_block(a_scr, wt_scr, n1_ref, e1_ref, r2_ref, e2_ref, key0, nkeys, ph):
    ts, tm = a_scr.shape
    for il in range(ts // nkeys):
        i_key = key0 + il
        for lc in range(tm // LANES):
            cols = slice(lc * LANES, (lc + 1) * LANES)
            n1t = n1_ref[i_key, :, cols]
            e1t = e1_ref[i_key, :, cols]
            n1b = [n1t[hd:hd + 1, :] for hd in range(ph)]
            e1b = [e1t[hd:hd + 1, :] for hd in range(ph)]
            tiles = []
            for rc in range(nkeys // SUBLANES):
                rows = slice(rc * SUBLANES, (rc + 1) * SUBLANES)
                w = None
                for hd in range(ph):
                    t = jnp.where(r2_ref[hd, rows, cols] < n1b[hd], e2_ref[hd, rows, cols], 0.0) * e1b[hd]
                    w = t if w is None else w + t
                b = a_scr[il * nkeys + rc * SUBLANES:il * nkeys + (rc + 1) * SUBLANES, cols]
                tiles.append(w * (b * (1.0 + lax.erf(b))))
            blk = jnp.concatenate(tiles, axis=0).astype(BF16)
            wt_scr[cols, il * nkeys:(il + 1) * nkeys] = blk.T


def _expert_kernel(h_ref, u_ref, v_ref, n1_ref, e1_ref, r2_ref, e2_ref, x_ref, mod_ref, g_ref, b_ref, *rest,
                   alpha, nkeys, ph, nj, ts, cast_next):
    if cast_next:
        un_ref, vn_ref, o_ref, un16_ref, vn16_ref, a_e, a_o, wt_e, wt_o = rest
        un16_ref[...] = (un_ref[...] * SQRT_HALF).astype(BF16)
        vn16_ref[...] = vn_ref[...].astype(BF16)
    else:
        o_ref, a_e, a_o, wt_e, wt_o = rest
    j = pl.program_id(1)
    tm, d = o_ref.shape
    gps = ts // nkeys

    @pl.when(j == 0)
    def _():
        o_ref[...] = jnp.zeros_like(o_ref)
        wt_e[...] = jnp.zeros_like(wt_e)
        a_o[...] = jnp.zeros_like(a_o)

    def mm1(rows, a_scr):
        half = tm // 2
        for s in range(2):
            a_scr[:, s * half:(s + 1) * half] = lax.dot_general(
                u_ref[rows, :], h_ref[s * half:(s + 1) * half, :], NT_DIMS, preferred_element_type=F32)

    def mm2(wt_scr, rows):
        half = d // 2
        for s in range(2):
            o_ref[:, s * half:(s + 1) * half] += jnp.dot(
                wt_scr[...], v_ref[rows, s * half:(s + 1) * half], preferred_element_type=F32)

    lo, hi = slice(0, ts), slice(ts, 2 * ts)
    last = 2 * nj - 1
    mm2(wt_e, lo)
    _gate_block(a_o, wt_o, n1_ref, e1_ref, r2_ref, e2_ref, jnp.maximum(2 * j - 1, 0) * gps, nkeys, ph)
    mm1(lo, a_e)
    mm2(wt_o, hi)
    _gate_block(a_e, wt_e, n1_ref, e1_ref, r2_ref, e2_ref, jnp.minimum(2 * j, last) * gps, nkeys, ph)
    mm1(hi, a_o)

    @pl.when(j == nj)
    def _():
        _residual_layer_norm_rows(x_ref, o_ref, mod_ref[0][5:6], g_ref[...], b_ref[...], alpha)


def _peer_experts(hbf, u, v, n1t, e1t, r2, e2, x2, mod, ln_g, ln_b, seq, alpha, nxt=None):
    t, d = x2.shape
    ne = u.shape[0]
    ph, nkeys, _ = r2.shape
    tm = _tile(seq, 512, 2 * LANES)
    ts = _tile(ne // 2, 256, nkeys)
    bps = seq // tm
    nj = ne // (2 * ts)
    route_spec = pl.BlockSpec((ph, nkeys, tm), lambda i, j: (0, 0, i), **ONE_BUFFER)
    key_spec = pl.BlockSpec((nkeys, ph, tm), lambda i, j: (0, 0, i), **ONE_BUFFER)
    in_specs = [
        pl.BlockSpec((tm, d), lambda i, j: (i, 0)),
        pl.BlockSpec((2 * ts, d), lambda i, j: (jnp.minimum(j, nj - 1), 0)),
        pl.BlockSpec((2 * ts, d), lambda i, j: (jnp.maximum(j - 1, 0), 0)),
        key_spec, key_spec, route_spec, route_spec,
        pl.BlockSpec((tm, d), lambda i, j: (i, 0), **ONE_BUFFER),
        pl.BlockSpec((1, N_MOD, d), lambda i, j: (i // bps, 0, 0)),
        pl.BlockSpec((1, d), lambda i, j: (0, 0)),
        pl.BlockSpec((1, d), lambda i, j: (0, 0)),
    ]
    operands = [hbf, u, v, n1t, e1t, r2, e2, x2, mod, ln_g.reshape(1, d), ln_b.reshape(1, d)]
    out_specs = [pl.BlockSpec((tm, d), lambda i, j: (i, 0), **ONE_BUFFER)]
    out_shape = [jax.ShapeDtypeStruct((t, d), F32)]
    if nxt is not None:
        u_all, v_all, layer = nxt
        steps = (t // tm) * (nj + 1)
        per_step = -(-ne // steps)
        rows = -(-per_step // PACKED_ROWS) * PACKED_ROWS
        assert ne % rows == 0, (ne, rows)
        last = ne // rows - 1

        def chunk(i, j):
            return jnp.minimum(i * (nj + 1) + j, last)

        in_specs += [pl.BlockSpec((None, rows, d), lambda i, j: (layer, chunk(i, j), 0))] * 2
        operands += [u_all, v_all]
        out_specs += [pl.BlockSpec((rows, d), lambda i, j: (chunk(i, j), 0))] * 2
        out_shape += [jax.ShapeDtypeStruct((ne, d), BF16)] * 2
    return pl.pallas_call(
        functools.partial(_expert_kernel, alpha=alpha, nkeys=nkeys, ph=ph, nj=nj, ts=ts, cast_next=nxt is not None),
        grid=(t // tm, nj + 1),
        in_specs=in_specs,
        out_specs=out_specs,
        out_shape=out_shape,
        scratch_shapes=[pltpu.VMEM((ts, tm), F32), pltpu.VMEM((ts, tm), F32),
                        pltpu.VMEM((tm, ts), BF16), pltpu.VMEM((tm, ts), BF16)],
        compiler_params=_params("parallel", "arbitrary"),
        name="peer_experts",
    )(*operands)


def kernel(x, c, w_c, b_c, ada_table, ln_tok_g, ln_tok_b, ln_ch_g, ln_ch_b, pool_w, pool_scale,
           fox_w_in, fox_b_f, fox_w_o, peer_w_q, peer_keys, peer_u, peer_v):
    nb, seq, d = x.shape
    depth = ada_table.shape[0]
    alpha = float((2 * depth) ** 0.25)
    nheads = fox_b_f.shape[1]
    dh = d // nheads
    hpad = -(-nheads // LANES) * LANES

    mod_all = _adaln(c, w_c, b_c, ada_table)

    pool_w16 = pool_w.astype(BF16)
    w_in_t = jnp.swapaxes(fox_w_in, 1, 2)
    w_qkv16 = _cast_rows(w_in_t, 3 * d)
    b_f = jnp.pad(fox_b_f, ((0, 0), (0, hpad - nheads)))
    w_o16 = fox_w_o.astype(BF16)
    w_q16 = peer_w_q.astype(BF16)
    keys16 = peer_keys.astype(BF16)
    u16 = v16 = None

    n_mixers = 2
    for i in range(depth):
        mod = mod_all[:, i].reshape(nb, N_MOD, d)
        jl = i // n_mixers
        if i % n_mixers == 0:
            x = _pool_layer(x, mod, pool_w16, jl, pool_scale[jl], ln_tok_g[i], ln_tok_b[i], alpha)
        else:
            x2 = x.reshape(nb * seq, d)
            qkv, f = _fox_proj(x2, mod, w_qkv16, w_in_t, jl, nheads, seq, float(dh) ** -0.5)
            cum = _decay_cumsum(f.reshape(nb, seq, hpad), b_f[jl].reshape(1, hpad))[:, :, :nheads]
            cum_t = cum.transpose(0, 2, 1)
            o = _fox_attention(qkv.reshape(nb, seq, 3 * d), cum_t[:, :, :, None], cum_t[:, :, None, :], nheads, dh)
            x = _fox_out_layer(o.reshape(nb * seq, d), w_o16, jl, x2, mod, ln_tok_g[i], ln_tok_b[i], seq, alpha)
            x = x.reshape(nb, seq, d)
        x2 = x.reshape(nb * seq, d)
        hbf, n1t, e1t, r2, e2, *first = _peer_router(x2, mod, w_q16, keys16, i, seq, (peer_u, peer_v) if i == 0 else None)
        if first:
            u16, v16 = first
        nxt = (peer_u, peer_v, i + 1) if i + 1 < depth else None
        x, *converted = _peer_experts(hbf, u16, v16, n1t, e1t, r2, e2, x2, mod, ln_ch_g[i], ln_ch_b[i], seq, alpha, nxt)
        if converted:
            u16, v16 = converted
        x = x.reshape(nb, seq, d)
    return x
```

```python
import functools

import jax
import jax.numpy as jnp
from jax import lax
from jax.experimental import pallas as pl
from jax.experimental.pallas import tpu as pltpu

F32 = jnp.float32
BF16 = jnp.bfloat16

LN_EPS = 1e-5
N_MOD = 6
POOL_WINDOWS = (2, 4, 8, 16)
POOL_HALO = 16
PEER_TOPK = 16
SQRT_HALF = 0.7071067811865476
LANES = 128
SUBLANES = 8
PACKED_ROWS = 16
LN_ROWS = 32
VMEM_LIMIT_BYTES = 56 * 1024 * 1024


NT_DIMS = (((1,), (1,)), ((), ()))
ONE_BUFFER = dict(pipeline_mode=pl.Buffered(1))


def _params(*semantics):
    return pltpu.CompilerParams(dimension_semantics=semantics, vmem_limit_bytes=VMEM_LIMIT_BYTES)


def _tile(dim, target, align):
    if dim <= target:
        return dim
    t = (target // align) * align
    while t > align and dim % t:
        t -= align
    assert dim % t == 0, (dim, target, align)
    return t


def _layer_norm(r, g, b):
    mu = jnp.mean(r, axis=-1, keepdims=True)
    xc = r - mu
    var = jnp.mean(xc * xc, axis=-1, keepdims=True)
    return xc * lax.rsqrt(var + LN_EPS) * g + b


def _residual_layer_norm_rows(x_ref, y_ref, gate, g, b, alpha):
    scale = 1.0 + gate

    def body(i, carry):
        rows = pl.ds(pl.multiple_of(i * LN_ROWS, LN_ROWS), LN_ROWS)
        y_ref[rows, :] = _layer_norm(alpha * x_ref[rows, :] + scale * y_ref[rows, :], g, b)
        return carry

    lax.fori_loop(0, y_ref.shape[0] // LN_ROWS, body, 0)


def _adaln_kernel(ct_ref, w_ref, b_ref, ada_ref, o_ref, acc_ref, *, nb, nk):
    k = pl.program_id(1)

    @pl.when(k == 0)
    def _():
        acc_ref[...] = jnp.zeros_like(acc_ref)

    s = jax.nn.silu(ct_ref[...])
    w = w_ref[...]
    tk, tn = w.shape
    for b in range(nb):
        prod = w * s[:, b:b + 1]
        acc_ref[b] += jnp.sum(prod.reshape(tk // SUBLANES, SUBLANES, tn), axis=0)

    @pl.when(k == nk - 1)
    def _():
        for b in range(nb):
            t0 = jnp.sum(acc_ref[b], axis=0, keepdims=True) + b_ref[...]
            o_ref[b] = t0 + ada_ref[...]


def _adaln(c, w_c, b_c, ada_table):
    nb, d = c.shape
    n = w_c.shape[1]
    depth = ada_table.shape[0]
    tk = _tile(d, 512, SUBLANES)
    tn = _tile(n, 2048, LANES)
    nk = d // tk
    return pl.pallas_call(
        functools.partial(_adaln_kernel, nb=nb, nk=nk),
        grid=(n // tn, nk),
        in_specs=[
            pl.BlockSpec((tk, nb), lambda j, k: (k, 0)),
            pl.BlockSpec((tk, tn), lambda j, k: (k, j)),
            pl.BlockSpec((1, tn), lambda j, k: (0, j)),
            pl.BlockSpec((depth, tn), lambda j, k: (0, j)),
        ],
        out_specs=pl.BlockSpec((nb, depth, tn), lambda j, k: (0, 0, j)),
        out_shape=jax.ShapeDtypeStruct((nb, depth, n), F32),
        scratch_shapes=[pltpu.VMEM((nb, SUBLANES, tn), F32)],
        compiler_params=_params("parallel", "arbitrary"),
        name="adaln",
    )(c.T, w_c, b_c.reshape(1, n), ada_table)


def _pool_kernel(x_ref, xp_ref, mod_ref, w_ref, scale_ref, g_ref, b_ref, o_ref, hh_ref, *, alpha, ts, dg):
    s_idx = pl.program_id(1)
    m = mod_ref[0]
    sh, sc, gate = m[0:1], m[1:2], m[2:3]
    x = x_ref[0]
    h = x * (1.0 + sc) + sh
    hp = xp_ref[0] * (1.0 + sc) + sh
    hh_ref[0:POOL_HALO] = jnp.where(s_idx > 0, hp, 0.0)
    hh_ref[POOL_HALO:] = h
    pos = lax.broadcasted_iota(jnp.int32, (ts, 1), 0) + s_idx * ts
    ys = []
    for gi, win in enumerate(POOL_WINDOWS):
        cols = slice(gi * dg, (gi + 1) * dg)
        acc = hh_ref[pl.ds(POOL_HALO, ts), cols]
        for k in range(1, win):
            acc = acc + hh_ref[pl.ds(POOL_HALO - k, ts), cols]
        cnt = jnp.minimum(pos + 1, win).astype(F32)
        z = acc / cnt - h[:, cols]
        ys.append(jnp.dot(z.astype(BF16), w_ref[gi], preferred_element_type=F32))
    y = jnp.concatenate(ys, axis=1) * scale_ref[...]
    r = alpha * x + (1.0 + gate) * y
    o_ref[0] = _layer_norm(r, g_ref[...], b_ref[...])


def _pool_layer(x, mod, w, layer, scale, ln_g, ln_b, alpha):
    nb, s, d = x.shape
    _, g, dg, _ = w.shape
    ts = _tile(s, 256, POOL_HALO)
    hb = ts // POOL_HALO
    return pl.pallas_call(
        functools.partial(_pool_kernel, alpha=alpha, ts=ts, dg=dg),
        grid=(nb, s // ts),
        in_specs=[
            pl.BlockSpec((1, ts, d), lambda b, i: (b, i, 0)),
            pl.BlockSpec((1, POOL_HALO, d), lambda b, i: (b, jnp.maximum(i * hb - 1, 0), 0)),
            pl.BlockSpec((1, N_MOD, d), lambda b, i: (b, 0, 0)),
            pl.BlockSpec((None, g, dg, dg), lambda b, i: (layer, 0, 0, 0)),
            pl.BlockSpec((1, d), lambda b, i: (0, 0)),
            pl.BlockSpec((1, d), lambda b, i: (0, 0)),
            pl.BlockSpec((1, d), lambda b, i: (0, 0)),
        ],
        out_specs=pl.BlockSpec((1, ts, d), lambda b, i: (b, i, 0)),
        out_shape=jax.ShapeDtypeStruct((nb, s, d), F32),
        scratch_shapes=[pltpu.VMEM((ts + POOL_HALO, d), F32)],
        compiler_params=_params("parallel", "parallel"),
        name="pool_layer",
    )(x, x, mod, w, scale.reshape(1, d), ln_g.reshape(1, d), ln_b.reshape(1, d))


def _cast_kernel(x_ref, o_ref):
    o_ref[...] = x_ref[...].astype(o_ref.dtype)


def _cast_rows(w, nrows):
    nl, _, cols = w.shape
    tk = _tile(nrows, 512, PACKED_ROWS)
    tn = _tile(cols, 2048, LANES)
    return pl.pallas_call(
        _cast_kernel,
        grid=(nl, nrows // tk, cols // tn),
        in_specs=[pl.BlockSpec((None, tk, tn), lambda l, i, j: (l, i, j))],
        out_specs=pl.BlockSpec((None, tk, tn), lambda l, i, j: (l, i, j)),
        out_shape=jax.ShapeDtypeStruct((nl, nrows, cols), BF16),
        compiler_params=_params("parallel", "parallel", "parallel"),
        name="cast_rows",
    )(w)


def _fox_proj_kernel(x_ref, mod_ref, w_ref, wf_ref, qkv_ref, f_ref, h_ref, *, nq_tiles, sm_scale, nheads):
    j = pl.program_id(1)

    @pl.when(j == 0)
    def _():
        m = mod_ref[0]
        h = x_ref[...] * (1.0 + m[1:2]) + m[0:1]
        h_hi = h.astype(BF16)
        h_ref[...] = h_hi
        row = lax.broadcasted_iota(jnp.int32, wf_ref.shape, 0)
        wf = jnp.where(row < nheads, wf_ref[...], 0.0)
        wf_hi = wf.astype(BF16)
        wf_lo = (wf - wf_hi.astype(F32)).astype(BF16)
        h_lo = (h - h_hi.astype(F32)).astype(BF16)
        f_ref[...] = (lax.dot_general(h_hi, wf_hi, NT_DIMS, preferred_element_type=F32)
                      + lax.dot_general(h_lo, wf_hi, NT_DIMS, preferred_element_type=F32)
                      + lax.dot_general(h_hi, wf_lo, NT_DIMS, preferred_element_type=F32))

    y = lax.dot_general(h_ref[...], w_ref[...], NT_DIMS, preferred_element_type=F32)
    y = y * jnp.where(j < nq_tiles, sm_scale, 1.0)
    qkv_ref[...] = y.astype(BF16)


def _fox_proj(x2, mod, w_qkv_t, w_in_t, layer, nheads, seq, sm_scale):
    t, d = x2.shape
    n = w_qkv_t.shape[1]
    assert n % LANES == 0 and nheads <= LANES, (n, nheads)
    hp = LANES
    tm = _tile(seq, 512, PACKED_ROWS)
    tn = _tile(d, 1024, LANES)
    bps = seq // tm
    return pl.pallas_call(
        functools.partial(_fox_proj_kernel, nq_tiles=d // tn, sm_scale=sm_scale, nheads=nheads),
        grid=(t // tm, n // tn),
        in_specs=[
            pl.BlockSpec((tm, d), lambda i, j: (i, 0)),
            pl.BlockSpec((1, N_MOD, d), lambda i, j: (i // bps, 0, 0)),
            pl.BlockSpec((None, tn, d), lambda i, j: (layer, j, 0)),
            pl.BlockSpec((None, hp, d), lambda i, j: (layer, n // hp, 0)),
        ],
        out_specs=[
            pl.BlockSpec((tm, tn), lambda i, j: (i, j)),
            pl.BlockSpec((tm, hp), lambda i, j: (i, 0)),
        ],
        out_shape=[jax.ShapeDtypeStruct((t, n), BF16), jax.ShapeDtypeStruct((t, hp), F32)],
        scratch_shapes=[pltpu.VMEM((tm, d), BF16)],
        compiler_params=_params("parallel", "arbitrary"),
        name="fox_proj",
    )(x2, mod, w_qkv_t, w_in_t)


def _split3(v):
    p1 = v.astype(BF16)
    r1 = v - p1.astype(F32)
    p2 = r1.astype(BF16)
    p3 = (r1 - p2.astype(F32)).astype(BF16)
    return p1, p2, p3


def _decay_kernel(f_ref, bf_ref, o_ref, *, chunk):
    s = f_ref.shape[1]
    row = lax.broadcasted_iota(jnp.int32, (chunk, chunk), 0)
    col = lax.broadcasted_iota(jnp.int32, (chunk, chunk), 1)
    tri = (col <= row).astype(BF16)
    carry = jnp.zeros((1, f_ref.shape[2]), F32)
    for c in range(s // chunk):
        rows = slice(c * chunk, (c + 1) * chunk)
        lf = jax.nn.log_sigmoid(f_ref[0, rows, :] + bf_ref[...])
        p1, p2, p3 = _split3(lf)
        cs = (jnp.dot(tri, p1, preferred_element_type=F32)
              + jnp.dot(tri, p2, preferred_element_type=F32)
              + jnp.dot(tri, p3, preferred_element_type=F32)) + carry
        o_ref[0, rows, :] = cs
        carry = cs[chunk - 1:chunk, :]


def _decay_cumsum(f, b_f):
    nb, s, hp = f.shape
    chunk = _tile(s, 256, PACKED_ROWS)
    return pl.pallas_call(
        functools.partial(_decay_kernel, chunk=chunk),
        grid=(nb,),
        in_specs=[pl.BlockSpec((1, s, hp), lambda b: (b, 0, 0)), pl.BlockSpec((1, hp), lambda b: (0, 0))],
        out_specs=pl.BlockSpec((1, s, hp), lambda b: (b, 0, 0)),
        out_shape=jax.ShapeDtypeStruct((nb, s, hp), F32),
        compiler_params=_params("parallel"),
        name="fox_decay",
    )(f, b_f)


def _attn_kernel(q_ref, k_ref, v_ref, cq_ref, ck_ref, o_ref, m_ref, l_ref, acc_ref, cqb_ref, *, tq):
    qi = pl.program_id(2)
    nch = tq // LANES
    m_ref[...] = jnp.full_like(m_ref, -jnp.inf)
    l_ref[...] = jnp.zeros_like(l_ref)
    acc_ref[...] = jnp.zeros_like(acc_ref)
    lane = lax.broadcasted_iota(jnp.int32, (tq, LANES), 1)
    mine = jnp.where(lane == pl.program_id(1) % LANES, cq_ref[0], 0.0)
    cqb_ref[...] = jnp.broadcast_to(jnp.sum(mine, axis=1, keepdims=True), cqb_ref.shape)

    def step(off, masked):
        kb = k_ref[0, pl.ds(off, tq), :]
        vb = v_ref[0, pl.ds(off, tq), :]
        ck = ck_ref[0, 0, :, pl.ds(off, tq)]
        s = lax.dot_general(q_ref[0], kb, NT_DIMS, preferred_element_type=F32)
        u = []
        for c in range(nch):
            uc = s[:, c * LANES:(c + 1) * LANES] - ck[:, c * LANES:(c + 1) * LANES]
            if masked:
                row = lax.broadcasted_iota(jnp.int32, (tq, LANES), 0)
                col = lax.broadcasted_iota(jnp.int32, (tq, LANES), 1) + c * LANES
                uc = jnp.where(col <= row, uc, -jnp.inf)
            u.append(uc)
        bm = functools.reduce(jnp.maximum, u)
        cqb = cqb_ref[...]
        m_old = m_ref[...]
        m_new = jnp.maximum(m_old, jnp.max(bm, axis=1, keepdims=True) + cqb)
        a = jnp.exp(m_old - m_new)
        r = cqb - m_new
        p = [jnp.exp(uc + r) for uc in u]
        l_ref[...] = a * l_ref[...] + functools.reduce(jnp.add, p)
        pb = jnp.concatenate([pc.astype(BF16) for pc in p], axis=1)
        acc_ref[...] = a * acc_ref[...] + jnp.dot(pb, vb, preferred_element_type=F32)
        m_ref[...] = m_new

    def body(ki, carry):
        step(pl.multiple_of(ki * tq, tq), False)
        return carry

    lax.fori_loop(0, qi, body, 0)
    step(pl.multiple_of(qi * tq, tq), True)
    l = jnp.sum(l_ref[...], axis=1, keepdims=True)
    o_ref[0] = (acc_ref[...] / l).astype(o_ref.dtype)


def _fox_attention(qkv, cum, ck, nheads, dh):
    nb, s, _ = qkv.shape
    d = nheads * dh
    assert dh == LANES, dh
    tq = _tile(s, 1024, LANES)
    stat = pltpu.VMEM((tq, LANES), F32)
    return pl.pallas_call(
        functools.partial(_attn_kernel, tq=tq),
        grid=(nb, nheads, s // tq),
        in_specs=[
            pl.BlockSpec((1, tq, dh), lambda b, h, i: (b, i, h)),
            pl.BlockSpec((1, s, dh), lambda b, h, i: (b, 0, nheads + h)),
            pl.BlockSpec((1, s, dh), lambda b, h, i: (b, 0, 2 * nheads + h)),
            pl.BlockSpec((1, tq, LANES), lambda b, h, i: (b, i, h // LANES)),
            pl.BlockSpec((1, 1, 1, s), lambda b, h, i: (b, h, 0, 0)),
        ],
        out_specs=pl.BlockSpec((1, tq, dh), lambda b, h, i: (b, i, h)),
        out_shape=jax.ShapeDtypeStruct((nb, s, d), BF16),
        scratch_shapes=[stat, stat, pltpu.VMEM((tq, dh), F32), stat],
        compiler_params=_params("parallel", "parallel", "parallel"),
        name="fox_attention",
    )(qkv, qkv, qkv, cum, ck)


def _out_ln_kernel(o_ref, w_ref, x_ref, mod_ref, g_ref, b_ref, out_ref, *, alpha, tn, nj, gate_row):
    j = pl.program_id(1)
    y = jnp.dot(o_ref[...], w_ref[...], preferred_element_type=F32)
    out_ref[:, pl.ds(pl.multiple_of(j * tn, tn), tn)] = y

    @pl.when(j == nj - 1)
    def _():
        gate = mod_ref[0][gate_row:gate_row + 1]
        r = alpha * x_ref[...] + (1.0 + gate) * out_ref[...]
        out_ref[...] = _layer_norm(r, g_ref[...], b_ref[...])


def _fox_out_layer(o2, w_o, layer, x2, mod, ln_g, ln_b, seq, alpha):
    t, d = x2.shape
    tm = _tile(seq, 512, PACKED_ROWS)
    tn = _tile(d, 1024, LANES)
    bps = seq // tm
    nj = d // tn
    return pl.pallas_call(
        functools.partial(_out_ln_kernel, alpha=alpha, tn=tn, nj=nj, gate_row=2),
        grid=(t // tm, nj),
        in_specs=[
            pl.BlockSpec((tm, d), lambda i, j: (i, 0)),
            pl.BlockSpec((None, d, tn), lambda i, j: (layer, 0, j)),
            pl.BlockSpec((tm, d), lambda i, j: (i, 0), **ONE_BUFFER),
            pl.BlockSpec((1, N_MOD, d), lambda i, j: (i // bps, 0, 0)),
            pl.BlockSpec((1, d), lambda i, j: (0, 0)),
            pl.BlockSpec((1, d), lambda i, j: (0, 0)),
        ],
        out_specs=pl.BlockSpec((tm, d), lambda i, j: (i, 0), **ONE_BUFFER),
        out_shape=jax.ShapeDtypeStruct((t, d), F32),
        compiler_params=_params("parallel", "arbitrary"),
        name="fox_out_layer",
    )(o2, w_o, x2, mod, ln_g.reshape(1, d), ln_b.reshape(1, d))


def _top_ranked(s, k, exact):
    rows = s.shape[0]
    row = lax.broadcasted_iota(jnp.int32, s.shape, 0)
    rank = jnp.full(s.shape, float(k), F32)
    vals = []
    for r in range(k):
        m = jnp.max(s, axis=0, keepdims=True)
        sel = s == m
        if exact:
            first = jnp.min(jnp.where(sel, row, rows), axis=0, keepdims=True)
            sel = row == first
        rank = jnp.where(sel, float(r), rank)
        vals.append(m)
        s = jnp.where(sel, -jnp.inf, s)
    return jnp.concatenate(vals, axis=0), rank


def _overfull(ranks, topk):
    fk = float(topk)
    counts = [jnp.sum((rk < fk).astype(F32), axis=0, keepdims=True) for rk in ranks]
    return jnp.max(functools.reduce(jnp.maximum, [jnp.abs(c - fk) for c in counts]))


def _key_stage(q, keys_ref, topk, exact):
    dhalf = q.shape[1] // 2
    out = []
    for p in range(2):
        qp = q[:, p * dhalf:(p + 1) * dhalf]
        s = lax.dot_general(keys_ref[p], qp, NT_DIMS, preferred_element_type=F32)
        out.append((s,) + _top_ranked(s, topk, exact))
    return out


def _pair_stage(s1, tv1, rank1, tv2, n1_ref, e1_ref, hd, topk, exact):
    half = topk // 2
    sub = lax.broadcasted_iota(jnp.int32, (8, 1), 0)
    pieces = [tv1[0:1] + tv2]
    for a in range(1, half):
        pieces.append(jnp.where(sub < topk // (a + 1), tv1[a:a + 1] + tv2[0:8], -jnp.inf))
    pieces.append(tv1[half:topk] + tv2[0:1])
    cand = jnp.concatenate(pieces, axis=0)
    cvals, crank = _top_ranked(cand, topk, exact)
    taken = crank < float(topk)
    z = jnp.sum(jnp.where(taken, jnp.exp(cand - cvals[0:1]), 0.0), axis=0, keepdims=True)
    takenf = taken.astype(F32)

    n1 = jnp.zeros_like(rank1)
    start = 0
    for a in range(half):
        size = topk if a == 0 else 8
        n_a = jnp.sum(takenf[start:start + size], axis=0, keepdims=True)
        n1 = jnp.where(rank1 == float(a), n_a, n1)
        start += size
    for a in range(half, topk):
        n1 = jnp.where(rank1 == float(a), takenf[start + a - half:start + a - half + 1], n1)

    n1_ref[:, pl.ds(hd, 1), :] = n1[:, None, :]
    e1_ref[:, pl.ds(hd, 1), :] = (SQRT_HALF * jnp.exp(s1 - tv1[0:1]) / z)[:, None, :]
    return crank


def _router_kernel(x_ref, mod_ref, wq_ref, keys_ref, *rest, topk, cast_weights):
    if cast_weights:
        uw_ref, vw_ref, hbf_ref, n1_ref, e1_ref, r2_ref, e2_ref, u16_ref, v16_ref = rest
        u16_ref[...] = (uw_ref[...] * SQRT_HALF).astype(BF16)
        v16_ref[...] = vw_ref[...].astype(BF16)
    else:
        hbf_ref, n1_ref, e1_ref, r2_ref, e2_ref = rest
    hd = pl.program_id(1)

    @pl.when(hd == 0)
    def _():
        m = mod_ref[0]
        hbf_ref[...] = (x_ref[...] * (1.0 + m[4:5]) + m[3:4]).astype(BF16)

    kh = hbf_ref.shape[1] // 2
    q = (jnp.dot(hbf_ref[:, :kh], wq_ref[:kh, :], preferred_element_type=F32)
         + jnp.dot(hbf_ref[:, kh:], wq_ref[kh:, :], preferred_element_type=F32)).astype(BF16)

    def write_second_half(s2, tv2, rank2):
        r2_ref[0] = rank2
        e2_ref[0] = jnp.exp(s2 - tv2[0:1])

    (s1, tv1, rank1), (s2, tv2, rank2) = _key_stage(q, keys_ref, topk, exact=False)
    key_tie = _overfull([rank1, rank2], topk)
    crank = _pair_stage(s1, tv1, rank1, tv2, n1_ref, e1_ref, hd, topk, exact=False)
    pair_tie = _overfull([crank], topk)
    write_second_half(s2, tv2, rank2)

    @pl.when(key_tie > 0.5)
    def _():
        (s1x, tv1x, rank1x), (s2x, tv2x, rank2x) = _key_stage(q, keys_ref, topk, exact=True)
        _pair_stage(s1x, tv1x, rank1x, tv2x, n1_ref, e1_ref, hd, topk, exact=True)
        write_second_half(s2x, tv2x, rank2x)

    @pl.when(jnp.logical_and(key_tie <= 0.5, pair_tie > 0.5))
    def _():
        _pair_stage(s1, tv1, rank1, tv2, n1_ref, e1_ref, hd, topk, exact=True)


def _peer_router(x2, mod, w_q, keys, layer, seq, weights=None):
    t, d = x2.shape
    _, _, nkeys, dhalf = keys.shape
    ph = w_q.shape[2] // (2 * dhalf)
    tr = _tile(seq, 512, LANES)
    bps = seq // tr
    route = jax.ShapeDtypeStruct((ph, nkeys, t), F32)
    route_spec = pl.BlockSpec((1, nkeys, tr), lambda i, h: (h, 0, i))
    key_major = jax.ShapeDtypeStruct((nkeys, ph, t), F32)
    key_spec = pl.BlockSpec((nkeys, ph, tr), lambda i, h: (0, 0, i))
    in_specs = [
        pl.BlockSpec((tr, d), lambda i, h: (i, 0)),
        pl.BlockSpec((1, N_MOD, d), lambda i, h: (i // bps, 0, 0)),
        pl.BlockSpec((None, d, 2 * dhalf), lambda i, h: (layer, 0, h)),
        pl.BlockSpec((None, 2, nkeys, dhalf), lambda i, h: (layer, 0, 0, 0)),
    ]
    operands = [x2, mod, w_q, keys]
    out_specs = [pl.BlockSpec((tr, d), lambda i, h: (i, 0)), key_spec, key_spec, route_spec, route_spec]
    out_shape = [jax.ShapeDtypeStruct((t, d), BF16), key_major, key_major, route, route]
    if weights is not None:
        ne = weights[0].shape[1]
        steps = (t // tr) * ph
        rows = ne // steps
        assert rows * steps == ne and rows % PACKED_ROWS == 0, (ne, steps)
        in_specs += [pl.BlockSpec((None, rows, d), lambda i, h: (layer, i * ph + h, 0))] * 2
        operands += list(weights)
        out_specs += [pl.BlockSpec((rows, d), lambda i, h: (i * ph + h, 0))] * 2
        out_shape += [jax.ShapeDtypeStruct((ne, d), BF16)] * 2
    return pl.pallas_call(
        functools.partial(_router_kernel, topk=PEER_TOPK, cast_weights=weights is not None),
        grid=(t // tr, ph),
        in_specs=in_specs,
        out_specs=out_specs,
        out_shape=out_shape,
        compiler_params=_params("parallel", "arbitrary"),
        name="peer_router",
    )(*operands)


def _gate_block(a_scr, wt_scr, n1_ref, e1_ref, r2_ref, e2_ref, key0, nkeys, ph):
    ts, tm = a_scr.shape
    for il in range(ts // nkeys):
        i_key = key0 + il
        for lc in range(tm // LANES):
            cols = slice(lc * LANES, (lc + 1) * LANES)
            n1t = n1_ref[i_key, :, cols]
            e1t = e1_ref[i_key, :, cols]
            n1b = [n1t[hd:hd + 1, :] for hd in range(ph)]
            e1b = [e1t[hd:hd + 1, :] for hd in range(ph)]
            tiles = []
            for rc in range(nkeys // SUBLANES):
                rows = slice(rc * SUBLANES, (rc + 1) * SUBLANES)
                w = None
                for hd in range(ph):
                    t = jnp.where(r2_ref[hd, rows, cols] < n1b[hd], e2_ref[hd, rows, cols], 0.0) * e1b[hd]
                    w = t if w is None else w + t
                b = a_scr[il * nkeys + rc * SUBLANES:il * nkeys + (rc + 1) * SUBLANES, cols]
                tiles.append(w * (b * (1.0 + lax.erf(b))))
            blk = jnp.concatenate(tiles, axis=0).astype(BF16)
            wt_scr[cols, il * nkeys:(il + 1) * nkeys] = blk.T


def _expert_kernel(h_ref, u_ref, v_ref, n1_ref, e1_ref, r2_ref, e2_ref, x_ref, mod_ref, g_ref, b_ref, *rest,
                   alpha, nkeys, ph, nj, ts, cast_next):
    if cast_next:
        un_ref, vn_ref, o_ref, un16_ref, vn16_ref, a_e, a_o, wt_e, wt_o = rest
        un16_ref[...] = (un_ref[...] * SQRT_HALF).astype(BF16)
        vn16_ref[...] = vn_ref[...].astype(BF16)
    else:
        o_ref, a_e, a_o, wt_e, wt_o = rest
    j = pl.program_id(1)
    tm, d = o_ref.shape
    gps = ts // nkeys

    @pl.when(j == 0)
    def _():
        o_ref[...] = jnp.zeros_like(o_ref)
        wt_e[...] = jnp.zeros_like(wt_e)
        a_o[...] = jnp.zeros_like(a_o)

    def mm1(rows, a_scr):
        half = tm // 2
        for s in range(2):
            a_scr[:, s * half:(s + 1) * half] = lax.dot_general(
                u_ref[rows, :], h_ref[s * half:(s + 1) * half, :], NT_DIMS, preferred_element_type=F32)

    def mm2(wt_scr, rows):
        half = d // 2
        for s in range(2):
            o_ref[:, s * half:(s + 1) * half] += jnp.dot(
                wt_scr[...], v_ref[rows, s * half:(s + 1) * half], preferred_element_type=F32)

    lo, hi = slice(0, ts), slice(ts, 2 * ts)
    last = 2 * nj - 1
    mm2(wt_e, lo)
    _gate_block(a_o, wt_o, n1_ref, e1_ref, r2_ref, e2_ref, jnp.maximum(2 * j - 1, 0) * gps, nkeys, ph)
    mm1(lo, a_e)
    mm2(wt_o, hi)
    _gate_block(a_e, wt_e, n1_ref, e1_ref, r2_ref, e2_ref, jnp.minimum(2 * j, last) * gps, nkeys, ph)
    mm1(hi, a_o)

    @pl.when(j == nj)
    def _():
        _residual_layer_norm_rows(x_ref, o_ref, mod_ref[0][5:6], g_ref[...], b_ref[...], alpha)


def _peer_experts(hbf, u, v, n1t, e1t, r2, e2, x2, mod, ln_g, ln_b, seq, alpha, nxt=None):
    t, d = x2.shape
    ne = u.shape[0]
    ph, nkeys, _ = r2.shape
    tm = _tile(seq, 512, 2 * LANES)
    ts = _tile(ne // 2, 256, nkeys)
    bps = seq // tm
    nj = ne // (2 * ts)
    route_spec = pl.BlockSpec((ph, nkeys, tm), lambda i, j: (0, 0, i), **ONE_BUFFER)
    key_spec = pl.BlockSpec((nkeys, ph, tm), lambda i, j: (0, 0, i), **ONE_BUFFER)
    in_specs = [
        pl.BlockSpec((tm, d), lambda i, j: (i, 0)),
        pl.BlockSpec((2 * ts, d), lambda i, j: (jnp.minimum(j, nj - 1), 0)),
        pl.BlockSpec((2 * ts, d), lambda i, j: (jnp.maximum(j - 1, 0), 0)),
        key_spec, key_spec, route_spec, route_spec,
        pl.BlockSpec((tm, d), lambda i, j: (i, 0), **ONE_BUFFER),
        pl.BlockSpec((1, N_MOD, d), lambda i, j: (i // bps, 0, 0)),
        pl.BlockSpec((1, d), lambda i, j: (0, 0)),
        pl.BlockSpec((1, d), lambda i, j: (0, 0)),
    ]
    operands = [hbf, u, v, n1t, e1t, r2, e2, x2, mod, ln_g.reshape(1, d), ln_b.reshape(1, d)]
    out_specs = [pl.BlockSpec((tm, d), lambda i, j: (i, 0), **ONE_BUFFER)]
    out_shape = [jax.ShapeDtypeStruct((t, d), F32)]
    if nxt is not None:
        u_all, v_all, layer = nxt
        steps = (t // tm) * (nj + 1)
        per_step = -(-ne // steps)
        rows = -(-per_step // PACKED_ROWS) * PACKED_ROWS
        assert ne % rows == 0, (ne, rows)
        last = ne // rows - 1

        def chunk(i, j):
            return jnp.minimum(i * (nj + 1) + j, last)

        in_specs += [pl.BlockSpec((None, rows, d), lambda i, j: (layer, chunk(i, j), 0))] * 2
        operands += [u_all, v_all]
        out_specs += [pl.BlockSpec((rows, d), lambda i, j: (chunk(i, j), 0))] * 2
        out_shape += [jax.ShapeDtypeStruct((ne, d), BF16)] * 2
    return pl.pallas_call(
        functools.partial(_expert_kernel, alpha=alpha, nkeys=nkeys, ph=ph, nj=nj, ts=ts, cast_next=nxt is not None),
        grid=(t // tm, nj + 1),
        in_specs=in_specs,
        out_specs=out_specs,
        out_shape=out_shape,
        scratch_shapes=[pltpu.VMEM((ts, tm), F32), pltpu.VMEM((ts, tm), F32),
                        pltpu.VMEM((tm, ts), BF16), pltpu.VMEM((tm, ts), BF16)],
        compiler_params=_params("parallel", "arbitrary"),
        name="peer_experts",
    )(*operands)


def kernel(x, c, w_c, b_c, ada_table, ln_tok_g, ln_tok_b, ln_ch_g, ln_ch_b, pool_w, pool_scale,
           fox_w_in, fox_b_f, fox_w_o, peer_w_q, peer_keys, peer_u, peer_v):
    nb, seq, d = x.shape
    depth = ada_table.shape[0]
    alpha = float((2 * depth) ** 0.25)
    nheads = fox_b_f.shape[1]
    dh = d // nheads
    hpad = -(-nheads // LANES) * LANES

    mod_all = _adaln(c, w_c, b_c, ada_table)

    pool_w16 = pool_w.astype(BF16)
    w_in_t = jnp.swapaxes(fox_w_in, 1, 2)
    w_qkv16 = _cast_rows(w_in_t, 3 * d)
    b_f = jnp.pad(fox_b_f, ((0, 0), (0, hpad - nheads)))
    w_o16 = fox_w_o.astype(BF16)
    w_q16 = peer_w_q.astype(BF16)
    keys16 = peer_keys.astype(BF16)
    u16 = v16 = None

    n_mixers = 2
    for i in range(depth):
        mod = mod_all[:, i].reshape(nb, N_MOD, d)
        jl = i // n_mixers
        if i % n_mixers == 0:
            x = _pool_layer(x, mod, pool_w16, jl, pool_scale[jl], ln_tok_g[i], ln_tok_b[i], alpha)
        else:
            x2 = x.reshape(nb * seq, d)
            qkv, f = _fox_proj(x2, mod, w_qkv16, w_in_t, jl, nheads, seq, float(dh) ** -0.5)
            cum = _decay_cumsum(f.reshape(nb, seq, hpad), b_f[jl].reshape(1, hpad))
            ck = cum[:, :, :nheads].transpose(0, 2, 1)[:, :, None, :]
            o = _fox_attention(qkv.reshape(nb, seq, 3 * d), cum, ck, nheads, dh)
            x = _fox_out_layer(o.reshape(nb * seq, d), w_o16, jl, x2, mod, ln_tok_g[i], ln_tok_b[i], seq, alpha)
            x = x.reshape(nb, seq, d)
        x2 = x.reshape(nb * seq, d)
        hbf, n1t, e1t, r2, e2, *first = _peer_router(x2, mod, w_q16, keys16, i, seq, (peer_u, peer_v) if i == 0 else None)
        if first:
            u16, v16 = first
        nxt = (peer_u, peer_v, i + 1) if i + 1 < depth else None
        x, *converted = _peer_experts(hbf, u16, v16, n1t, e1t, r2, e2, x2, mod, ln_ch_g[i], ln_ch_b[i], seq, alpha, nxt)
        if converted:
            u16, v16 = converted
        x = x.reshape(nb, seq, d)
    return x
```

```python
import functools

import jax
import jax.numpy as jnp
from jax import lax
from jax.experimental import pallas as pl
from jax.experimental.pallas import tpu as pltpu

F32 = jnp.float32
BF16 = jnp.bfloat16

LN_EPS = 1e-5
N_MOD = 6
POOL_WINDOWS = (2, 4, 8, 16)
POOL_HALO = 16
PEER_TOPK = 16
SQRT_HALF = 0.7071067811865476
LANES = 128
SUBLANES = 8
PACKED_ROWS = 16
LN_ROWS = 64
VMEM_LIMIT_BYTES = 56 * 1024 * 1024


NT_DIMS = (((1,), (1,)), ((), ()))
ONE_BUFFER = dict(pipeline_mode=pl.Buffered(1))


def _params(*semantics):
    return pltpu.CompilerParams(dimension_semantics=semantics, vmem_limit_bytes=VMEM_LIMIT_BYTES)


def _tile(dim, target, align):
    if dim <= target:
        return dim
    t = (target // align) * align
    while t > align and dim % t:
        t -= align
    assert dim % t == 0, (dim, target, align)
    return t


def _layer_norm(r, g, b):
    mu = jnp.mean(r, axis=-1, keepdims=True)
    xc = r - mu
    var = jnp.mean(xc * xc, axis=-1, keepdims=True)
    return xc * lax.rsqrt(var + LN_EPS) * g + b


def _residual_layer_norm_rows(x_ref, y_ref, gate, g, b, alpha):
    scale = 1.0 + gate

    def body(i, carry):
        rows = pl.ds(pl.multiple_of(i * LN_ROWS, LN_ROWS), LN_ROWS)
        y_ref[rows, :] = _layer_norm(alpha * x_ref[rows, :] + scale * y_ref[rows, :], g, b)
        return carry

    lax.fori_loop(0, y_ref.shape[0] // LN_ROWS, body, 0)


def _adaln_kernel(ct_ref, w_ref, b_ref, ada_ref, o_ref, acc_ref, *, nb, nk):
    k = pl.program_id(1)

    @pl.when(k == 0)
    def _():
        acc_ref[...] = jnp.zeros_like(acc_ref)

    s = jax.nn.silu(ct_ref[...])
    w = w_ref[...]
    tk, tn = w.shape
    for b in range(nb):
        prod = w * s[:, b:b + 1]
        acc_ref[b] += jnp.sum(prod.reshape(tk // SUBLANES, SUBLANES, tn), axis=0)

    @pl.when(k == nk - 1)
    def _():
        for b in range(nb):
            t0 = jnp.sum(acc_ref[b], axis=0, keepdims=True) + b_ref[...]
            o_ref[b] = t0 + ada_ref[...]


def _adaln(c, w_c, b_c, ada_table):
    nb, d = c.shape
    n = w_c.shape[1]
    depth = ada_table.shape[0]
    tk = _tile(d, 512, SUBLANES)
    tn = _tile(n, 2048, LANES)
    nk = d // tk
    return pl.pallas_call(
        functools.partial(_adaln_kernel, nb=nb, nk=nk),
        grid=(n // tn, nk),
        in_specs=[
            pl.BlockSpec((tk, nb), lambda j, k: (k, 0)),
            pl.BlockSpec((tk, tn), lambda j, k: (k, j)),
            pl.BlockSpec((1, tn), lambda j, k: (0, j)),
            pl.BlockSpec((depth, tn), lambda j, k: (0, j)),
        ],
        out_specs=pl.BlockSpec((nb, depth, tn), lambda j, k: (0, 0, j)),
        out_shape=jax.ShapeDtypeStruct((nb, depth, n), F32),
        scratch_shapes=[pltpu.VMEM((nb, SUBLANES, tn), F32)],
        compiler_params=_params("parallel", "arbitrary"),
        name="adaln",
    )(c.T, w_c, b_c.reshape(1, n), ada_table)


def _pool_kernel(x_ref, xp_ref, mod_ref, w_ref, scale_ref, g_ref, b_ref, o_ref, hh_ref, *, alpha, ts, dg):
    s_idx = pl.program_id(1)
    m = mod_ref[0]
    sh, sc, gate = m[0:1], m[1:2], m[2:3]
    x = x_ref[0]
    h = x * (1.0 + sc) + sh
    hp = xp_ref[0] * (1.0 + sc) + sh
    hh_ref[0:POOL_HALO] = jnp.where(s_idx > 0, hp, 0.0)
    hh_ref[POOL_HALO:] = h
    pos = lax.broadcasted_iota(jnp.int32, (ts, 1), 0) + s_idx * ts
    ys = []
    for gi, win in enumerate(POOL_WINDOWS):
        cols = slice(gi * dg, (gi + 1) * dg)
        acc = hh_ref[pl.ds(POOL_HALO, ts), cols]
        for k in range(1, win):
            acc = acc + hh_ref[pl.ds(POOL_HALO - k, ts), cols]
        cnt = jnp.minimum(pos + 1, win).astype(F32)
        z = acc / cnt - h[:, cols]
        ys.append(jnp.dot(z.astype(BF16), w_ref[gi], preferred_element_type=F32))
    y = jnp.concatenate(ys, axis=1) * scale_ref[...]
    r = alpha * x + (1.0 + gate) * y
    o_ref[0] = _layer_norm(r, g_ref[...], b_ref[...])


def _pool_layer(x, mod, w, layer, scale, ln_g, ln_b, alpha):
    nb, s, d = x.shape
    _, g, dg, _ = w.shape
    ts = _tile(s, 256, POOL_HALO)
    hb = ts // POOL_HALO
    return pl.pallas_call(
        functools.partial(_pool_kernel, alpha=alpha, ts=ts, dg=dg),
        grid=(nb, s // ts),
        in_specs=[
            pl.BlockSpec((1, ts, d), lambda b, i: (b, i, 0)),
            pl.BlockSpec((1, POOL_HALO, d), lambda b, i: (b, jnp.maximum(i * hb - 1, 0), 0)),
            pl.BlockSpec((1, N_MOD, d), lambda b, i: (b, 0, 0)),
            pl.BlockSpec((None, g, dg, dg), lambda b, i: (layer, 0, 0, 0)),
            pl.BlockSpec((1, d), lambda b, i: (0, 0)),
            pl.BlockSpec((1, d), lambda b, i: (0, 0)),
            pl.BlockSpec((1, d), lambda b, i: (0, 0)),
        ],
        out_specs=pl.BlockSpec((1, ts, d), lambda b, i: (b, i, 0)),
        out_shape=jax.ShapeDtypeStruct((nb, s, d), F32),
        scratch_shapes=[pltpu.VMEM((ts + POOL_HALO, d), F32)],
        compiler_params=_params("parallel", "parallel"),
        name="pool_layer",
    )(x, x, mod, w, scale.reshape(1, d), ln_g.reshape(1, d), ln_b.reshape(1, d))


def _cast_kernel(x_ref, o_ref):
    o_ref[...] = x_ref[...].astype(o_ref.dtype)


def _cast_rows(w, nrows):
    nl, _, cols = w.shape
    tk = _tile(nrows, 512, PACKED_ROWS)
    tn = _tile(cols, 2048, LANES)
    return pl.pallas_call(
        _cast_kernel,
        grid=(nl, nrows // tk, cols // tn),
        in_specs=[pl.BlockSpec((None, tk, tn), lambda l, i, j: (l, i, j))],
        out_specs=pl.BlockSpec((None, tk, tn), lambda l, i, j: (l, i, j)),
        out_shape=jax.ShapeDtypeStruct((nl, nrows, cols), BF16),
        compiler_params=_params("parallel", "parallel", "parallel"),
        name="cast_rows",
    )(w)


def _fox_proj_kernel(x_ref, mod_ref, w_ref, wf_ref, qkv_ref, f_ref, h_ref, *, nq_tiles, sm_scale, nheads):
    j = pl.program_id(1)

    @pl.when(j == 0)
    def _():
        m = mod_ref[0]
        h = x_ref[...] * (1.0 + m[1:2]) + m[0:1]
        h_hi = h.astype(BF16)
        h_ref[...] = h_hi
        row = lax.broadcasted_iota(jnp.int32, wf_ref.shape, 0)
        wf = jnp.where(row < nheads, wf_ref[...], 0.0)
        wf_hi = wf.astype(BF16)
        wf_lo = (wf - wf_hi.astype(F32)).astype(BF16)
        h_lo = (h - h_hi.astype(F32)).astype(BF16)
        f_ref[...] = (lax.dot_general(h_hi, wf_hi, NT_DIMS, preferred_element_type=F32)
                      + lax.dot_general(h_lo, wf_hi, NT_DIMS, preferred_element_type=F32)
                      + lax.dot_general(h_hi, wf_lo, NT_DIMS, preferred_element_type=F32))

    y = lax.dot_general(h_ref[...], w_ref[...], NT_DIMS, preferred_element_type=F32)
    y = y * jnp.where(j < nq_tiles, sm_scale, 1.0)
    qkv_ref[...] = y.astype(BF16)


def _fox_proj(x2, mod, w_qkv_t, w_in_t, layer, nheads, seq, sm_scale):
    t, d = x2.shape
    n = w_qkv_t.shape[1]
    assert n % LANES == 0 and nheads <= LANES, (n, nheads)
    hp = LANES
    tm = _tile(seq, 512, PACKED_ROWS)
    tn = _tile(d, 1024, LANES)
    bps = seq // tm
    return pl.pallas_call(
        functools.partial(_fox_proj_kernel, nq_tiles=d // tn, sm_scale=sm_scale, nheads=nheads),
        grid=(t // tm, n // tn),
        in_specs=[
            pl.BlockSpec((tm, d), lambda i, j: (i, 0)),
            pl.BlockSpec((1, N_MOD, d), lambda i, j: (i // bps, 0, 0)),
            pl.BlockSpec((None, tn, d), lambda i, j: (layer, j, 0)),
            pl.BlockSpec((None, hp, d), lambda i, j: (layer, n // hp, 0)),
        ],
        out_specs=[
            pl.BlockSpec((tm, tn), lambda i, j: (i, j)),
            pl.BlockSpec((tm, hp), lambda i, j: (i, 0)),
        ],
        out_shape=[jax.ShapeDtypeStruct((t, n), BF16), jax.ShapeDtypeStruct((t, hp), F32)],
        scratch_shapes=[pltpu.VMEM((tm, d), BF16)],
        compiler_params=_params("parallel", "arbitrary"),
        name="fox_proj",
    )(x2, mod, w_qkv_t, w_in_t)


def _split3(v):
    p1 = v.astype(BF16)
    r1 = v - p1.astype(F32)
    p2 = r1.astype(BF16)
    p3 = (r1 - p2.astype(F32)).astype(BF16)
    return p1, p2, p3


def _decay_kernel(f_ref, bf_ref, o_ref, *, chunk):
    s = f_ref.shape[1]
    row = lax.broadcasted_iota(jnp.int32, (chunk, chunk), 0)
    col = lax.broadcasted_iota(jnp.int32, (chunk, chunk), 1)
    tri = (col <= row).astype(BF16)
    carry = jnp.zeros((1, f_ref.shape[2]), F32)
    for c in range(s // chunk):
        rows = slice(c * chunk, (c + 1) * chunk)
        lf = jax.nn.log_sigmoid(f_ref[0, rows, :] + bf_ref[...])
        p1, p2, p3 = _split3(lf)
        cs = (jnp.dot(tri, p1, preferred_element_type=F32)
              + jnp.dot(tri, p2, preferred_element_type=F32)
              + jnp.dot(tri, p3, preferred_element_type=F32)) + carry
        o_ref[0, rows, :] = cs
        carry = cs[chunk - 1:chunk, :]


def _decay_cumsum(f, b_f):
    nb, s, hp = f.shape
    chunk = _tile(s, 256, PACKED_ROWS)
    return pl.pallas_call(
        functools.partial(_decay_kernel, chunk=chunk),
        grid=(nb,),
        in_specs=[pl.BlockSpec((1, s, hp), lambda b: (b, 0, 0)), pl.BlockSpec((1, hp), lambda b: (0, 0))],
        out_specs=pl.BlockSpec((1, s, hp), lambda b: (b, 0, 0)),
        out_shape=jax.ShapeDtypeStruct((nb, s, hp), F32),
        compiler_params=_params("parallel"),
        name="fox_decay",
    )(f, b_f)


def _attn_kernel(q_ref, k_ref, v_ref, cq_ref, ck_ref, o_ref, m_ref, l_ref, acc_ref, cqb_ref, *, tq):
    qi = pl.program_id(2)
    nch = tq // LANES
    m_ref[...] = jnp.full_like(m_ref, -jnp.inf)
    l_ref[...] = jnp.zeros_like(l_ref)
    acc_ref[...] = jnp.zeros_like(acc_ref)
    lane = lax.broadcasted_iota(jnp.int32, (tq, LANES), 1)
    mine = jnp.where(lane == pl.program_id(1) % LANES, cq_ref[0], 0.0)
    cqb_ref[...] = jnp.broadcast_to(jnp.sum(mine, axis=1, keepdims=True), cqb_ref.shape)

    def step(off, masked):
        kb = k_ref[0, pl.ds(off, tq), :]
        vb = v_ref[0, pl.ds(off, tq), :]
        ck = ck_ref[0, 0, :, pl.ds(off, tq)]
        s = lax.dot_general(q_ref[0], kb, NT_DIMS, preferred_element_type=F32)
        u = []
        for c in range(nch):
            uc = s[:, c * LANES:(c + 1) * LANES] - ck[:, c * LANES:(c + 1) * LANES]
            if masked:
                row = lax.broadcasted_iota(jnp.int32, (tq, LANES), 0)
                col = lax.broadcasted_iota(jnp.int32, (tq, LANES), 1) + c * LANES
                uc = jnp.where(col <= row, uc, -jnp.inf)
            u.append(uc)
        bm = functools.reduce(jnp.maximum, u)
        cqb = cqb_ref[...]
        m_old = m_ref[...]
        m_new = jnp.maximum(m_old, jnp.max(bm, axis=1, keepdims=True) + cqb)
        a = jnp.exp(m_old - m_new)
        r = cqb - m_new
        p = [jnp.exp(uc + r) for uc in u]
        l_ref[...] = a * l_ref[...] + functools.reduce(jnp.add, p)
        pb = jnp.concatenate([pc.astype(BF16) for pc in p], axis=1)
        acc_ref[...] = a * acc_ref[...] + jnp.dot(pb, vb, preferred_element_type=F32)
        m_ref[...] = m_new

    def body(ki, carry):
        step(pl.multiple_of(ki * tq, tq), False)
        return carry

    lax.fori_loop(0, qi, body, 0)
    step(pl.multiple_of(qi * tq, tq), True)
    l = jnp.sum(l_ref[...], axis=1, keepdims=True)
    o_ref[0] = (acc_ref[...] / l).astype(o_ref.dtype)


def _fox_attention(qkv, cum, ck, nheads, dh):
    nb, s, _ = qkv.shape
    d = nheads * dh
    assert dh == LANES, dh
    tq = _tile(s, 1024, LANES)
    stat = pltpu.VMEM((tq, LANES), F32)
    return pl.pallas_call(
        functools.partial(_attn_kernel, tq=tq),
        grid=(nb, nheads, s // tq),
        in_specs=[
            pl.BlockSpec((1, tq, dh), lambda b, h, i: (b, i, h)),
            pl.BlockSpec((1, s, dh), lambda b, h, i: (b, 0, nheads + h)),
            pl.BlockSpec((1, s, dh), lambda b, h, i: (b, 0, 2 * nheads + h)),
            pl.BlockSpec((1, tq, LANES), lambda b, h, i: (b, i, h // LANES)),
            pl.BlockSpec((1, 1, 1, s), lambda b, h, i: (b, h, 0, 0)),
        ],
        out_specs=pl.BlockSpec((1, tq, dh), lambda b, h, i: (b, i, h)),
        out_shape=jax.ShapeDtypeStruct((nb, s, d), BF16),
        scratch_shapes=[stat, stat, pltpu.VMEM((tq, dh), F32), stat],
        compiler_params=_params("parallel", "parallel", "parallel"),
        name="fox_attention",
    )(qkv, qkv, qkv, cum, ck)


def _out_ln_kernel(o_ref, w_ref, x_ref, mod_ref, g_ref, b_ref, out_ref, *, alpha, tn, nj, gate_row):
    j = pl.program_id(1)
    y = jnp.dot(o_ref[...], w_ref[...], preferred_element_type=F32)
    out_ref[:, pl.ds(pl.multiple_of(j * tn, tn), tn)] = y

    @pl.when(j == nj - 1)
    def _():
        gate = mod_ref[0][gate_row:gate_row + 1]
        r = alpha * x_ref[...] + (1.0 + gate) * out_ref[...]
        out_ref[...] = _layer_norm(r, g_ref[...], b_ref[...])


def _fox_out_layer(o2, w_o, layer, x2, mod, ln_g, ln_b, seq, alpha):
    t, d = x2.shape
    tm = _tile(seq, 512, PACKED_ROWS)
    tn = _tile(d, 1024, LANES)
    bps = seq // tm
    nj = d // tn
    return pl.pallas_call(
        functools.partial(_out_ln_kernel, alpha=alpha, tn=tn, nj=nj, gate_row=2),
        grid=(t // tm, nj),
        in_specs=[
            pl.BlockSpec((tm, d), lambda i, j: (i, 0)),
            pl.BlockSpec((None, d, tn), lambda i, j: (layer, 0, j)),
            pl.BlockSpec((tm, d), lambda i, j: (i, 0), **ONE_BUFFER),
            pl.BlockSpec((1, N_MOD, d), lambda i, j: (i // bps, 0, 0)),
            pl.BlockSpec((1, d), lambda i, j: (0, 0)),
            pl.BlockSpec((1, d), lambda i, j: (0, 0)),
        ],
        out_specs=pl.BlockSpec((tm, d), lambda i, j: (i, 0), **ONE_BUFFER),
        out_shape=jax.ShapeDtypeStruct((t, d), F32),
        compiler_params=_params("parallel", "arbitrary"),
        name="fox_out_layer",
    )(o2, w_o, x2, mod, ln_g.reshape(1, d), ln_b.reshape(1, d))


def _top_ranked(s, k, exact):
    rows = s.shape[0]
    row = lax.broadcasted_iota(jnp.int32, s.shape, 0)
    rank = jnp.full(s.shape, float(k), F32)
    vals = []
    for r in range(k):
        m = jnp.max(s, axis=0, keepdims=True)
        sel = s == m
        if exact:
            first = jnp.min(jnp.where(sel, row, rows), axis=0, keepdims=True)
            sel = row == first
        rank = jnp.where(sel, float(r), rank)
        vals.append(m)
        s = jnp.where(sel, -jnp.inf, s)
    return jnp.concatenate(vals, axis=0), rank


def _overfull(ranks, topk):
    fk = float(topk)
    counts = [jnp.sum((rk < fk).astype(F32), axis=0, keepdims=True) for rk in ranks]
    return jnp.max(functools.reduce(jnp.maximum, [jnp.abs(c - fk) for c in counts]))


def _key_stage(q, keys_ref, topk, exact):
    dhalf = q.shape[1] // 2
    out = []
    for p in range(2):
        qp = q[:, p * dhalf:(p + 1) * dhalf]
        s = lax.dot_general(keys_ref[p], qp, NT_DIMS, preferred_element_type=F32)
        out.append((s,) + _top_ranked(s, topk, exact))
    return out


def _pair_stage(s1, tv1, rank1, tv2, n1_ref, e1_ref, hd, topk, exact):
    half = topk // 2
    sub = lax.broadcasted_iota(jnp.int32, (8, 1), 0)
    pieces = [tv1[0:1] + tv2]
    for a in range(1, half):
        pieces.append(jnp.where(sub < topk // (a + 1), tv1[a:a + 1] + tv2[0:8], -jnp.inf))
    pieces.append(tv1[half:topk] + tv2[0:1])
    cand = jnp.concatenate(pieces, axis=0)
    cvals, crank = _top_ranked(cand, topk, exact)
    taken = crank < float(topk)
    z = jnp.sum(jnp.where(taken, jnp.exp(cand - cvals[0:1]), 0.0), axis=0, keepdims=True)
    takenf = taken.astype(F32)

    n1 = jnp.zeros_like(rank1)
    start = 0
    for a in range(half):
        size = topk if a == 0 else 8
        n_a = jnp.sum(takenf[start:start + size], axis=0, keepdims=True)
        n1 = jnp.where(rank1 == float(a), n_a, n1)
        start += size
    for a in range(half, topk):
        n1 = jnp.where(rank1 == float(a), takenf[start + a - half:start + a - half + 1], n1)

    n1_ref[:, pl.ds(hd, 1), :] = n1[:, None, :]
    e1_ref[:, pl.ds(hd, 1), :] = (SQRT_HALF * jnp.exp(s1 - tv1[0:1]) / z)[:, None, :]
    return crank


def _router_kernel(x_ref, mod_ref, wq_ref, keys_ref, *rest, topk, cast_weights):
    if cast_weights:
        uw_ref, vw_ref, hbf_ref, n1_ref, e1_ref, r2_ref, e2_ref, u16_ref, v16_ref = rest
        u16_ref[...] = (uw_ref[...] * SQRT_HALF).astype(BF16)
        v16_ref[...] = vw_ref[...].astype(BF16)
    else:
        hbf_ref, n1_ref, e1_ref, r2_ref, e2_ref = rest
    hd = pl.program_id(1)

    @pl.when(hd == 0)
    def _():
        m = mod_ref[0]
        hbf_ref[...] = (x_ref[...] * (1.0 + m[4:5]) + m[3:4]).astype(BF16)

    kh = hbf_ref.shape[1] // 2
    q = (jnp.dot(hbf_ref[:, :kh], wq_ref[:kh, :], preferred_element_type=F32)
         + jnp.dot(hbf_ref[:, kh:], wq_ref[kh:, :], preferred_element_type=F32)).astype(BF16)

    def write_second_half(s2, tv2, rank2):
        r2_ref[0] = rank2
        e2_ref[0] = jnp.exp(s2 - tv2[0:1])

    (s1, tv1, rank1), (s2, tv2, rank2) = _key_stage(q, keys_ref, topk, exact=False)
    key_tie = _overfull([rank1, rank2], topk)
    crank = _pair_stage(s1, tv1, rank1, tv2, n1_ref, e1_ref, hd, topk, exact=False)
    pair_tie = _overfull([crank], topk)
    write_second_half(s2, tv2, rank2)

    @pl.when(key_tie > 0.5)
    def _():
        (s1x, tv1x, rank1x), (s2x, tv2x, rank2x) = _key_stage(q, keys_ref, topk, exact=True)
        _pair_stage(s1x, tv1x, rank1x, tv2x, n1_ref, e1_ref, hd, topk, exact=True)
        write_second_half(s2x, tv2x, rank2x)

    @pl.when(jnp.logical_and(key_tie <= 0.5, pair_tie > 0.5))
    def _():
        _pair_stage(s1, tv1, rank1, tv2, n1_ref, e1_ref, hd, topk, exact=True)


def _peer_router(x2, mod, w_q, keys, layer, seq, weights=None):
    t, d = x2.shape
    _, _, nkeys, dhalf = keys.shape
    ph = w_q.shape[2] // (2 * dhalf)
    tr = _tile(seq, 512, LANES)
    bps = seq // tr
    route = jax.ShapeDtypeStruct((ph, nkeys, t), F32)
    route_spec = pl.BlockSpec((1, nkeys, tr), lambda i, h: (h, 0, i))
    key_major = jax.ShapeDtypeStruct((nkeys, ph, t), F32)
    key_spec = pl.BlockSpec((nkeys, ph, tr), lambda i, h: (0, 0, i))
    in_specs = [
        pl.BlockSpec((tr, d), lambda i, h: (i, 0)),
        pl.BlockSpec((1, N_MOD, d), lambda i, h: (i // bps, 0, 0)),
        pl.BlockSpec((None, d, 2 * dhalf), lambda i, h: (layer, 0, h)),
        pl.BlockSpec((None, 2, nkeys, dhalf), lambda i, h: (layer, 0, 0, 0)),
    ]
    operands = [x2, mod, w_q, keys]
    out_specs = [pl.BlockSpec((tr, d), lambda i, h: (i, 0)), key_spec, key_spec, route_spec, route_spec]
    out_shape = [jax.ShapeDtypeStruct((t, d), BF16), key_major, key_major, route, route]
    if weights is not None:
        ne = weights[0].shape[1]
        steps = (t // tr) * ph
        rows = ne // steps
        assert rows * steps == ne and rows % PACKED_ROWS == 0, (ne, steps)
        in_specs += [pl.BlockSpec((None, rows, d), lambda i, h: (layer, i * ph + h, 0))] * 2
        operands += list(weights)
        out_specs += [pl.BlockSpec((rows, d), lambda i, h: (i * ph + h, 0))] * 2
        out_shape += [jax.ShapeDtypeStruct((ne, d), BF16)] * 2
    return pl.pallas_call(
        functools.partial(_router_kernel, topk=PEER_TOPK, cast_weights=weights is not None),
        grid=(t // tr, ph),
        in_specs=in_specs,
        out_specs=out_specs,
        out_shape=out_shape,
        compiler_params=_params("parallel", "arbitrary"),
        name="peer_router",
    )(*operands)


def _gate_block(a_scr, wt_scr, n1_ref, e1_ref, r2_ref, e2_ref, key0, nkeys, ph):
    ts, tm = a_scr.shape
    for il in range(ts // nkeys):
        i_key = key0 + il
        for lc in range(tm // LANES):
            cols = slice(lc * LANES, (lc + 1) * LANES)
            n1t = n1_ref[i_key, :, cols]
            e1t = e1_ref[i_key, :, cols]
            n1b = [n1t[hd:hd + 1, :] for hd in range(ph)]
            e1b = [e1t[hd:hd + 1, :] for hd in range(ph)]
            tiles = []
            for rc in range(nkeys // SUBLANES):
                rows = slice(rc * SUBLANES, (rc + 1) * SUBLANES)
                w = None
                for hd in range(ph):
                    t = jnp.where(r2_ref[hd, rows, cols] < n1b[hd], e2_ref[hd, rows, cols], 0.0) * e1b[hd]
                    w = t if w is None else w + t
                b = a_scr[il * nkeys + rc * SUBLANES:il * nkeys + (rc + 1) * SUBLANES, cols]
                tiles.append(w * (b * (1.0 + lax.erf(b))))
            blk = jnp.concatenate(tiles, axis=0).astype(BF16)
            wt_scr[cols, il * nkeys:(il + 1) * nkeys] = blk.T


def _expert_kernel(h_ref, u_ref, v_ref, n1_ref, e1_ref, r2_ref, e2_ref, x_ref, mod_ref, g_ref, b_ref, *rest,
                   alpha, nkeys, ph, nj, ts, cast_next):
    if cast_next:
        un_ref, vn_ref, o_ref, un16_ref, vn16_ref, a_e, a_o, wt_e, wt_o = rest
        un16_ref[...] = (un_ref[...] * SQRT_HALF).astype(BF16)
        vn16_ref[...] = vn_ref[...].astype(BF16)
    else:
        o_ref, a_e, a_o, wt_e, wt_o = rest
    j = pl.program_id(1)
    tm, d = o_ref.shape
    gps = ts // nkeys

    @pl.when(j == 0)
    def _():
        o_ref[...] = jnp.zeros_like(o_ref)
        wt_e[...] = jnp.zeros_like(wt_e)
        a_o[...] = jnp.zeros_like(a_o)

    def mm1(rows, a_scr):
        half = tm // 2
        for s in range(2):
            a_scr[:, s * half:(s + 1) * half] = lax.dot_general(
                u_ref[rows, :], h_ref[s * half:(s + 1) * half, :], NT_DIMS, preferred_element_type=F32)

    def mm2(wt_scr, rows):
        half = d // 2
        for s in range(2):
            o_ref[:, s * half:(s + 1) * half] += jnp.dot(
                wt_scr[...], v_ref[rows, s * half:(s + 1) * half], preferred_element_type=F32)

    lo, hi = slice(0, ts), slice(ts, 2 * ts)
    last = 2 * nj - 1
    mm2(wt_e, lo)
    _gate_block(a_o, wt_o, n1_ref, e1_ref, r2_ref, e2_ref, jnp.maximum(2 * j - 1, 0) * gps, nkeys, ph)
    mm1(lo, a_e)
    mm2(wt_o, hi)
    _gate_block(a_e, wt_e, n1_ref, e1_ref, r2_ref, e2_ref, jnp.minimum(2 * j, last) * gps, nkeys, ph)
    mm1(hi, a_o)

    @pl.when(j == nj)
    def _():
        _residual_layer_norm_rows(x_ref, o_ref, mod_ref[0][5:6], g_ref[...], b_ref[...], alpha)


def _peer_experts(hbf, u, v, n1t, e1t, r2, e2, x2, mod, ln_g, ln_b, seq, alpha, nxt=None):
    t, d = x2.shape
    ne = u.shape[0]
    ph, nkeys, _ = r2.shape
    tm = _tile(seq, 512, 2 * LANES)
    ts = _tile(ne // 2, 256, nkeys)
    bps = seq // tm
    nj = ne // (2 * ts)
    route_spec = pl.BlockSpec((ph, nkeys, tm), lambda i, j: (0, 0, i), **ONE_BUFFER)
    key_spec = pl.BlockSpec((nkeys, ph, tm), lambda i, j: (0, 0, i), **ONE_BUFFER)
    in_specs = [
        pl.BlockSpec((tm, d), lambda i, j: (i, 0)),
        pl.BlockSpec((2 * ts, d), lambda i, j: (jnp.minimum(j, nj - 1), 0)),
        pl.BlockSpec((2 * ts, d), lambda i, j: (jnp.maximum(j - 1, 0), 0)),
        key_spec, key_spec, route_spec, route_spec,
        pl.BlockSpec((tm, d), lambda i, j: (i, 0), **ONE_BUFFER),
        pl.BlockSpec((1, N_MOD, d), lambda i, j: (i // bps, 0, 0)),
        pl.BlockSpec((1, d), lambda i, j: (0, 0)),
        pl.BlockSpec((1, d), lambda i, j: (0, 0)),
    ]
    operands = [hbf, u, v, n1t, e1t, r2, e2, x2, mod, ln_g.reshape(1, d), ln_b.reshape(1, d)]
    out_specs = [pl.BlockSpec((tm, d), lambda i, j: (i, 0), **ONE_BUFFER)]
    out_shape = [jax.ShapeDtypeStruct((t, d), F32)]
    if nxt is not None:
        u_all, v_all, layer = nxt
        steps = (t // tm) * (nj + 1)
        per_step = -(-ne // steps)
        rows = -(-per_step // PACKED_ROWS) * PACKED_ROWS
        assert ne % rows == 0, (ne, rows)
        last = ne // rows - 1

        def chunk(i, j):
            return jnp.minimum(i * (nj + 1) + j, last)

        in_specs += [pl.BlockSpec((None, rows, d), lambda i, j: (layer, chunk(i, j), 0))] * 2
        operands += [u_all, v_all]
        out_specs += [pl.BlockSpec((rows, d), lambda i, j: (chunk(i, j), 0))] * 2
        out_shape += [jax.ShapeDtypeStruct((ne, d), BF16)] * 2
    return pl.pallas_call(
        functools.partial(_expert_kernel, alpha=alpha, nkeys=nkeys, ph=ph, nj=nj, ts=ts, cast_next=nxt is not None),
        grid=(t // tm, nj + 1),
        in_specs=in_specs,
        out_specs=out_specs,
        out_shape=out_shape,
        scratch_shapes=[pltpu.VMEM((ts, tm), F32), pltpu.VMEM((ts, tm), F32),
                        pltpu.VMEM((tm, ts), BF16), pltpu.VMEM((tm, ts), BF16)],
        compiler_params=_params("parallel", "arbitrary"),
        name="peer_experts",
    )(*operands)


def kernel(x, c, w_c, b_c, ada_table, ln_tok_g, ln_tok_b, ln_ch_g, ln_ch_b, pool_w, pool_scale,
           fox_w_in, fox_b_f, fox_w_o, peer_w_q, peer_keys, peer_u, peer_v):
    nb, seq, d = x.shape
    depth = ada_table.shape[0]
    alpha = float((2 * depth) ** 0.25)
    nheads = fox_b_f.shape[1]
    dh = d // nheads
    hpad = -(-nheads // LANES) * LANES

    mod_all = _adaln(c, w_c, b_c, ada_table)

    pool_w16 = pool_w.astype(BF16)
    w_in_t = jnp.swapaxes(fox_w_in, 1, 2)
    w_qkv16 = _cast_rows(w_in_t, 3 * d)
    b_f = jnp.pad(fox_b_f, ((0, 0), (0, hpad - nheads)))
    w_o16 = fox_w_o.astype(BF16)
    w_q16 = peer_w_q.astype(BF16)
    keys16 = peer_keys.astype(BF16)
    u16 = v16 = None

    n_mixers = 2
    for i in range(depth):
        mod = mod_all[:, i].reshape(nb, N_MOD, d)
        jl = i // n_mixers
        if i % n_mixers == 0:
            x = _pool_layer(x, mod, pool_w16, jl, pool_scale[jl], ln_tok_g[i], ln_tok_b[i], alpha)
        else:
            x2 = x.reshape(nb * seq, d)
            qkv, f = _fox_proj(x2, mod, w_qkv16, w_in_t, jl, nheads, seq, float(dh) ** -0.5)
            cum = _decay_cumsum(f.reshape(nb, seq, hpad), b_f[jl].reshape(1, hpad))
            ck = cum[:, :, :nheads].transpose(0, 2, 1)[:, :, None, :]
            o = _fox_attention(qkv.reshape(nb, seq, 3 * d), cum, ck, nheads, dh)
            x = _fox_out_layer(o.reshape(nb * seq, d), w_o16, jl, x2, mod, ln_tok_g[i], ln_tok_b[i], seq, alpha)
            x = x.reshape(nb, seq, d)
        x2 = x.reshape(nb * seq, d)
        hbf, n1t, e1t, r2, e2, *first = _peer_router(x2, mod, w_q16, keys16, i, seq, (peer_u, peer_v) if i == 0 else None)
        if first:
            u16, v16 = first
        nxt = (peer_u, peer_v, i + 1) if i + 1 < depth else None
        x, *converted = _peer_experts(hbf, u16, v16, n1t, e1t, r2, e2, x2, mod, ln_ch_g[i], ln_ch_b[i], seq, alpha, nxt)
        if converted:
            u16, v16 = converted
        x = x.reshape(nb, seq, d)
    return x
```

```python
import functools

import jax
import jax.numpy as jnp
from jax import lax
from jax.experimental import pallas as pl
from jax.experimental.pallas import tpu as pltpu

F32 = jnp.float32
BF16 = jnp.bfloat16

LN_EPS = 1e-5
N_MOD = 6
POOL_WINDOWS = (2, 4, 8, 16)
POOL_HALO = 16
PEER_TOPK = 16
SQRT_HALF = 0.7071067811865476
LANES = 128
SUBLANES = 8
PACKED_ROWS = 16
LN_ROWS = 64
VMEM_LIMIT_BYTES = 56 * 1024 * 1024


NT_DIMS = (((1,), (1,)), ((), ()))
ONE_BUFFER = dict(pipeline_mode=pl.Buffered(1))


def _params(*semantics):
    return pltpu.CompilerParams(dimension_semantics=semantics, vmem_limit_bytes=VMEM_LIMIT_BYTES)


def _tile(dim, target, align):
    if dim <= target:
        return dim
    t = (target // align) * align
    while t > align and dim % t:
        t -= align
    assert dim % t == 0, (dim, target, align)
    return t


def _layer_norm(r, g, b):
    mu = jnp.mean(r, axis=-1, keepdims=True)
    xc = r - mu
    var = jnp.mean(xc * xc, axis=-1, keepdims=True)
    return xc * lax.rsqrt(var + LN_EPS) * g + b


def _residual_layer_norm_rows(x_ref, y_ref, gate, g, b, alpha):
    scale = 1.0 + gate

    def body(i, carry):
        rows = pl.ds(pl.multiple_of(i * LN_ROWS, LN_ROWS), LN_ROWS)
        y_ref[rows, :] = _layer_norm(alpha * x_ref[rows, :] + scale * y_ref[rows, :], g, b)
        return carry

    lax.fori_loop(0, y_ref.shape[0] // LN_ROWS, body, 0)


def _adaln_kernel(ct_ref, w_ref, b_ref, ada_ref, o_ref, acc_ref, *, nb, nk):
    k = pl.program_id(1)

    @pl.when(k == 0)
    def _():
        acc_ref[...] = jnp.zeros_like(acc_ref)

    s = jax.nn.silu(ct_ref[...])
    w = w_ref[...]
    tk, tn = w.shape
    for b in range(nb):
        prod = w * s[:, b:b + 1]
        acc_ref[b] += jnp.sum(prod.reshape(tk // SUBLANES, SUBLANES, tn), axis=0)

    @pl.when(k == nk - 1)
    def _():
        for b in range(nb):
            t0 = jnp.sum(acc_ref[b], axis=0, keepdims=True) + b_ref[...]
            o_ref[b] = t0 + ada_ref[...]


def _adaln(c, w_c, b_c, ada_table):
    nb, d = c.shape
    n = w_c.shape[1]
    depth = ada_table.shape[0]
    tk = _tile(d, 512, SUBLANES)
    tn = _tile(n, 2048, LANES)
    nk = d // tk
    return pl.pallas_call(
        functools.partial(_adaln_kernel, nb=nb, nk=nk),
        grid=(n // tn, nk),
        in_specs=[
            pl.BlockSpec((tk, nb), lambda j, k: (k, 0)),
            pl.BlockSpec((tk, tn), lambda j, k: (k, j)),
            pl.BlockSpec((1, tn), lambda j, k: (0, j)),
            pl.BlockSpec((depth, tn), lambda j, k: (0, j)),
        ],
        out_specs=pl.BlockSpec((nb, depth, tn), lambda j, k: (0, 0, j)),
        out_shape=jax.ShapeDtypeStruct((nb, depth, n), F32),
        scratch_shapes=[pltpu.VMEM((nb, SUBLANES, tn), F32)],
        compiler_params=_params("parallel", "arbitrary"),
        name="adaln",
    )(c.T, w_c, b_c.reshape(1, n), ada_table)


def _pool_kernel(x_ref, xp_ref, mod_ref, w_ref, scale_ref, g_ref, b_ref, o_ref, hh_ref, *, alpha, ts, dg):
    s_idx = pl.program_id(1)
    m = mod_ref[0]
    sh, sc, gate = m[0:1], m[1:2], m[2:3]
    x = x_ref[0]
    h = x * (1.0 + sc) + sh
    hp = xp_ref[0] * (1.0 + sc) + sh
    hh_ref[0:POOL_HALO] = jnp.where(s_idx > 0, hp, 0.0)
    hh_ref[POOL_HALO:] = h
    pos = lax.broadcasted_iota(jnp.int32, (ts, 1), 0) + s_idx * ts
    ys = []
    for gi, win in enumerate(POOL_WINDOWS):
        cols = slice(gi * dg, (gi + 1) * dg)
        acc = hh_ref[pl.ds(POOL_HALO, ts), cols]
        for k in range(1, win):
            acc = acc + hh_ref[pl.ds(POOL_HALO - k, ts), cols]
        cnt = jnp.minimum(pos + 1, win).astype(F32)
        z = acc / cnt - h[:, cols]
        ys.append(jnp.dot(z.astype(BF16), w_ref[gi], preferred_element_type=F32))
    y = jnp.concatenate(ys, axis=1) * scale_ref[...]
    r = alpha * x + (1.0 + gate) * y
    o_ref[0] = _layer_norm(r, g_ref[...], b_ref[...])


def _pool_layer(x, mod, w, layer, scale, ln_g, ln_b, alpha):
    nb, s, d = x.shape
    _, g, dg, _ = w.shape
    ts = _tile(s, 256, POOL_HALO)
    hb = ts // POOL_HALO
    return pl.pallas_call(
        functools.partial(_pool_kernel, alpha=alpha, ts=ts, dg=dg),
        grid=(nb, s // ts),
        in_specs=[
            pl.BlockSpec((1, ts, d), lambda b, i: (b, i, 0)),
            pl.BlockSpec((1, POOL_HALO, d), lambda b, i: (b, jnp.maximum(i * hb - 1, 0), 0)),
            pl.BlockSpec((1, N_MOD, d), lambda b, i: (b, 0, 0)),
            pl.BlockSpec((None, g, dg, dg), lambda b, i: (layer, 0, 0, 0)),
            pl.BlockSpec((1, d), lambda b, i: (0, 0)),
            pl.BlockSpec((1, d), lambda b, i: (0, 0)),
            pl.BlockSpec((1, d), lambda b, i: (0, 0)),
        ],
        out_specs=pl.BlockSpec((1, ts, d), lambda b, i: (b, i, 0)),
        out_shape=jax.ShapeDtypeStruct((nb, s, d), F32),
        scratch_shapes=[pltpu.VMEM((ts + POOL_HALO, d), F32)],
        compiler_params=_params("parallel", "parallel"),
        name="pool_layer",
    )(x, x, mod, w, scale.reshape(1, d), ln_g.reshape(1, d), ln_b.reshape(1, d))


def _cast_kernel(x_ref, o_ref):
    o_ref[...] = x_ref[...].astype(o_ref.dtype)


def _cast_rows(w, nrows):
    nl, _, cols = w.shape
    tk = _tile(nrows, 512, PACKED_ROWS)
    tn = _tile(cols, 2048, LANES)
    return pl.pallas_call(
        _cast_kernel,
        grid=(nl, nrows // tk, cols // tn),
        in_specs=[pl.BlockSpec((None, tk, tn), lambda l, i, j: (l, i, j))],
        out_specs=pl.BlockSpec((None, tk, tn), lambda l, i, j: (l, i, j)),
        out_shape=jax.ShapeDtypeStruct((nl, nrows, cols), BF16),
        compiler_params=_params("parallel", "parallel", "parallel"),
        name="cast_rows",
    )(w)


def _fox_proj_kernel(x_ref, mod_ref, w_ref, wf_ref, qkv_ref, f_ref, h_ref, *, nq_tiles, sm_scale, nheads):
    j = pl.program_id(1)

    @pl.when(j == 0)
    def _():
        m = mod_ref[0]
        h = x_ref[...] * (1.0 + m[1:2]) + m[0:1]
        h_hi = h.astype(BF16)
        h_ref[...] = h_hi
        row = lax.broadcasted_iota(jnp.int32, wf_ref.shape, 0)
        wf = jnp.where(row < nheads, wf_ref[...], 0.0)
        wf_hi = wf.astype(BF16)
        wf_lo = (wf - wf_hi.astype(F32)).astype(BF16)
        h_lo = (h - h_hi.astype(F32)).astype(BF16)
        f_ref[...] = (lax.dot_general(h_hi, wf_hi, NT_DIMS, preferred_element_type=F32)
                      + lax.dot_general(h_lo, wf_hi, NT_DIMS, preferred_element_type=F32)
                      + lax.dot_general(h_hi, wf_lo, NT_DIMS, preferred_element_type=F32))

    y = lax.dot_general(h_ref[...], w_ref[...], NT_DIMS, preferred_element_type=F32)
    y = y * jnp.where(j < nq_tiles, sm_scale, 1.0)
    qkv_ref[...] = y.astype(BF16)


def _fox_proj(x2, mod, w_qkv_t, w_in_t, layer, nheads, seq, sm_scale):
    t, d = x2.shape
    n = w_qkv_t.shape[1]
    assert n % LANES == 0 and nheads <= LANES, (n, nheads)
    hp = LANES
    tm = _tile(seq, 512, PACKED_ROWS)
    tn = _tile(d, 1024, LANES)
    bps = seq // tm
    return pl.pallas_call(
        functools.partial(_fox_proj_kernel, nq_tiles=d // tn, sm_scale=sm_scale, nheads=nheads),
        grid=(t // tm, n // tn),
        in_specs=[
            pl.BlockSpec((tm, d), lambda i, j: (i, 0)),
            pl.BlockSpec((1, N_MOD, d), lambda i, j: (i // bps, 0, 0)),
            pl.BlockSpec((None, tn, d), lambda i, j: (layer, j, 0)),
            pl.BlockSpec((None, hp, d), lambda i, j: (layer, n // hp, 0)),
        ],
        out_specs=[
            pl.BlockSpec((tm, tn), lambda i, j: (i, j)),
            pl.BlockSpec((tm, hp), lambda i, j: (i, 0)),
        ],
        out_shape=[jax.ShapeDtypeStruct((t, n), BF16), jax.ShapeDtypeStruct((t, hp), F32)],
        scratch_shapes=[pltpu.VMEM((tm, d), BF16)],
        compiler_params=_params("parallel", "arbitrary"),
        name="fox_proj",
    )(x2, mod, w_qkv_t, w_in_t)


def _split3(v):
    p1 = v.astype(BF16)
    r1 = v - p1.astype(F32)
    p2 = r1.astype(BF16)
    p3 = (r1 - p2.astype(F32)).astype(BF16)
    return p1, p2, p3


def _decay_kernel(f_ref, bf_ref, o_ref, *, chunk):
    s = f_ref.shape[1]
    row = lax.broadcasted_iota(jnp.int32, (chunk, chunk), 0)
    col = lax.broadcasted_iota(jnp.int32, (chunk, chunk), 1)
    tri = (col <= row).astype(BF16)
    carry = jnp.zeros((1, f_ref.shape[2]), F32)
    for c in range(s // chunk):
        rows = slice(c * chunk, (c + 1) * chunk)
        lf = jax.nn.log_sigmoid(f_ref[0, rows, :] + bf_ref[...])
        p1, p2, p3 = _split3(lf)
        cs = (jnp.dot(tri, p1, preferred_element_type=F32)
              + jnp.dot(tri, p2, preferred_element_type=F32)
              + jnp.dot(tri, p3, preferred_element_type=F32)) + carry
        o_ref[0, rows, :] = cs
        carry = cs[chunk - 1:chunk, :]


def _decay_cumsum(f, b_f):
    nb, s, hp = f.shape
    chunk = _tile(s, 256, PACKED_ROWS)
    return pl.pallas_call(
        functools.partial(_decay_kernel, chunk=chunk),
        grid=(nb,),
        in_specs=[pl.BlockSpec((1, s, hp), lambda b: (b, 0, 0)), pl.BlockSpec((1, hp), lambda b: (0, 0))],
        out_specs=pl.BlockSpec((1, s, hp), lambda b: (b, 0, 0)),
        out_shape=jax.ShapeDtypeStruct((nb, s, hp), F32),
        compiler_params=_params("parallel"),
        name="fox_decay",
    )(f, b_f)


def _attn_kernel(q_ref, k_ref, v_ref, cq_ref, ck_ref, o_ref, m_ref, l_ref, acc_ref, cqb_ref, *, tq):
    qi = pl.program_id(2)
    nch = tq // LANES
    m_ref[...] = jnp.full_like(m_ref, -jnp.inf)
    l_ref[...] = jnp.zeros_like(l_ref)
    acc_ref[...] = jnp.zeros_like(acc_ref)
    lane = lax.broadcasted_iota(jnp.int32, (tq, LANES), 1)
    mine = jnp.where(lane == pl.program_id(1) % LANES, cq_ref[0], 0.0)
    cqb_ref[...] = jnp.broadcast_to(jnp.sum(mine, axis=1, keepdims=True), cqb_ref.shape)

    def step(off, masked):
        kb = k_ref[0, pl.ds(off, tq), :]
        vb = v_ref[0, pl.ds(off, tq), :]
        ck = ck_ref[0, 0, :, pl.ds(off, tq)]
        s = lax.dot_general(q_ref[0], kb, NT_DIMS, preferred_element_type=F32)
        u = []
        for c in range(nch):
            uc = s[:, c * LANES:(c + 1) * LANES] - ck[:, c * LANES:(c + 1) * LANES]
            if masked:
                row = lax.broadcasted_iota(jnp.int32, (tq, LANES), 0)
                col = lax.broadcasted_iota(jnp.int32, (tq, LANES), 1) + c * LANES
                uc = jnp.where(col <= row, uc, -jnp.inf)
            u.append(uc)
        bm = functools.reduce(jnp.maximum, u)
        cqb = cqb_ref[...]
        m_old = m_ref[...]
        m_new = jnp.maximum(m_old, jnp.max(bm, axis=1, keepdims=True) + cqb)
        a = jnp.exp(m_old - m_new)
        r = cqb - m_new
        p = [jnp.exp(uc + r) for uc in u]
        l_ref[...] = a * l_ref[...] + functools.reduce(jnp.add, p)
        pb = jnp.concatenate([pc.astype(BF16) for pc in p], axis=1)
        acc_ref[...] = a * acc_ref[...] + jnp.dot(pb, vb, preferred_element_type=F32)
        m_ref[...] = m_new

    def body(ki, carry):
        step(pl.multiple_of(ki * tq, tq), False)
        return carry

    lax.fori_loop(0, qi, body, 0)
    step(pl.multiple_of(qi * tq, tq), True)
    l = jnp.sum(l_ref[...], axis=1, keepdims=True)
    o_ref[0] = (acc_ref[...] / l).astype(o_ref.dtype)


def _fox_attention(qkv, cum, ck, nheads, dh):
    nb, s, _ = qkv.shape
    d = nheads * dh
    assert dh == LANES, dh
    tq = _tile(s, 1024, LANES)
    stat = pltpu.VMEM((tq, LANES), F32)
    return pl.pallas_call(
        functools.partial(_attn_kernel, tq=tq),
        grid=(nb, nheads, s // tq),
        in_specs=[
            pl.BlockSpec((1, tq, dh), lambda b, h, i: (b, i, h)),
            pl.BlockSpec((1, s, dh), lambda b, h, i: (b, 0, nheads + h)),
            pl.BlockSpec((1, s, dh), lambda b, h, i: (b, 0, 2 * nheads + h)),
            pl.BlockSpec((1, tq, LANES), lambda b, h, i: (b, i, h // LANES)),
            pl.BlockSpec((1, 1, 1, s), lambda b, h, i: (b, h, 0, 0)),
        ],
        out_specs=pl.BlockSpec((1, tq, dh), lambda b, h, i: (b, i, h)),
        out_shape=jax.ShapeDtypeStruct((nb, s, d), BF16),
        scratch_shapes=[stat, stat, pltpu.VMEM((tq, dh), F32), stat],
        compiler_params=_params("parallel", "parallel", "parallel"),
        name="fox_attention",
    )(qkv, qkv, qkv, cum, ck)


def _out_ln_kernel(o_ref, w_ref, x_ref, mod_ref, g_ref, b_ref, out_ref, *, alpha, tn, nj, gate_row):
    j = pl.program_id(1)
    y = jnp.dot(o_ref[...], w_ref[...], preferred_element_type=F32)
    out_ref[:, pl.ds(pl.multiple_of(j * tn, tn), tn)] = y

    @pl.when(j == nj - 1)
    def _():
        _residual_layer_norm_rows(x_ref, out_ref, mod_ref[0][gate_row:gate_row + 1], g_ref[...], b_ref[...], alpha)


def _fox_out_layer(o2, w_o, layer, x2, mod, ln_g, ln_b, seq, alpha):
    t, d = x2.shape
    tm = _tile(seq, 512, PACKED_ROWS)
    tn = _tile(d, 1024, LANES)
    bps = seq // tm
    nj = d // tn
    return pl.pallas_call(
        functools.partial(_out_ln_kernel, alpha=alpha, tn=tn, nj=nj, gate_row=2),
        grid=(t // tm, nj),
        in_specs=[
            pl.BlockSpec((tm, d), lambda i, j: (i, 0)),
            pl.BlockSpec((None, d, tn), lambda i, j: (layer, 0, j)),
            pl.BlockSpec((tm, d), lambda i, j: (i, 0)),
            pl.BlockSpec((1, N_MOD, d), lambda i, j: (i // bps, 0, 0)),
            pl.BlockSpec((1, d), lambda i, j: (0, 0)),
            pl.BlockSpec((1, d), lambda i, j: (0, 0)),
        ],
        out_specs=pl.BlockSpec((tm, d), lambda i, j: (i, 0), **ONE_BUFFER),
        out_shape=jax.ShapeDtypeStruct((t, d), F32),
        compiler_params=_params("parallel", "arbitrary"),
        name="fox_out_layer",
    )(o2, w_o, x2, mod, ln_g.reshape(1, d), ln_b.reshape(1, d))


def _top_ranked(s, k, exact):
    rows = s.shape[0]
    row = lax.broadcasted_iota(jnp.int32, s.shape, 0)
    rank = jnp.full(s.shape, float(k), F32)
    vals = []
    for r in range(k):
        m = jnp.max(s, axis=0, keepdims=True)
        sel = s == m
        if exact:
            first = jnp.min(jnp.where(sel, row, rows), axis=0, keepdims=True)
            sel = row == first
        rank = jnp.where(sel, float(r), rank)
        vals.append(m)
        s = jnp.where(sel, -jnp.inf, s)
    return jnp.concatenate(vals, axis=0), rank


def _overfull(ranks, topk):
    fk = float(topk)
    counts = [jnp.sum((rk < fk).astype(F32), axis=0, keepdims=True) for rk in ranks]
    return jnp.max(functools.reduce(jnp.maximum, [jnp.abs(c - fk) for c in counts]))


def _key_stage(q, keys_ref, topk, exact):
    dhalf = q.shape[1] // 2
    out = []
    for p in range(2):
        qp = q[:, p * dhalf:(p + 1) * dhalf]
        s = lax.dot_general(keys_ref[p], qp, NT_DIMS, preferred_element_type=F32)
        out.append((s,) + _top_ranked(s, topk, exact))
    return out


def _pair_stage(s1, tv1, rank1, tv2, n1_ref, e1_ref, hd, topk, exact):
    half = topk // 2
    sub = lax.broadcasted_iota(jnp.int32, (8, 1), 0)
    pieces = [tv1[0:1] + tv2]
    for a in range(1, half):
        pieces.append(jnp.where(sub < topk // (a + 1), tv1[a:a + 1] + tv2[0:8], -jnp.inf))
    pieces.append(tv1[half:topk] + tv2[0:1])
    cand = jnp.concatenate(pieces, axis=0)
    cvals, crank = _top_ranked(cand, topk, exact)
    taken = crank < float(topk)
    z = jnp.sum(jnp.where(taken, jnp.exp(cand - cvals[0:1]), 0.0), axis=0, keepdims=True)
    takenf = taken.astype(F32)

    n1 = jnp.zeros_like(rank1)
    start = 0
    for a in range(half):
        size = topk if a == 0 else 8
        n_a = jnp.sum(takenf[start:start + size], axis=0, keepdims=True)
        n1 = jnp.where(rank1 == float(a), n_a, n1)
        start += size
    for a in range(half, topk):
        n1 = jnp.where(rank1 == float(a), takenf[start + a - half:start + a - half + 1], n1)

    n1_ref[:, pl.ds(hd, 1), :] = n1[:, None, :]
    e1_ref[:, pl.ds(hd, 1), :] = (SQRT_HALF * jnp.exp(s1 - tv1[0:1]) / z)[:, None, :]
    return crank


def _router_kernel(x_ref, mod_ref, wq_ref, keys_ref, *rest, topk, cast_weights):
    if cast_weights:
        uw_ref, vw_ref, hbf_ref, n1_ref, e1_ref, r2_ref, e2_ref, u16_ref, v16_ref = rest
        u16_ref[...] = (uw_ref[...] * SQRT_HALF).astype(BF16)
        v16_ref[...] = vw_ref[...].astype(BF16)
    else:
        hbf_ref, n1_ref, e1_ref, r2_ref, e2_ref = rest
    hd = pl.program_id(1)

    @pl.when(hd == 0)
    def _():
        m = mod_ref[0]
        hbf_ref[...] = (x_ref[...] * (1.0 + m[4:5]) + m[3:4]).astype(BF16)

    kh = hbf_ref.shape[1] // 2
    q = (jnp.dot(hbf_ref[:, :kh], wq_ref[:kh, :], preferred_element_type=F32)
         + jnp.dot(hbf_ref[:, kh:], wq_ref[kh:, :], preferred_element_type=F32)).astype(BF16)

    def write_second_half(s2, tv2, rank2):
        r2_ref[0] = rank2
        e2_ref[0] = jnp.exp(s2 - tv2[0:1])

    (s1, tv1, rank1), (s2, tv2, rank2) = _key_stage(q, keys_ref, topk, exact=False)
    key_tie = _overfull([rank1, rank2], topk)
    crank = _pair_stage(s1, tv1, rank1, tv2, n1_ref, e1_ref, hd, topk, exact=False)
    pair_tie = _overfull([crank], topk)
    write_second_half(s2, tv2, rank2)

    @pl.when(key_tie > 0.5)
    def _():
        (s1x, tv1x, rank1x), (s2x, tv2x, rank2x) = _key_stage(q, keys_ref, topk, exact=True)
        _pair_stage(s1x, tv1x, rank1x, tv2x, n1_ref, e1_ref, hd, topk, exact=True)
        write_second_half(s2x, tv2x, rank2x)

    @pl.when(jnp.logical_and(key_tie <= 0.5, pair_tie > 0.5))
    def _():
        _pair_stage(s1, tv1, rank1, tv2, n1_ref, e1_ref, hd, topk, exact=True)


def _peer_router(x2, mod, w_q, keys, layer, seq, weights=None):
    t, d = x2.shape
    _, _, nkeys, dhalf = keys.shape
    ph = w_q.shape[2] // (2 * dhalf)
    tr = _tile(seq, 512, LANES)
    bps = seq // tr
    route = jax.ShapeDtypeStruct((ph, nkeys, t), F32)
    route_spec = pl.BlockSpec((1, nkeys, tr), lambda i, h: (h, 0, i))
    key_major = jax.ShapeDtypeStruct((nkeys, ph, t), F32)
    key_spec = pl.BlockSpec((nkeys, ph, tr), lambda i, h: (0, 0, i))
    in_specs = [
        pl.BlockSpec((tr, d), lambda i, h: (i, 0)),
        pl.BlockSpec((1, N_MOD, d), lambda i, h: (i // bps, 0, 0)),
        pl.BlockSpec((None, d, 2 * dhalf), lambda i, h: (layer, 0, h)),
        pl.BlockSpec((None, 2, nkeys, dhalf), lambda i, h: (layer, 0, 0, 0)),
    ]
    operands = [x2, mod, w_q, keys]
    out_specs = [pl.BlockSpec((tr, d), lambda i, h: (i, 0)), key_spec, key_spec, route_spec, route_spec]
    out_shape = [jax.ShapeDtypeStruct((t, d), BF16), key_major, key_major, route, route]
    if weights is not None:
        ne = weights[0].shape[1]
        steps = (t // tr) * ph
        rows = ne // steps
        assert rows * steps == ne and rows % PACKED_ROWS == 0, (ne, steps)
        in_specs += [pl.BlockSpec((None, rows, d), lambda i, h: (layer, i * ph + h, 0))] * 2
        operands += list(weights)
        out_specs += [pl.BlockSpec((rows, d), lambda i, h: (i * ph + h, 0))] * 2
        out_shape += [jax.ShapeDtypeStruct((ne, d), BF16)] * 2
    return pl.pallas_call(
        functools.partial(_router_kernel, topk=PEER_TOPK, cast_weights=weights is not None),
        grid=(t // tr, ph),
        in_specs=in_specs,
        out_specs=out_specs,
        out_shape=out_shape,
        compiler_params=_params("parallel", "arbitrary"),
        name="peer_router",
    )(*operands)


def _gate_block(a_scr, wt_scr, n1_ref, e1_ref, r2_ref, e2_ref, key0, nkeys, ph):
    ts, tm = a_scr.shape
    for il in range(ts // nkeys):
        i_key = key0 + il
        for lc in range(tm // LANES):
            cols = slice(lc * LANES, (lc + 1) * LANES)
            n1t = n1_ref[i_key, :, cols]
            e1t = e1_ref[i_key, :, cols]
            n1b = [n1t[hd:hd + 1, :] for hd in range(ph)]
            e1b = [e1t[hd:hd + 1, :] for hd in range(ph)]
            tiles = []
            for rc in range(nkeys // SUBLANES):
                rows = slice(rc * SUBLANES, (rc + 1) * SUBLANES)
                w = None
                for hd in range(ph):
                    t = jnp.where(r2_ref[hd, rows, cols] < n1b[hd], e2_ref[hd, rows, cols], 0.0) * e1b[hd]
                    w = t if w is None else w + t
                b = a_scr[il * nkeys + rc * SUBLANES:il * nkeys + (rc + 1) * SUBLANES, cols]
                tiles.append(w * (b * (1.0 + lax.erf(b))))
            blk = jnp.concatenate(tiles, axis=0).astype(BF16)
            wt_scr[cols, il * nkeys:(il + 1) * nkeys] = blk.T


def _expert_kernel(h_ref, u_ref, v_ref, n1_ref, e1_ref, r2_ref, e2_ref, x_ref, mod_ref, g_ref, b_ref, *rest,
                   alpha, nkeys, ph, nj, ts, cast_next):
    if cast_next:
        un_ref, vn_ref, o_ref, un16_ref, vn16_ref, a_e, a_o, wt_e, wt_o = rest
        un16_ref[...] = (un_ref[...] * SQRT_HALF).astype(BF16)
        vn16_ref[...] = vn_ref[...].astype(BF16)
    else:
        o_ref, a_e, a_o, wt_e, wt_o = rest
    j = pl.program_id(1)
    tm, d = o_ref.shape
    gps = ts // nkeys

    @pl.when(j == 0)
    def _():
        o_ref[...] = jnp.zeros_like(o_ref)
        wt_e[...] = jnp.zeros_like(wt_e)
        a_o[...] = jnp.zeros_like(a_o)

    def mm1(rows, a_scr):
        half = tm // 2
        for s in range(2):
            a_scr[:, s * half:(s + 1) * half] = lax.dot_general(
                u_ref[rows, :], h_ref[s * half:(s + 1) * half, :], NT_DIMS, preferred_element_type=F32)

    def mm2(wt_scr, rows):
        half = d // 2
        for s in range(2):
            o_ref[:, s * half:(s + 1) * half] += jnp.dot(
                wt_scr[...], v_ref[rows, s * half:(s + 1) * half], preferred_element_type=F32)

    lo, hi = slice(0, ts), slice(ts, 2 * ts)
    last = 2 * nj - 1
    mm2(wt_e, lo)
    _gate_block(a_o, wt_o, n1_ref, e1_ref, r2_ref, e2_ref, jnp.maximum(2 * j - 1, 0) * gps, nkeys, ph)
    mm1(lo, a_e)
    mm2(wt_o, hi)
    _gate_block(a_e, wt_e, n1_ref, e1_ref, r2_ref, e2_ref, jnp.minimum(2 * j, last) * gps, nkeys, ph)
    mm1(hi, a_o)

    @pl.when(j == nj)
    def _():
        _residual_layer_norm_rows(x_ref, o_ref, mod_ref[0][5:6], g_ref[...], b_ref[...], alpha)


def _peer_experts(hbf, u, v, n1t, e1t, r2, e2, x2, mod, ln_g, ln_b, seq, alpha, nxt=None):
    t, d = x2.shape
    ne = u.shape[0]
    ph, nkeys, _ = r2.shape
    tm = _tile(seq, 512, 2 * LANES)
    ts = _tile(ne // 2, 256, nkeys)
    bps = seq // tm
    nj = ne // (2 * ts)
    route_spec = pl.BlockSpec((ph, nkeys, tm), lambda i, j: (0, 0, i), **ONE_BUFFER)
    key_spec = pl.BlockSpec((nkeys, ph, tm), lambda i, j: (0, 0, i), **ONE_BUFFER)
    in_specs = [
        pl.BlockSpec((tm, d), lambda i, j: (i, 0)),
        pl.BlockSpec((2 * ts, d), lambda i, j: (jnp.minimum(j, nj - 1), 0)),
        pl.BlockSpec((2 * ts, d), lambda i, j: (jnp.maximum(j - 1, 0), 0)),
        key_spec, key_spec, route_spec, route_spec,
        pl.BlockSpec((tm, d), lambda i, j: (i, 0), **ONE_BUFFER),
        pl.BlockSpec((1, N_MOD, d), lambda i, j: (i // bps, 0, 0)),
        pl.BlockSpec((1, d), lambda i, j: (0, 0)),
        pl.BlockSpec((1, d), lambda i, j: (0, 0)),
    ]
    operands = [hbf, u, v, n1t, e1t, r2, e2, x2, mod, ln_g.reshape(1, d), ln_b.reshape(1, d)]
    out_specs = [pl.BlockSpec((tm, d), lambda i, j: (i, 0), **ONE_BUFFER)]
    out_shape = [jax.ShapeDtypeStruct((t, d), F32)]
    if nxt is not None:
        u_all, v_all, layer = nxt
        steps = (t // tm) * (nj + 1)
        per_step = -(-ne // steps)
        rows = -(-per_step // PACKED_ROWS) * PACKED_ROWS
        assert ne % rows == 0, (ne, rows)
        last = ne // rows - 1

        def chunk(i, j):
            return jnp.minimum(i * (nj + 1) + j, last)

        in_specs += [pl.BlockSpec((None, rows, d), lambda i, j: (layer, chunk(i, j), 0))] * 2
        operands += [u_all, v_all]
        out_specs += [pl.BlockSpec((rows, d), lambda i, j: (chunk(i, j), 0))] * 2
        out_shape += [jax.ShapeDtypeStruct((ne, d), BF16)] * 2
    return pl.pallas_call(
        functools.partial(_expert_kernel, alpha=alpha, nkeys=nkeys, ph=ph, nj=nj, ts=ts, cast_next=nxt is not None),
        grid=(t // tm, nj + 1),
        in_specs=in_specs,
        out_specs=out_specs,
        out_shape=out_shape,
        scratch_shapes=[pltpu.VMEM((ts, tm), F32), pltpu.VMEM((ts, tm), F32),
                        pltpu.VMEM((tm, ts), BF16), pltpu.VMEM((tm, ts), BF16)],
        compiler_params=_params("parallel", "arbitrary"),
        name="peer_experts",
    )(*operands)


def kernel(x, c, w_c, b_c, ada_table, ln_tok_g, ln_tok_b, ln_ch_g, ln_ch_b, pool_w, pool_scale,
           fox_w_in, fox_b_f, fox_w_o, peer_w_q, peer_keys, peer_u, peer_v):
    nb, seq, d = x.shape
    depth = ada_table.shape[0]
    alpha = float((2 * depth) ** 0.25)
    nheads = fox_b_f.shape[1]
    dh = d // nheads
    hpad = -(-nheads // LANES) * LANES

    mod_all = _adaln(c, w_c, b_c, ada_table)

    pool_w16 = pool_w.astype(BF16)
    w_in_t = jnp.swapaxes(fox_w_in, 1, 2)
    w_qkv16 = _cast_rows(w_in_t, 3 * d)
    b_f = jnp.pad(fox_b_f, ((0, 0), (0, hpad - nheads)))
    w_o16 = fox_w_o.astype(BF16)
    w_q16 = peer_w_q.astype(BF16)
    keys16 = peer_keys.astype(BF16)
    u16 = v16 = None

    n_mixers = 2
    for i in range(depth):
        mod = mod_all[:, i].reshape(nb, N_MOD, d)
        jl = i // n_mixers
        if i % n_mixers == 0:
            x = _pool_layer(x, mod, pool_w16, jl, pool_scale[jl], ln_tok_g[i], ln_tok_b[i], alpha)
        else:
            x2 = x.reshape(nb * seq, d)
            qkv, f = _fox_proj(x2, mod, w_qkv16, w_in_t, jl, nheads, seq, float(dh) ** -0.5)
            cum = _decay_cumsum(f.reshape(nb, seq, hpad), b_f[jl].reshape(1, hpad))
            ck = cum[:, :, :nheads].transpose(0, 2, 1)[:, :, None, :]
            o = _fox_attention(qkv.reshape(nb, seq, 3 * d), cum, ck, nheads, dh)
            x = _fox_out_layer(o.reshape(nb * seq, d), w_o16, jl, x2, mod, ln_tok_g[i], ln_tok_b[i], seq, alpha)
            x = x.reshape(nb, seq, d)
        x2 = x.reshape(nb * seq, d)
        hbf, n1t, e1t, r2, e2, *first = _peer_router(x2, mod, w_q16, keys16, i, seq, (peer_u, peer_v) if i == 0 else None)
        if first:
            u16, v16 = first
        nxt = (peer_u, peer_v, i + 1) if i + 1 < depth else None
        x, *converted = _peer_experts(hbf, u16, v16, n1t, e1t, r2, e2, x2, mod, ln_ch_g[i], ln_ch_b[i], seq, alpha, nxt)
        if converted:
            u16, v16 = converted
        x = x.reshape(nb, seq, d)
    return x
```

```python
import functools

import jax
import jax.numpy as jnp
from jax import lax
from jax.experimental import pallas as pl
from jax.experimental.pallas import tpu as pltpu

F32 = jnp.float32
BF16 = jnp.bfloat16

LN_EPS = 1e-5
N_MOD = 6
POOL_WINDOWS = (2, 4, 8, 16)
POOL_HALO = 16
PEER_TOPK = 16
SQRT_HALF = 0.7071067811865476
LANES = 128
SUBLANES = 8
PACKED_ROWS = 16
LN_ROWS = 64
VMEM_LIMIT_BYTES = 56 * 1024 * 1024


NT_DIMS = (((1,), (1,)), ((), ()))
ONE_BUFFER = dict(pipeline_mode=pl.Buffered(1))


def _params(*semantics):
    return pltpu.CompilerParams(dimension_semantics=semantics, vmem_limit_bytes=VMEM_LIMIT_BYTES)


def _tile(dim, target, align):
    if dim <= target:
        return dim
    t = (target // align) * align
    while t > align and dim % t:
        t -= align
    assert dim % t == 0, (dim, target, align)
    return t


def _layer_norm(r, g, b):
    mu = jnp.mean(r, axis=-1, keepdims=True)
    xc = r - mu
    var = jnp.mean(xc * xc, axis=-1, keepdims=True)
    return xc * lax.rsqrt(var + LN_EPS) * g + b


def _residual_layer_norm_rows(x_ref, y_ref, gate, g, b, alpha):
    scale = 1.0 + gate

    def body(i, carry):
        rows = pl.ds(pl.multiple_of(i * LN_ROWS, LN_ROWS), LN_ROWS)
        y_ref[rows, :] = _layer_norm(alpha * x_ref[rows, :] + scale * y_ref[rows, :], g, b)
        return carry

    lax.fori_loop(0, y_ref.shape[0] // LN_ROWS, body, 0)


def _adaln_kernel(ct_ref, w_ref, b_ref, ada_ref, o_ref, acc_ref, *, nb, nk):
    k = pl.program_id(1)

    @pl.when(k == 0)
    def _():
        acc_ref[...] = jnp.zeros_like(acc_ref)

    s = jax.nn.silu(ct_ref[...])
    w = w_ref[...]
    tk, tn = w.shape
    for b in range(nb):
        prod = w * s[:, b:b + 1]
        acc_ref[b] += jnp.sum(prod.reshape(tk // SUBLANES, SUBLANES, tn), axis=0)

    @pl.when(k == nk - 1)
    def _():
        for b in range(nb):
            t0 = jnp.sum(acc_ref[b], axis=0, keepdims=True) + b_ref[...]
            o_ref[b] = t0 + ada_ref[...]


def _adaln(c, w_c, b_c, ada_table):
    nb, d = c.shape
    n = w_c.shape[1]
    depth = ada_table.shape[0]
    tk = _tile(d, 512, SUBLANES)
    tn = _tile(n, 2048, LANES)
    nk = d // tk
    return pl.pallas_call(
        functools.partial(_adaln_kernel, nb=nb, nk=nk),
        grid=(n // tn, nk),
        in_specs=[
            pl.BlockSpec((tk, nb), lambda j, k: (k, 0)),
            pl.BlockSpec((tk, tn), lambda j, k: (k, j)),
            pl.BlockSpec((1, tn), lambda j, k: (0, j)),
            pl.BlockSpec((depth, tn), lambda j, k: (0, j)),
        ],
        out_specs=pl.BlockSpec((nb, depth, tn), lambda j, k: (0, 0, j)),
        out_shape=jax.ShapeDtypeStruct((nb, depth, n), F32),
        scratch_shapes=[pltpu.VMEM((nb, SUBLANES, tn), F32)],
        compiler_params=_params("parallel", "arbitrary"),
        name="adaln",
    )(c.T, w_c, b_c.reshape(1, n), ada_table)


def _pool_kernel(x_ref, xp_ref, mod_ref, w_ref, scale_ref, g_ref, b_ref, o_ref, hh_ref, *, alpha, ts, dg):
    s_idx = pl.program_id(1)
    m = mod_ref[0]
    sh, sc, gate = m[0:1], m[1:2], m[2:3]
    x = x_ref[0]
    h = x * (1.0 + sc) + sh
    hp = xp_ref[0] * (1.0 + sc) + sh
    base = SUBLANES + POOL_HALO
    end = base + ts
    hh_ref[0:SUBLANES] = jnp.zeros((SUBLANES, hh_ref.shape[1]), F32)
    hh_ref[SUBLANES:base] = jnp.where(s_idx > 0, hp, 0.0)
    hh_ref[base:] = h
    pos = lax.broadcasted_iota(jnp.int32, (ts, 1), 0) + s_idx * ts
    ys = []
    for gi, win in enumerate(POOL_WINDOWS):
        cols = slice(gi * dg, (gi + 1) * dg)
        sh = 1
        while sh < win:
            hh_ref[SUBLANES:end, cols] = hh_ref[SUBLANES:end, cols] + hh_ref[SUBLANES - sh:end - sh, cols]
            sh *= 2
        acc = hh_ref[base:end, cols]
        cnt = jnp.minimum(pos + 1, win).astype(F32)
        z = acc / cnt - h[:, cols]
        ys.append(jnp.dot(z.astype(BF16), w_ref[gi], preferred_element_type=F32))
    y = jnp.concatenate(ys, axis=1) * scale_ref[...]
    r = alpha * x + (1.0 + gate) * y
    o_ref[0] = _layer_norm(r, g_ref[...], b_ref[...])


def _pool_layer(x, mod, w, layer, scale, ln_g, ln_b, alpha):
    nb, s, d = x.shape
    _, g, dg, _ = w.shape
    ts = _tile(s, 256, POOL_HALO)
    hb = ts // POOL_HALO
    return pl.pallas_call(
        functools.partial(_pool_kernel, alpha=alpha, ts=ts, dg=dg),
        grid=(nb, s // ts),
        in_specs=[
            pl.BlockSpec((1, ts, d), lambda b, i: (b, i, 0)),
            pl.BlockSpec((1, POOL_HALO, d), lambda b, i: (b, jnp.maximum(i * hb - 1, 0), 0)),
            pl.BlockSpec((1, N_MOD, d), lambda b, i: (b, 0, 0)),
            pl.BlockSpec((None, g, dg, dg), lambda b, i: (layer, 0, 0, 0)),
            pl.BlockSpec((1, d), lambda b, i: (0, 0)),
            pl.BlockSpec((1, d), lambda b, i: (0, 0)),
            pl.BlockSpec((1, d), lambda b, i: (0, 0)),
        ],
        out_specs=pl.BlockSpec((1, ts, d), lambda b, i: (b, i, 0)),
        out_shape=jax.ShapeDtypeStruct((nb, s, d), F32),
        scratch_shapes=[pltpu.VMEM((SUBLANES + POOL_HALO + ts, d), F32)],
        compiler_params=_params("parallel", "parallel"),
        name="pool_layer",
    )(x, x, mod, w, scale.reshape(1, d), ln_g.reshape(1, d), ln_b.reshape(1, d))


def _cast_kernel(x_ref, o_ref):
    o_ref[...] = x_ref[...].astype(o_ref.dtype)


def _cast_rows(w, nrows):
    nl, _, cols = w.shape
    tk = _tile(nrows, 512, PACKED_ROWS)
    tn = _tile(cols, 2048, LANES)
    return pl.pallas_call(
        _cast_kernel,
        grid=(nl, nrows // tk, cols // tn),
        in_specs=[pl.BlockSpec((None, tk, tn), lambda l, i, j: (l, i, j))],
        out_specs=pl.BlockSpec((None, tk, tn), lambda l, i, j: (l, i, j)),
        out_shape=jax.ShapeDtypeStruct((nl, nrows, cols), BF16),
        compiler_params=_params("parallel", "parallel", "parallel"),
        name="cast_rows",
    )(w)


def _fox_proj_kernel(x_ref, mod_ref, w_ref, wf_ref, qkv_ref, f_ref, h_ref, *, nq_tiles, sm_scale, nheads):
    j = pl.program_id(1)

    @pl.when(j == 0)
    def _():
        m = mod_ref[0]
        h = x_ref[...] * (1.0 + m[1:2]) + m[0:1]
        h_hi = h.astype(BF16)
        h_ref[...] = h_hi
        row = lax.broadcasted_iota(jnp.int32, wf_ref.shape, 0)
        wf = jnp.where(row < nheads, wf_ref[...], 0.0)
        wf_hi = wf.astype(BF16)
        wf_lo = (wf - wf_hi.astype(F32)).astype(BF16)
        h_lo = (h - h_hi.astype(F32)).astype(BF16)
        f_ref[...] = (lax.dot_general(h_hi, wf_hi, NT_DIMS, preferred_element_type=F32)
                      + lax.dot_general(h_lo, wf_hi, NT_DIMS, preferred_element_type=F32)
                      + lax.dot_general(h_hi, wf_lo, NT_DIMS, preferred_element_type=F32))

    y = lax.dot_general(h_ref[...], w_ref[...], NT_DIMS, preferred_element_type=F32)
    y = y * jnp.where(j < nq_tiles, sm_scale, 1.0)
    qkv_ref[...] = y.astype(BF16)


def _fox_proj(x2, mod, w_qkv_t, w_in_t, layer, nheads, seq, sm_scale):
    t, d = x2.shape
    n = w_qkv_t.shape[1]
    assert n % LANES == 0 and nheads <= LANES, (n, nheads)
    hp = LANES
    tm = _tile(seq, 512, PACKED_ROWS)
    tn = _tile(d, 1024, LANES)
    bps = seq // tm
    return pl.pallas_call(
        functools.partial(_fox_proj_kernel, nq_tiles=d // tn, sm_scale=sm_scale, nheads=nheads),
        grid=(t // tm, n // tn),
        in_specs=[
            pl.BlockSpec((tm, d), lambda i, j: (i, 0)),
            pl.BlockSpec((1, N_MOD, d), lambda i, j: (i // bps, 0, 0)),
            pl.BlockSpec((None, tn, d), lambda i, j: (layer, j, 0)),
            pl.BlockSpec((None, hp, d), lambda i, j: (layer, n // hp, 0)),
        ],
        out_specs=[
            pl.BlockSpec((tm, tn), lambda i, j: (i, j)),
            pl.BlockSpec((tm, hp), lambda i, j: (i, 0)),
        ],
        out_shape=[jax.ShapeDtypeStruct((t, n), BF16), jax.ShapeDtypeStruct((t, hp), F32)],
        scratch_shapes=[pltpu.VMEM((tm, d), BF16)],
        compiler_params=_params("parallel", "arbitrary"),
        name="fox_proj",
    )(x2, mod, w_qkv_t, w_in_t)


def _split3(v):
    p1 = v.astype(BF16)
    r1 = v - p1.astype(F32)
    p2 = r1.astype(BF16)
    p3 = (r1 - p2.astype(F32)).astype(BF16)
    return p1, p2, p3


def _decay_kernel(f_ref, bf_ref, o_ref, *, chunk):
    s = f_ref.shape[1]
    row = lax.broadcasted_iota(jnp.int32, (chunk, chunk), 0)
    col = lax.broadcasted_iota(jnp.int32, (chunk, chunk), 1)
    tri = (col <= row).astype(BF16)
    carry = jnp.zeros((1, f_ref.shape[2]), F32)
    for c in range(s // chunk):
        rows = slice(c * chunk, (c + 1) * chunk)
        lf = jax.nn.log_sigmoid(f_ref[0, rows, :] + bf_ref[...])
        p1, p2, p3 = _split3(lf)
        cs = (jnp.dot(tri, p1, preferred_element_type=F32)
              + jnp.dot(tri, p2, preferred_element_type=F32)
              + jnp.dot(tri, p3, preferred_element_type=F32)) + carry
        o_ref[0, rows, :] = cs
        carry = cs[chunk - 1:chunk, :]


def _decay_cumsum(f, b_f):
    nb, s, hp = f.shape
    chunk = _tile(s, 256, PACKED_ROWS)
    return pl.pallas_call(
        functools.partial(_decay_kernel, chunk=chunk),
        grid=(nb,),
        in_specs=[pl.BlockSpec((1, s, hp), lambda b: (b, 0, 0)), pl.BlockSpec((1, hp), lambda b: (0, 0))],
        out_specs=pl.BlockSpec((1, s, hp), lambda b: (b, 0, 0)),
        out_shape=jax.ShapeDtypeStruct((nb, s, hp), F32),
        compiler_params=_params("parallel"),
        name="fox_decay",
    )(f, b_f)


def _attn_kernel(q_ref, k_ref, v_ref, cq_ref, ck_ref, o_ref, m_ref, l_ref, acc_ref, cqb_ref, *, tq):
    qi = pl.program_id(2)
    nch = tq // LANES
    m_ref[...] = jnp.full_like(m_ref, -jnp.inf)
    l_ref[...] = jnp.zeros_like(l_ref)
    acc_ref[...] = jnp.zeros_like(acc_ref)
    lane = lax.broadcasted_iota(jnp.int32, (tq, LANES), 1)
    mine = jnp.where(lane == pl.program_id(1) % LANES, cq_ref[0], 0.0)
    cqb_ref[...] = jnp.broadcast_to(jnp.sum(mine, axis=1, keepdims=True), cqb_ref.shape)

    def step(off, masked):
        kb = k_ref[0, pl.ds(off, tq), :]
        vb = v_ref[0, pl.ds(off, tq), :]
        ck = ck_ref[0, 0, :, pl.ds(off, tq)]
        s = lax.dot_general(q_ref[0], kb, NT_DIMS, preferred_element_type=F32)
        u = []
        for c in range(nch):
            uc = s[:, c * LANES:(c + 1) * LANES] - ck[:, c * LANES:(c + 1) * LANES]
            if masked:
                row = lax.broadcasted_iota(jnp.int32, (tq, LANES), 0)
                col = lax.broadcasted_iota(jnp.int32, (tq, LANES), 1) + c * LANES
                uc = jnp.where(col <= row, uc, -jnp.inf)
            u.append(uc)
        bm = functools.reduce(jnp.maximum, u)
        cqb = cqb_ref[...]
        m_old = m_ref[...]
        m_new = jnp.maximum(m_old, jnp.max(bm, axis=1, keepdims=True) + cqb)
        a = jnp.exp(m_old - m_new)
        r = cqb - m_new
        p = [jnp.exp(uc + r) for uc in u]
        l_ref[...] = a * l_ref[...] + functools.reduce(jnp.add, p)
        pb = jnp.concatenate([pc.astype(BF16) for pc in p], axis=1)
        acc_ref[...] = a * acc_ref[...] + jnp.dot(pb, vb, preferred_element_type=F32)
        m_ref[...] = m_new

    def body(ki, carry):
        step(pl.multiple_of(ki * tq, tq), False)
        return carry

    lax.fori_loop(0, qi, body, 0)
    step(pl.multiple_of(qi * tq, tq), True)
    l = jnp.sum(l_ref[...], axis=1, keepdims=True)
    o_ref[0] = (acc_ref[...] / l).astype(o_ref.dtype)


def _fox_attention(qkv, cum, ck, nheads, dh):
    nb, s, _ = qkv.shape
    d = nheads * dh
    assert dh == LANES, dh
    tq = _tile(s, 1024, LANES)
    stat = pltpu.VMEM((tq, LANES), F32)
    return pl.pallas_call(
        functools.partial(_attn_kernel, tq=tq),
        grid=(nb, nheads, s // tq),
        in_specs=[
            pl.BlockSpec((1, tq, dh), lambda b, h, i: (b, i, h)),
            pl.BlockSpec((1, s, dh), lambda b, h, i: (b, 0, nheads + h)),
            pl.BlockSpec((1, s, dh), lambda b, h, i: (b, 0, 2 * nheads + h)),
            pl.BlockSpec((1, tq, LANES), lambda b, h, i: (b, i, h // LANES)),
            pl.BlockSpec((1, 1, 1, s), lambda b, h, i: (b, h, 0, 0)),
        ],
        out_specs=pl.BlockSpec((1, tq, dh), lambda b, h, i: (b, i, h)),
        out_shape=jax.ShapeDtypeStruct((nb, s, d), BF16),
        scratch_shapes=[stat, stat, pltpu.VMEM((tq, dh), F32), stat],
        compiler_params=_params("parallel", "parallel", "parallel"),
        name="fox_attention",
    )(qkv, qkv, qkv, cum, ck)


def _out_ln_kernel(o_ref, w_ref, x_ref, mod_ref, g_ref, b_ref, out_ref, *, alpha, tn, nj, gate_row):
    j = pl.program_id(1)
    y = jnp.dot(o_ref[...], w_ref[...], preferred_element_type=F32)
    out_ref[:, pl.ds(pl.multiple_of(j * tn, tn), tn)] = y

    @pl.when(j == nj - 1)
    def _():
        _residual_layer_norm_rows(x_ref, out_ref, mod_ref[0][gate_row:gate_row + 1], g_ref[...], b_ref[...], alpha)


def _fox_out_layer(o2, w_o, layer, x2, mod, ln_g, ln_b, seq, alpha):
    t, d = x2.shape
    tm = _tile(seq, 512, PACKED_ROWS)
    tn = _tile(d, 1024, LANES)
    bps = seq // tm
    nj = d // tn
    return pl.pallas_call(
        functools.partial(_out_ln_kernel, alpha=alpha, tn=tn, nj=nj, gate_row=2),
        grid=(t // tm, nj),
        in_specs=[
            pl.BlockSpec((tm, d), lambda i, j: (i, 0)),
            pl.BlockSpec((None, d, tn), lambda i, j: (layer, 0, j)),
            pl.BlockSpec((tm, d), lambda i, j: (i, 0)),
            pl.BlockSpec((1, N_MOD, d), lambda i, j: (i // bps, 0, 0)),
            pl.BlockSpec((1, d), lambda i, j: (0, 0)),
            pl.BlockSpec((1, d), lambda i, j: (0, 0)),
        ],
        out_specs=pl.BlockSpec((tm, d), lambda i, j: (i, 0), **ONE_BUFFER),
        out_shape=jax.ShapeDtypeStruct((t, d), F32),
        compiler_params=_params("parallel", "arbitrary"),
        name="fox_out_layer",
    )(o2, w_o, x2, mod, ln_g.reshape(1, d), ln_b.reshape(1, d))


def _top_ranked(s, k, exact):
    rows = s.shape[0]
    row = lax.broadcasted_iota(jnp.int32, s.shape, 0)
    rank = jnp.full(s.shape, float(k), F32)
    vals = []
    for r in range(k):
        m = jnp.max(s, axis=0, keepdims=True)
        sel = s == m
        if exact:
            first = jnp.min(jnp.where(sel, row, rows), axis=0, keepdims=True)
            sel = row == first
        rank = jnp.where(sel, float(r), rank)
        vals.append(m)
        s = jnp.where(sel, -jnp.inf, s)
    return jnp.concatenate(vals, axis=0), rank


def _overfull(ranks, topk):
    fk = float(topk)
    counts = [jnp.sum((rk < fk).astype(F32), axis=0, keepdims=True) for rk in ranks]
    return jnp.max(functools.reduce(jnp.maximum, [jnp.abs(c - fk) for c in counts]))


def _key_stage(q, keys_ref, topk, exact):
    dhalf = q.shape[1] // 2
    out = []
    for p in range(2):
        qp = q[:, p * dhalf:(p + 1) * dhalf]
        s = lax.dot_general(keys_ref[p], qp, NT_DIMS, preferred_element_type=F32)
        out.append((s,) + _top_ranked(s, topk, exact))
    return out


def _pair_stage(s1, tv1, rank1, tv2, n1_ref, e1_ref, hd, topk, exact):
    half = topk // 2
    sub = lax.broadcasted_iota(jnp.int32, (8, 1), 0)
    pieces = [tv1[0:1] + tv2]
    for a in range(1, half):
        pieces.append(jnp.where(sub < topk // (a + 1), tv1[a:a + 1] + tv2[0:8], -jnp.inf))
    pieces.append(tv1[half:topk] + tv2[0:1])
    cand = jnp.concatenate(pieces, axis=0)
    cvals, crank = _top_ranked(cand, topk, exact)
    taken = crank < float(topk)
    z = jnp.sum(jnp.where(taken, jnp.exp(cand - cvals[0:1]), 0.0), axis=0, keepdims=True)
    takenf = taken.astype(F32)

    n1 = jnp.zeros_like(rank1)
    start = 0
    for a in range(half):
        size = topk if a == 0 else 8
        n_a = jnp.sum(takenf[start:start + size], axis=0, keepdims=True)
        n1 = jnp.where(rank1 == float(a), n_a, n1)
        start += size
    for a in range(half, topk):
        n1 = jnp.where(rank1 == float(a), takenf[start + a - half:start + a - half + 1], n1)

    n1_ref[:, pl.ds(hd, 1), :] = n1[:, None, :]
    e1_ref[:, pl.ds(hd, 1), :] = (SQRT_HALF * jnp.exp(s1 - tv1[0:1]) / z)[:, None, :]
    return crank


def _router_kernel(x_ref, mod_ref, wq_ref, keys_ref, *rest, topk, cast_weights):
    if cast_weights:
        uw_ref, vw_ref, hbf_ref, n1_ref, e1_ref, r2_ref, e2_ref, u16_ref, v16_ref = rest
        u16_ref[...] = (uw_ref[...] * SQRT_HALF).astype(BF16)
        v16_ref[...] = vw_ref[...].astype(BF16)
    else:
        hbf_ref, n1_ref, e1_ref, r2_ref, e2_ref = rest
    hd = pl.program_id(1)

    @pl.when(hd == 0)
    def _():
        m = mod_ref[0]
        hbf_ref[...] = (x_ref[...] * (1.0 + m[4:5]) + m[3:4]).astype(BF16)

    kh = hbf_ref.shape[1] // 2
    q = (jnp.dot(hbf_ref[:, :kh], wq_ref[:kh, :], preferred_element_type=F32)
         + jnp.dot(hbf_ref[:, kh:], wq_ref[kh:, :], preferred_element_type=F32)).astype(BF16)

    def write_second_half(s2, tv2, rank2):
        r2_ref[0] = rank2
        e2_ref[0] = jnp.exp(s2 - tv2[0:1])

    (s1, tv1, rank1), (s2, tv2, rank2) = _key_stage(q, keys_ref, topk, exact=False)
    key_tie = _overfull([rank1, rank2], topk)
    crank = _pair_stage(s1, tv1, rank1, tv2, n1_ref, e1_ref, hd, topk, exact=False)
    pair_tie = _overfull([crank], topk)
    write_second_half(s2, tv2, rank2)

    @pl.when(key_tie > 0.5)
    def _():
        (s1x, tv1x, rank1x), (s2x, tv2x, rank2x) = _key_stage(q, keys_ref, topk, exact=True)
        _pair_stage(s1x, tv1x, rank1x, tv2x, n1_ref, e1_ref, hd, topk, exact=True)
        write_second_half(s2x, tv2x, rank2x)

    @pl.when(jnp.logical_and(key_tie <= 0.5, pair_tie > 0.5))
    def _():
        _pair_stage(s1, tv1, rank1, tv2, n1_ref, e1_ref, hd, topk, exact=True)


def _peer_router(x2, mod, w_q, keys, layer, seq, weights=None):
    t, d = x2.shape
    _, _, nkeys, dhalf = keys.shape
    ph = w_q.shape[2] // (2 * dhalf)
    tr = _tile(seq, 512, LANES)
    bps = seq // tr
    route = jax.ShapeDtypeStruct((ph, nkeys, t), F32)
    route_spec = pl.BlockSpec((1, nkeys, tr), lambda i, h: (h, 0, i))
    key_major = jax.ShapeDtypeStruct((nkeys, ph, t), F32)
    key_spec = pl.BlockSpec((nkeys, ph, tr), lambda i, h: (0, 0, i))
    in_specs = [
        pl.BlockSpec((tr, d), lambda i, h: (i, 0)),
        pl.BlockSpec((1, N_MOD, d), lambda i, h: (i // bps, 0, 0)),
        pl.BlockSpec((None, d, 2 * dhalf), lambda i, h: (layer, 0, h)),
        pl.BlockSpec((None, 2, nkeys, dhalf), lambda i, h: (layer, 0, 0, 0)),
    ]
    operands = [x2, mod, w_q, keys]
    out_specs = [pl.BlockSpec((tr, d), lambda i, h: (i, 0)), key_spec, key_spec, route_spec, route_spec]
    out_shape = [jax.ShapeDtypeStruct((t, d), BF16), key_major, key_major, route, route]
    if weights is not None:
        ne = weights[0].shape[1]
        steps = (t // tr) * ph
        rows = ne // steps
        assert rows * steps == ne and rows % PACKED_ROWS == 0, (ne, steps)
        in_specs += [pl.BlockSpec((None, rows, d), lambda i, h: (layer, i * ph + h, 0))] * 2
        operands += list(weights)
        out_specs += [pl.BlockSpec((rows, d), lambda i, h: (i * ph + h, 0))] * 2
        out_shape += [jax.ShapeDtypeStruct((ne, d), BF16)] * 2
    return pl.pallas_call(
        functools.partial(_router_kernel, topk=PEER_TOPK, cast_weights=weights is not None),
        grid=(t // tr, ph),
        in_specs=in_specs,
        out_specs=out_specs,
        out_shape=out_shape,
        compiler_params=_params("parallel", "arbitrary"),
        name="peer_router",
    )(*operands)


def _gate_block(a_scr, wt_scr, n1_ref, e1_ref, r2_ref, e2_ref, key0, nkeys, ph):
    ts, tm = a_scr.shape
    for il in range(ts // nkeys):
        i_key = key0 + il
        for lc in range(tm // LANES):
            cols = slice(lc * LANES, (lc + 1) * LANES)
            n1t = n1_ref[i_key, :, cols]
            e1t = e1_ref[i_key, :, cols]
            n1b = [n1t[hd:hd + 1, :] for hd in range(ph)]
            e1b = [e1t[hd:hd + 1, :] for hd in range(ph)]
            tiles = []
            for rc in range(nkeys // SUBLANES):
                rows = slice(rc * SUBLANES, (rc + 1) * SUBLANES)
                w = None
                for hd in range(ph):
                    t = jnp.where(r2_ref[hd, rows, cols] < n1b[hd], e2_ref[hd, rows, cols], 0.0) * e1b[hd]
                    w = t if w is None else w + t
                b = a_scr[il * nkeys + rc * SUBLANES:il * nkeys + (rc + 1) * SUBLANES, cols]
                tiles.append(w * (b * (1.0 + lax.erf(b))))
            blk = jnp.concatenate(tiles, axis=0).astype(BF16)
            wt_scr[cols, il * nkeys:(il + 1) * nkeys] = blk.T


def _expert_kernel(h_ref, u_ref, v_ref, n1_ref, e1_ref, r2_ref, e2_ref, x_ref, mod_ref, g_ref, b_ref, *rest,
                   alpha, nkeys, ph, nj, ts, cast_next):
    if cast_next:
        un_ref, vn_ref, o_ref, un16_ref, vn16_ref, a_e, a_o, wt_e, wt_o = rest
        un16_ref[...] = (un_ref[...] * SQRT_HALF).astype(BF16)
        vn16_ref[...] = vn_ref[...].astype(BF16)
    else:
        o_ref, a_e, a_o, wt_e, wt_o = rest
    j = pl.program_id(1)
    tm, d = o_ref.shape
    gps = ts // nkeys

    @pl.when(j == 0)
    def _():
        o_ref[...] = jnp.zeros_like(o_ref)
        wt_e[...] = jnp.zeros_like(wt_e)
        a_o[...] = jnp.zeros_like(a_o)

    def mm1(rows, a_scr):
        half = tm // 2
        for s in range(2):
            a_scr[:, s * half:(s + 1) * half] = lax.dot_general(
                u_ref[rows, :], h_ref[s * half:(s + 1) * half, :], NT_DIMS, preferred_element_type=F32)

    def mm2(wt_scr, rows):
        half = d // 2
        for s in range(2):
            o_ref[:, s * half:(s + 1) * half] += jnp.dot(
                wt_scr[...], v_ref[rows, s * half:(s + 1) * half], preferred_element_type=F32)

    lo, hi = slice(0, ts), slice(ts, 2 * ts)
    last = 2 * nj - 1
    mm2(wt_e, lo)
    _gate_block(a_o, wt_o, n1_ref, e1_ref, r2_ref, e2_ref, jnp.maximum(2 * j - 1, 0) * gps, nkeys, ph)
    mm1(lo, a_e)
    mm2(wt_o, hi)
    _gate_block(a_e, wt_e, n1_ref, e1_ref, r2_ref, e2_ref, jnp.minimum(2 * j, last) * gps, nkeys, ph)
    mm1(hi, a_o)

    @pl.when(j == nj)
    def _():
        _residual_layer_norm_rows(x_ref, o_ref, mod_ref[0][5:6], g_ref[...], b_ref[...], alpha)


def _peer_experts(hbf, u, v, n1t, e1t, r2, e2, x2, mod, ln_g, ln_b, seq, alpha, nxt=None):
    t, d = x2.shape
    ne = u.shape[0]
    ph, nkeys, _ = r2.shape
    tm = _tile(seq, 512, 2 * LANES)
    ts = _tile(ne // 2, 256, nkeys)
    bps = seq // tm
    nj = ne // (2 * ts)
    route_spec = pl.BlockSpec((ph, nkeys, tm), lambda i, j: (0, 0, i), **ONE_BUFFER)
    key_spec = pl.BlockSpec((nkeys, ph, tm), lambda i, j: (0, 0, i), **ONE_BUFFER)
    in_specs = [
        pl.BlockSpec((tm, d), lambda i, j: (i, 0)),
        pl.BlockSpec((2 * ts, d), lambda i, j: (jnp.minimum(j, nj - 1), 0)),
        pl.BlockSpec((2 * ts, d), lambda i, j: (jnp.maximum(j - 1, 0), 0)),
        key_spec, key_spec, route_spec, route_spec,
        pl.BlockSpec((tm, d), lambda i, j: (i, 0), **ONE_BUFFER),
        pl.BlockSpec((1, N_MOD, d), lambda i, j: (i // bps, 0, 0)),
        pl.BlockSpec((1, d), lambda i, j: (0, 0)),
        pl.BlockSpec((1, d), lambda i, j: (0, 0)),
    ]
    operands = [hbf, u, v, n1t, e1t, r2, e2, x2, mod, ln_g.reshape(1, d), ln_b.reshape(1, d)]
    out_specs = [pl.BlockSpec((tm, d), lambda i, j: (i, 0), **ONE_BUFFER)]
    out_shape = [jax.ShapeDtypeStruct((t, d), F32)]
    if nxt is not None:
        u_all, v_all, layer = nxt
        steps = (t // tm) * (nj + 1)
        per_step = -(-ne // steps)
        rows = -(-per_step // PACKED_ROWS) * PACKED_ROWS
        assert ne % rows == 0, (ne, rows)
        last = ne // rows - 1

        def chunk(i, j):
            return jnp.minimum(i * (nj + 1) + j, last)

        in_specs += [pl.BlockSpec((None, rows, d), lambda i, j: (layer, chunk(i, j), 0))] * 2
        operands += [u_all, v_all]
        out_specs += [pl.BlockSpec((rows, d), lambda i, j: (chunk(i, j), 0))] * 2
        out_shape += [jax.ShapeDtypeStruct((ne, d), BF16)] * 2
    return pl.pallas_call(
        functools.partial(_expert_kernel, alpha=alpha, nkeys=nkeys, ph=ph, nj=nj, ts=ts, cast_next=nxt is not None),
        grid=(t // tm, nj + 1),
        in_specs=in_specs,
        out_specs=out_specs,
        out_shape=out_shape,
        scratch_shapes=[pltpu.VMEM((ts, tm), F32), pltpu.VMEM((ts, tm), F32),
                        pltpu.VMEM((tm, ts), BF16), pltpu.VMEM((tm, ts), BF16)],
        compiler_params=_params("parallel", "arbitrary"),
        name="peer_experts",
    )(*operands)


def kernel(x, c, w_c, b_c, ada_table, ln_tok_g, ln_tok_b, ln_ch_g, ln_ch_b, pool_w, pool_scale,
           fox_w_in, fox_b_f, fox_w_o, peer_w_q, peer_keys, peer_u, peer_v):
    nb, seq, d = x.shape
    depth = ada_table.shape[0]
    alpha = float((2 * depth) ** 0.25)
    nheads = fox_b_f.shape[1]
    dh = d // nheads
    hpad = -(-nheads // LANES) * LANES

    mod_all = _adaln(c, w_c, b_c, ada_table)

    pool_w16 = pool_w.astype(BF16)
    w_in_t = jnp.swapaxes(fox_w_in, 1, 2)
    w_qkv16 = _cast_rows(w_in_t, 3 * d)
    b_f = jnp.pad(fox_b_f, ((0, 0), (0, hpad - nheads)))
    w_o16 = fox_w_o.astype(BF16)
    w_q16 = peer_w_q.astype(BF16)
    keys16 = peer_keys.astype(BF16)
    u16 = v16 = None

    n_mixers = 2
    for i in range(depth):
        mod = mod_all[:, i].reshape(nb, N_MOD, d)
        jl = i // n_mixers
        if i % n_mixers == 0:
            x = _pool_layer(x, mod, pool_w16, jl, pool_scale[jl], ln_tok_g[i], ln_tok_b[i], alpha)
        else:
            x2 = x.reshape(nb * seq, d)
            qkv, f = _fox_proj(x2, mod, w_qkv16, w_in_t, jl, nheads, seq, float(dh) ** -0.5)
            cum = _decay_cumsum(f.reshape(nb, seq, hpad), b_f[jl].reshape(1, hpad))
            ck = cum[:, :, :nheads].transpose(0, 2, 1)[:, :, None, :]
            o = _fox_attention(qkv.reshape(nb, seq, 3 * d), cum, ck, nheads, dh)
            x = _fox_out_layer(o.reshape(nb * seq, d), w_o16, jl, x2, mod, ln_tok_g[i], ln_tok_b[i], seq, alpha)
            x = x.reshape(nb, seq, d)
        x2 = x.reshape(nb * seq, d)
        hbf, n1t, e1t, r2, e2, *first = _peer_router(x2, mod, w_q16, keys16, i, seq, (peer_u, peer_v) if i == 0 else None)
        if first:
            u16, v16 = first
        nxt = (peer_u, peer_v, i + 1) if i + 1 < depth else None
        x, *converted = _peer_experts(hbf, u16, v16, n1t, e1t, r2, e2, x2, mod, ln_ch_g[i], ln_ch_b[i], seq, alpha, nxt)
        if converted:
            u16, v16 = converted
        x = x.reshape(nb, seq, d)
    return x
```
